```python
import math
import jax
import jax.numpy as jnp
from jax import lax
import numpy as np

D_MODEL = 2048
BATCH = 16
SEQ = 256
DEPTH = 4
DEC_BATCH = 4
DEC_SEQ = 4096
PAST_LEN = 256

F32 = jnp.float32
GRID_W = 64
N_MIXERS = 4
D_FF = 5632
NORM_EPS = 1e-6
S5_WIDTH = D_MODEL // 2
S5_GROUP = 16
S5_GROUPS = S5_WIDTH // S5_GROUP
S5_STATE = 64
RG_WIDTH = D_MODEL
RG_BLOCKS = 16
RG_BLOCK = RG_WIDTH // RG_BLOCKS
RG_CONV = 4
RG_C = 8.0
DA_HEADS = 8
DA_DK = D_MODEL // (2 * DA_HEADS)
DA_DV = 2 * DA_DK
Q_BLOCK = 128
ROPE_BASE = 10000.0
ML_WIDTH = 2 * D_MODEL
ML_HEADS = 8
ML_DH = ML_WIDTH // ML_HEADS
ML_CONV = 4
ML_CHUNK = 64

kernel_name = 'hybrid_diffusion_s5_rglru_diffattn_mlstm_step'


def _rms(x, g):
    x32 = x.astype(F32)
    y = x32 * lax.rsqrt(jnp.mean(x32 * x32, axis=-1, keepdims=True) + NORM_EPS)
    return (y * g.astype(F32)).astype(x.dtype)


def _modulate(x, g, shift, scale):
    return _rms(x, g) * (1.0 + scale) + shift


def _swiglu(h, w1, w3, w2):
    return (jax.nn.silu(h @ w1) * (h @ w3)) @ w2


def _ffn_half(x, g, shift, scale, gate, w1, w3, w2):
    return x + 0.5 * gate * _swiglu(_modulate(x, g, shift, scale), w1, w3, w2)


def _conv_centred(x, w, b):
    width = w.shape[0]
    pl = (width - 1) // 2
    L = x.shape[1]
    xp = jnp.pad(x, ((0, 0), (pl, width - 1 - pl), (0, 0)))
    y = b
    for j in range(width):
        y = y + xp[:, j:j + L] * w[j]
    return y


def _real_scan(a, b):
    def comb(e1, e2):
        a1, b1 = e1
        a2, b2 = e2
        return a1 * a2, a2 * b1 + b2
    return lax.associative_scan(comb, (a, b), axis=1)[1]


def _complex_scan(ar, ai, br, bi):
    def comb(e1, e2):
        a1r, a1i, b1r, b1i = e1
        a2r, a2i, b2r, b2i = e2
        return (a1r * a2r - a1i * a2i, a1r * a2i + a1i * a2r,
                a2r * b1r - a2i * b1i + b2r, a2r * b1i + a2i * b1r + b2i)
    _, _, hr, hi = lax.associative_scan(comb, (ar, ai, br, bi), axis=1)
    return hr, hi


def _s5_direction(u, h0r, h0i, a_re, a_im, log_dt, b_re, b_im, c_re, c_im, reverse):
    a_re = a_re.astype(F32)
    a_im = a_im.astype(F32)
    b_re = b_re.astype(F32)
    b_im = b_im.astype(F32)
    dt = jnp.exp(log_dt.astype(F32))[:, None]
    mag = jnp.exp(a_re * dt)
    lr = mag * jnp.cos(a_im * dt)
    li = mag * jnp.sin(a_im * dt)
    den = a_re * a_re + a_im * a_im
    cr = ((lr - 1.0) * a_re + li * a_im) / den
    ci = (li * a_re - (lr - 1.0) * a_im) / den
    bbr = cr[..., None] * b_re - ci[..., None] * b_im
    bbi = cr[..., None] * b_im + ci[..., None] * b_re
    if reverse:
        u = jnp.flip(u, axis=1)
    bur = jnp.einsum('gpc,blgc->blgp', bbr, u)
    bui = jnp.einsum('gpc,blgc->blgp', bbi, u)
    bur = bur.at[:, 0].add(lr * h0r - li * h0i)
    bui = bui.at[:, 0].add(lr * h0i + li * h0r)
    shape = (1, u.shape[1]) + lr.shape
    hr, hi = _complex_scan(jnp.broadcast_to(lr, shape), jnp.broadcast_to(li, shape), bur, bui)
    y = (jnp.einsum('gcp,blgp->blgc', c_re.astype(F32), hr)
         - jnp.einsum('gcp,blgp->blgc', c_im.astype(F32), hi))
    if reverse:
        y = jnp.flip(y, axis=1)
    return y, hr[:, -1], hi[:, -1]


def _s5_mixer(h, state, p):
    (w_in, a_re, a_im, log_dt, b_re, b_im, c_re, c_im, d_skip, w_glu, b_glu, w_out) = p
    B, L, _ = h.shape
    u = (h @ w_in).astype(F32)
    ug = u.reshape(B, L, S5_GROUPS, S5_GROUP)
    y = d_skip.astype(F32) * u
    finals = []
    for d in range(2):
        yd, fr, fi = _s5_direction(ug, state[:, d, 0].astype(F32), state[:, d, 1].astype(F32),
                                   a_re[d], a_im[d], log_dt[d], b_re[d], b_im[d],
                                   c_re[d], c_im[d], reverse=(d == 1))
        y = y + yd.reshape(B, L, S5_WIDTH)
        finals.append(jnp.stack([fr, fi], axis=1))
    z = jax.nn.gelu(y).astype(h.dtype)
    z = z * jax.nn.sigmoid(z @ w_glu + b_glu)
    out = z @ w_out
    return out, jnp.stack(finals, axis=1).astype(h.dtype)


def _rglru_direction(xc, h0, wa, ba, wx, bx, lam, reverse):
    B, L, _ = xc.shape
    if reverse:
        xc = jnp.flip(xc, axis=1)
    xb = xc.reshape(B, L, RG_BLOCKS, RG_BLOCK)
    r = jax.nn.sigmoid(jnp.einsum('blnc,ncd->blnd', xb, wa).reshape(B, L, RG_WIDTH) + ba)
    ig = jax.nn.sigmoid(jnp.einsum('blnc,ncd->blnd', xb, wx).reshape(B, L, RG_WIDTH) + bx)
    log_a = -RG_C * r * jax.nn.softplus(-lam)
    a = jnp.exp(log_a)
    bv = jnp.sqrt(-jnp.expm1(2.0 * log_a)) * (ig * xc)
    bv = bv.at[:, 0].add(a[:, 0] * h0)
    hs = _real_scan(a, bv)
    final = hs[:, -1]
    if reverse:
        hs = jnp.flip(hs, axis=1)
    return hs, final


def _rglru_mixer(h, state, p):
    (w_in, w_gate, conv_w, conv_b, wa, ba, wx, bx, lam, w_out) = p
    xc = _conv_centred(h @ w_in, conv_w, conv_b).astype(F32)
    gate = jax.nn.gelu(h @ w_gate)
    y = jnp.zeros_like(xc)
    finals = []
    for d in range(2):
        hs, f = _rglru_direction(xc, state[:, d].astype(F32), wa[d].astype(F32), ba[d].astype(F32),
                                 wx[d].astype(F32), bx[d].astype(F32), lam[d].astype(F32),
                                 reverse=(d == 1))
        y = y + hs
        finals.append(f)
    out = (y.astype(h.dtype) * gate) @ w_out
    return out, jnp.stack(finals, axis=1).astype(h.dtype)


def _rope_1d(x, pos):
    n = x.shape[-1]
    inv = ROPE_BASE ** (-jnp.arange(0, n, 2, dtype=F32) / n)
    ang = pos[:, None] * inv
    cos = jnp.cos(ang)[:, None, None, :]
    sin = jnp.sin(ang)[:, None, None, :]
    x1, x2 = jnp.split(x.astype(F32), 2, axis=-1)
    return jnp.concatenate([x1 * cos - x2 * sin, x2 * cos + x1 * sin], axis=-1)


def _axial_rope(x):
    L = x.shape[1]
    rows = L // GRID_W
    row = jnp.repeat(jnp.arange(rows, dtype=F32), GRID_W)
    col = jnp.tile(jnp.arange(GRID_W, dtype=F32), rows)
    half = x.shape[-1] // 2
    y = jnp.concatenate([_rope_1d(x[..., :half], row), _rope_1d(x[..., half:], col)], axis=-1)
    return y.astype(x.dtype)


def _attend_blocks(q, k, v):
    B, Lq, H, _, dk = q.shape
    nb = Lq // Q_BLOCK
    scale = 1.0 / math.sqrt(dk)
    qb = jnp.moveaxis(q.reshape(B, nb, Q_BLOCK, H, 2, dk), 1, 0)

    def one(qblk):
        s = jnp.einsum('bqhcd,bkhcd->bhcqk', qblk, k).astype(F32) * scale
        pr = jax.nn.softmax(s, axis=-1).astype(v.dtype)
        return jnp.einsum('bhcqk,bkhe->bqhce', pr, v)

    o = lax.map(one, qb)
    return jnp.moveaxis(o, 0, 1).reshape(B, Lq, H, 2, v.shape[-1])


def _dattn_project(h, wq, wk, wv):
    B, L, _ = h.shape
    q = (h @ wq).reshape(B, L, DA_HEADS, 2, DA_DK)
    k = (h @ wk).reshape(B, L, DA_HEADS, 2, DA_DK)
    v = (h @ wv).reshape(B, L, DA_HEADS, DA_DV)
    return q, k, v


def _dattn_output(o, wo, lam_p, subln_g, lam_init):
    lp = lam_p.astype(F32)
    lam = jnp.exp(jnp.sum(lp[0] * lp[1])) - jnp.exp(jnp.sum(lp[2] * lp[3])) + lam_init
    o32 = o.astype(F32)
    d = o32[..., 0, :] - lam * o32[..., 1, :]
    d = d * lax.rsqrt(jnp.mean(d * d, axis=-1, keepdims=True) + NORM_EPS)
    d = d * subln_g.astype(F32) * (1.0 - lam_init)
    B, L = d.shape[:2]
    return d.reshape(B, L, DA_HEADS * DA_DV).astype(o.dtype) @ wo


def _mlstm_chunk_step(carry, xs):
    C, n, m = carry
    q, k, v, ig, lf = xs
    T = q.shape[2]
    b = jnp.cumsum(lf, axis=-1)
    causal = jnp.tril(jnp.ones((T, T), dtype=bool))
    dlog = jnp.where(causal, b[..., :, None] - b[..., None, :] + ig[..., None, :], -jnp.inf)
    inter = b + m[..., None]
    m_t = jnp.maximum(inter, jnp.max(dlog, axis=-1))
    w_intra = jnp.exp(dlog - m_t[..., None])
    w_inter = jnp.exp(inter - m_t)
    s = jnp.einsum('bhtd,bhsd->bhts', q, k) * w_intra
    num = (jnp.einsum('bhts,bhsd->bhtd', s, v)
           + w_inter[..., None] * jnp.einsum('bhvk,bhtk->bhtv', C, q))
    den = jnp.sum(s, axis=-1) + w_inter * jnp.einsum('bhk,bhtk->bht', n, q)
    h = num / jnp.maximum(jnp.abs(den), jnp.exp(-m_t))[..., None]
    m_new = m_t[..., -1]
    w_old = jnp.exp(b[..., -1] + m - m_new)
    w_in = jnp.exp(b[..., -1:] - b + ig - m_new[..., None])
    C_new = w_old[..., None, None] * C + jnp.einsum('bhsv,bhsk->bhvk', v * w_in[..., None], k)
    n_new = w_old[..., None] * n + jnp.einsum('bhs,bhsk->bhk', w_in, k)
    return (C_new, n_new, m_new), h


def _to_chunks(t):
    B, L = t.shape[:2]
    t = t.reshape((B, L // ML_CHUNK, ML_CHUNK) + t.shape[2:])
    return jnp.swapaxes(jnp.moveaxis(t, 1, 0), 2, 3)


def _mlstm_direction(q, k, v, ig, fg, C0, n0, m0, reverse):
    B, L, H, DH = q.shape
    if reverse:
        q, k, v, ig, fg = (jnp.flip(t, axis=1) for t in (q, k, v, ig, fg))
    lf = jax.nn.log_sigmoid(fg)
    xs = tuple(_to_chunks(t) for t in (q, k, v, ig, lf))
    (C, n, m), hc = lax.scan(_mlstm_chunk_step, (C0, n0, m0), xs)
    h = jnp.moveaxis(jnp.swapaxes(hc, 2, 3), 0, 1).reshape(B, L, H, DH)
    if reverse:
        h = jnp.flip(h, axis=1)
    return h, C, n, m


def _mlstm_mixer(h, C0, n0, m0, p):
    (w_up, conv_w, conv_b, wq, wk, wv, w_gate, b_gate, w_o, b_o, gn_g, skip, w_down) = p
    B, L, _ = h.shape
    xi = h @ w_up
    xc = jax.nn.silu(_conv_centred(xi, conv_w, conv_b))
    q = jnp.einsum('blhd,hde->blhe', xc.reshape(B, L, ML_HEADS, ML_DH), wq)
    k = jnp.einsum('blhd,hde->blhe', xc.reshape(B, L, ML_HEADS, ML_DH), wk) / math.sqrt(ML_DH)
    v = jnp.einsum('blhd,hde->blhe', xi.reshape(B, L, ML_HEADS, ML_DH), wv)
    g = (q.reshape(B, L, ML_WIDTH) @ w_gate[0] + k.reshape(B, L, ML_WIDTH) @ w_gate[1]
         + v.reshape(B, L, ML_WIDTH) @ w_gate[2] + b_gate)
    g = g.astype(F32).reshape(B, L, 2, 2, ML_HEADS)
    q32, k32, v32 = q.astype(F32), k.astype(F32), v.astype(F32)
    cell = jnp.zeros_like(q32)
    Cs, ns, ms = [], [], []
    for d in range(2):
        hd, Cd, nd, md = _mlstm_direction(q32, k32, v32, g[:, :, d, 0], g[:, :, d, 1],
                                          C0[:, d].astype(F32), n0[:, d].astype(F32),
                                          m0[:, d].astype(F32), reverse=(d == 1))
        cell = cell + hd
        Cs.append(Cd)
        ns.append(nd)
        ms.append(md)
    o = jax.nn.sigmoid(h @ w_o + b_o).astype(F32).reshape(B, L, ML_HEADS, ML_DH)
    hc = o * cell
    hn = hc * lax.rsqrt(jnp.mean(hc * hc, axis=-1, keepdims=True) + NORM_EPS)
    y = hn.reshape(B, L, ML_WIDTH).astype(h.dtype) * gn_g + skip * xc
    out = y @ w_down
    return (out, jnp.stack(Cs, axis=1).astype(h.dtype), jnp.stack(ns, axis=1).astype(h.dtype),
            jnp.stack(ms, axis=1).astype(h.dtype))


def setup_inputs(seed: int = 0) -> dict:
    key = jax.random.key(seed)
    ks = iter(jax.random.split(key, 72))

    def nrm(shape, s=1.0):
        return jax.random.normal(next(ks), shape, F32) * s

    D = D_MODEL
    NL = DEPTH
    G, P = S5_GROUPS, S5_STATE
    inp = {}
    inp['x_prompt'] = nrm((BATCH, SEQ, D))
    inp['x_sample'] = nrm((DEC_BATCH, DEC_SEQ, D))
    inp['state_s5'] = nrm((DEC_BATCH, 2, 2, G, P), 0.5)
    inp['state_rglru'] = nrm((DEC_BATCH, 2, RG_WIDTH), 0.5)
    inp['cache_dattn_k'] = nrm((DEC_BATCH, PAST_LEN, DA_HEADS, 2, DA_DK))
    inp['cache_dattn_v'] = nrm((DEC_BATCH, PAST_LEN, DA_HEADS, DA_DV))
    inp['state_mlstm_C'] = nrm((DEC_BATCH, 2, ML_HEADS, ML_DH, ML_DH), 0.05)
    inp['state_mlstm_n'] = nrm((DEC_BATCH, 2, ML_HEADS, ML_DH), 0.05)
    inp['state_mlstm_m'] = nrm((DEC_BATCH, 2, ML_HEADS), 0.5)
    inp['c'] = nrm((DEC_BATCH, D))
    inp['c_ctx'] = nrm((D,))
    inp['ada_w'] = nrm((NL, D, 9 * D), D ** -0.5)
    inp['ada_b'] = nrm((NL, 9 * D), 0.02)
    inp['norm_g'] = 1.0 + nrm((NL, 3, D), 0.02)
    inp['ffn1_w1'] = nrm((NL, D, D_FF), D ** -0.5)
    inp['ffn1_w3'] = nrm((NL, D, D_FF), D ** -0.5)
    inp['ffn1_w2'] = nrm((NL, D_FF, D), D_FF ** -0.5)
    inp['ffn2_w1'] = nrm((NL, D, D_FF), D ** -0.5)
    inp['ffn2_w3'] = nrm((NL, D, D_FF), D ** -0.5)
    inp['ffn2_w2'] = nrm((NL, D_FF, D), D_FF ** -0.5)
    inp['final_norm_g'] = 1.0 + nrm((D,), 0.02)
    inp['s5_w_in'] = nrm((D, S5_WIDTH), D ** -0.5)
    inp['s5_a_re'] = -0.5 + nrm((2, G, P), 0.01)
    inp['s5_a_im'] = math.pi * jnp.arange(P, dtype=F32) + nrm((2, G, P), 0.01)
    inp['s5_log_dt'] = jax.random.uniform(next(ks), (2, G), F32, math.log(1e-3), math.log(1e-1))
    inp['s5_b_re'] = nrm((2, G, P, S5_GROUP), (2 * S5_GROUP) ** -0.5)
    inp['s5_b_im'] = nrm((2, G, P, S5_GROUP), (2 * S5_GROUP) ** -0.5)
    inp['s5_c_re'] = nrm((2, G, S5_GROUP, P), 0.5)
    inp['s5_c_im'] = nrm((2, G, S5_GROUP, P), 0.5)
    inp['s5_d'] = nrm((S5_WIDTH,))
    inp['s5_w_glu'] = nrm((S5_WIDTH, S5_WIDTH), S5_WIDTH ** -0.5)
    inp['s5_b_glu'] = nrm((S5_WIDTH,), 0.02)
    inp['s5_w_out'] = nrm((S5_WIDTH, D), S5_WIDTH ** -0.5)
    inp['rg_w_in'] = nrm((D, RG_WIDTH), D ** -0.5)
    inp['rg_w_gate'] = nrm((D, RG_WIDTH), D ** -0.5)
    inp['rg_conv_w'] = nrm((RG_CONV, RG_WIDTH), RG_CONV ** -0.5)
    inp['rg_conv_b'] = nrm((RG_WIDTH,), 0.02)
    inp['rg_wa'] = nrm((2, RG_BLOCKS, RG_BLOCK, RG_BLOCK), RG_BLOCK ** -0.5)
    inp['rg_ba'] = nrm((2, RG_WIDTH), 0.02)
    inp['rg_wx'] = nrm((2, RG_BLOCKS, RG_BLOCK, RG_BLOCK), RG_BLOCK ** -0.5)
    inp['rg_bx'] = nrm((2, RG_WIDTH), 0.02)
    a_pow = jax.random.uniform(next(ks), (2, RG_WIDTH), F32, 0.9, 0.999)
    s_lam = a_pow ** (1.0 / RG_C)
    inp['rg_lam'] = jnp.log(s_lam) - jnp.log1p(-s_lam)
    inp['rg_w_out'] = nrm((RG_WIDTH, D), RG_WIDTH ** -0.5)
    inp['da_wq'] = nrm((D, 2 * DA_HEADS * DA_DK), D ** -0.5)
    inp['da_wk'] = nrm((D, 2 * DA_HEADS * DA_DK), D ** -0.5)
    inp['da_wv'] = nrm((D, DA_HEADS * DA_DV), D ** -0.5)
    inp['da_wo'] = nrm((DA_HEADS * DA_DV, D), (DA_HEADS * DA_DV) ** -0.5)
    inp['da_lam'] = nrm((4, DA_DK), 0.1)
    inp['da_subln_g'] = 1.0 + nrm((DA_DV,), 0.02)
    inp['ml_w_up'] = nrm((D, ML_WIDTH), D ** -0.5)
    inp['ml_conv_w'] = nrm((ML_CONV, ML_WIDTH), ML_CONV ** -0.5)
    inp['ml_conv_b'] = nrm((ML_WIDTH,), 0.02)
    inp['ml_wq'] = nrm((ML_HEADS, ML_DH, ML_DH), ML_DH ** -0.5)
    inp['ml_wk'] = nrm((ML_HEADS, ML_DH, ML_DH), ML_DH ** -0.5)
    inp['ml_wv'] = nrm((ML_HEADS, ML_DH, ML_DH), ML_DH ** -0.5)
    inp['ml_w_gate'] = nrm((3, ML_WIDTH, 4 * ML_HEADS), 0.5 * (3 * ML_WIDTH) ** -0.5)
    ig_b = nrm((2, 1, ML_HEADS), 0.1)
    fg_b = jnp.linspace(3.0, 6.0, ML_HEADS, dtype=F32)[None, None, :] + nrm((2, 1, ML_HEADS), 0.1)
    inp['ml_b_gate'] = jnp.concatenate([ig_b, fg_b], axis=1).reshape(4 * ML_HEADS)
    inp['ml_w_o'] = nrm((D, ML_WIDTH), D ** -0.5)
    inp['ml_b_o'] = nrm((ML_WIDTH,), 0.02)
    inp['ml_gn'] = 1.0 + nrm((ML_WIDTH,), 0.02)
    inp['ml_skip'] = 1.0 + nrm((ML_WIDTH,), 0.02)
    inp['ml_w_down'] = nrm((ML_WIDTH, D), ML_WIDTH ** -0.5)
    return inp


def reference(x_prompt, x_sample, state_s5, state_rglru, cache_dattn_k, cache_dattn_v,
              state_mlstm_C, state_mlstm_n, state_mlstm_m, c, c_ctx,
              ada_w, ada_b, norm_g, ffn1_w1, ffn1_w3, ffn1_w2, ffn2_w1, ffn2_w3, ffn2_w2,
              final_norm_g,
              s5_w_in, s5_a_re, s5_a_im, s5_log_dt, s5_b_re, s5_b_im, s5_c_re, s5_c_im,
              s5_d, s5_w_glu, s5_b_glu, s5_w_out,
              rg_w_in, rg_w_gate, rg_conv_w, rg_conv_b, rg_wa, rg_ba, rg_wx, rg_bx, rg_lam,
              rg_w_out,
              da_wq, da_wk, da_wv, da_wo, da_lam, da_subln_g,
              ml_w_up, ml_conv_w, ml_conv_b, ml_wq, ml_wk, ml_wv, ml_w_gate, ml_b_gate,
              ml_w_o, ml_b_o, ml_gn, ml_skip, ml_w_down):
    s5_p = (s5_w_in, s5_a_re, s5_a_im, s5_log_dt, s5_b_re, s5_b_im, s5_c_re, s5_c_im,
            s5_d, s5_w_glu, s5_b_glu, s5_w_out)
    rg_p = (rg_w_in, rg_w_gate, rg_conv_w, rg_conv_b, rg_wa, rg_ba, rg_wx, rg_bx, rg_lam, rg_w_out)
    ml_p = (ml_w_up, ml_conv_w, ml_conv_b, ml_wq, ml_wk, ml_wv, ml_w_gate, ml_b_gate,
            ml_w_o, ml_b_o, ml_gn, ml_skip, ml_w_down)
    ctx, lat = x_prompt, x_sample
    Bc = ctx.shape[0]
    sc = jax.nn.silu(c_ctx)[None, :]
    sl = jax.nn.silu(c)
    for i in range(DEPTH):
        mc = jnp.split((sc @ ada_w[i] + ada_b[i])[:, None, :], 9, axis=-1)
        mlt = jnp.split((sl @ ada_w[i] + ada_b[i])[:, None, :], 9, axis=-1)
        ctx = _ffn_half(ctx, norm_g[i, 0], mc[0], mc[1], mc[2], ffn1_w1[i], ffn1_w3[i], ffn1_w2[i])
        lat = _ffn_half(lat, norm_g[i, 0], mlt[0], mlt[1], mlt[2], ffn1_w1[i], ffn1_w3[i], ffn1_w2[i])
        hc = _modulate(ctx, norm_g[i, 1], mc[3], mc[4])
        hl = _modulate(lat, norm_g[i, 1], mlt[3], mlt[4])
        kind = i % N_MIXERS
        if kind == 0:
            zero_s5 = jnp.zeros((Bc, 2, 2, S5_GROUPS, S5_STATE), hc.dtype)
            oc, new_s5 = _s5_mixer(hc, zero_s5, s5_p)
            ol, _ = _s5_mixer(hl, state_s5, s5_p)
        elif kind == 1:
            zero_rg = jnp.zeros((Bc, 2, RG_WIDTH), hc.dtype)
            oc, new_rg = _rglru_mixer(hc, zero_rg, rg_p)
            ol, _ = _rglru_mixer(hl, state_rglru, rg_p)
        elif kind == 2:
            lam_init = 0.8 - 0.6 * math.exp(-0.3 * i)
            qc, kc, vc = _dattn_project(hc, da_wq, da_wk, da_wv)
            oc = _dattn_output(_attend_blocks(qc, kc, vc), da_wo, da_lam, da_subln_g, lam_init)
            new_k, new_v = kc, vc
            ql, kl, vl = _dattn_project(hl, da_wq, da_wk, da_wv)
            ql = _axial_rope(ql)
            kl = _axial_rope(kl)
            k_all = jnp.concatenate([cache_dattn_k.astype(kl.dtype), kl], axis=1)
            v_all = jnp.concatenate([cache_dattn_v.astype(vl.dtype), vl], axis=1)
            ol = _dattn_output(_attend_blocks(ql, k_all, v_all), da_wo, da_lam, da_subln_g, lam_init)
        else:
            C0 = jnp.zeros((Bc, 2, ML_HEADS, ML_DH, ML_DH), hc.dtype)
            n0 = jnp.zeros((Bc, 2, ML_HEADS, ML_DH), hc.dtype)
            m0 = jnp.zeros((Bc, 2, ML_HEADS), hc.dtype)
            oc, new_C, new_n, new_m = _mlstm_mixer(hc, C0, n0, m0, ml_p)
            ol, _, _, _ = _mlstm_mixer(hl, state_mlstm_C, state_mlstm_n, state_mlstm_m, ml_p)
        ctx = ctx + mc[5] * oc
        lat = lat + mlt[5] * ol
        ctx = _ffn_half(ctx, norm_g[i, 2], mc[6], mc[7], mc[8], ffn2_w1[i], ffn2_w3[i], ffn2_w2[i])
        lat = _ffn_half(lat, norm_g[i, 2], mlt[6], mlt[7], mlt[8], ffn2_w1[i], ffn2_w3[i], ffn2_w2[i])
    y_prompt = _rms(ctx, final_norm_g)
    y_sample = _rms(lat, final_norm_g)
    return (y_prompt, y_sample, new_s5, new_rg, new_k, new_v, new_C, new_n, new_m)
```

```python
import functools
import math

import jax
import jax.numpy as jnp
from jax import lax
from jax.experimental import pallas as pl
from jax.experimental.pallas import tpu as pltpu

F32 = jnp.float32
BF16 = jnp.bfloat16

NORM_EPS = 1e-6
VMEM_LIMIT_BYTES = 56 * 1024 * 1024
MOD_ROWS = 8


def _cparams(*sem):
    return pltpu.CompilerParams(dimension_semantics=sem, vmem_limit_bytes=VMEM_LIMIT_BYTES)


def _bdot(a, b):
    return jnp.dot(a.astype(BF16), b.astype(BF16), preferred_element_type=F32)


def _mod_row(ref, row):
    return ref[pl.ds(row, 1), :]


class Stream:
    def __init__(self, batch, seq, row0, per_batch_mod):
        self.batch, self.seq, self.row0 = batch, seq, row0
        self.n = batch * seq
        self.rows_per_mod = seq if per_batch_mod else self.n

    def mod_row(self, first_token):
        return self.row0 + first_token // self.rows_per_mod


def _adaln_kernel(c_ref, w_ref, b_ref, o_ref):
    c = c_ref[...]
    s = c * jax.nn.sigmoid(c)
    o_ref[...] = _bdot(s, w_ref[...]) + b_ref[...]


def adaln(c_all, ada_w, ada_b, *, tn=1024):
    nl, d, m = ada_w.shape
    return pl.pallas_call(
        _adaln_kernel,
        out_shape=jax.ShapeDtypeStruct((nl, MOD_ROWS, m), F32),
        grid=(nl, m // tn),
        in_specs=[pl.BlockSpec((MOD_ROWS, d), lambda l, j: (0, 0)),
                  pl.BlockSpec((None, d, tn), lambda l, j: (l, 0, j)),
                  pl.BlockSpec((None, 1, tn), lambda l, j: (l, 0, j))],
        out_specs=pl.BlockSpec((None, MOD_ROWS, tn), lambda l, j: (l, 0, j)),
        compiler_params=_cparams("parallel", "parallel"),
        name="adaln",
    )(c_all, ada_w, ada_b.reshape(nl, 1, m))


def _mod_spec(d, layer, k):
    return pl.BlockSpec((None, MOD_ROWS, d), lambda *_: (layer, 0, k))


def _rms(x, g):
    return x * lax.rsqrt(jnp.mean(x * x, axis=-1, keepdims=True) + NORM_EPS) * g


def _modnorm_kernel(x_ref, g_ref, sh_ref, sc_ref, o_ref, *, stream, tm):
    row = stream.mod_row(pl.program_id(0) * tm)
    y = _rms(x_ref[...], g_ref[...])
    o_ref[...] = (y * (1.0 + _mod_row(sc_ref, row)) + _mod_row(sh_ref, row)).astype(o_ref.dtype)


def modnorm(x, g, mods, layer, k_shift, stream, *, tm=512):
    n, d = x.shape
    return pl.pallas_call(
        functools.partial(_modnorm_kernel, stream=stream, tm=tm),
        out_shape=jax.ShapeDtypeStruct((n, d), BF16),
        grid=(n // tm,),
        in_specs=[pl.BlockSpec((tm, d), lambda i: (i, 0)),
                  pl.BlockSpec((1, d), lambda i: (0, 0)),
                  _mod_spec(d, layer, k_shift),
                  _mod_spec(d, layer, k_shift + 1)],
        out_specs=pl.BlockSpec((tm, d), lambda i: (i, 0)),
        compiler_params=_cparams("parallel"),
        name="modnorm",
    )(x, g.reshape(1, d), mods, mods)


def _rmsnorm_kernel(x_ref, g_ref, o_ref):
    o_ref[...] = _rms(x_ref[...], g_ref[...])


def rmsnorm(x, g, *, tm=512):
    n, d = x.shape
    return pl.pallas_call(
        _rmsnorm_kernel,
        out_shape=jax.ShapeDtypeStruct((n, d), F32),
        grid=(n // tm,),
        in_specs=[pl.BlockSpec((tm, d), lambda i: (i, 0)), pl.BlockSpec((1, d), lambda i: (0, 0))],
        out_specs=pl.BlockSpec((tm, d), lambda i: (i, 0)),
        compiler_params=_cparams("parallel"),
        name="rmsnorm",
    )(x, g.reshape(1, d))


def _ffn_kernel(x_ref, g_ref, sh_ref, sc_ref, gt_ref, w1_ref, w3_ref, w2_ref, o_ref, h_ref, acc_ref,
                *, stream, tm):
    f = pl.program_id(1)
    row = stream.mod_row(pl.program_id(0) * tm)

    @pl.when(f == 0)
    def _():
        y = _rms(x_ref[...], g_ref[...])
        h_ref[...] = (y * (1.0 + _mod_row(sc_ref, row)) + _mod_row(sh_ref, row)).astype(BF16)
        acc_ref[...] = jnp.zeros_like(acc_ref)

    h = h_ref[...]
    a = jnp.dot(h, w1_ref[...], preferred_element_type=F32)
    b = jnp.dot(h, w3_ref[...], preferred_element_type=F32)
    u = (a * jax.nn.sigmoid(a)) * b
    acc_ref[...] += jnp.dot(u.astype(BF16), w2_ref[...], preferred_element_type=F32)

    @pl.when(f == pl.num_programs(1) - 1)
    def _():
        o_ref[...] = x_ref[...] + (0.5 * _mod_row(gt_ref, row)) * acc_ref[...]


def ffn_half(x, g, mods, layer, k_shift, w1, w3, w2, stream, *, tm=512, tf=512):
    n, d = x.shape
    dff = w1.shape[1]
    return pl.pallas_call(
        functools.partial(_ffn_kernel, stream=stream, tm=tm),
        out_shape=jax.ShapeDtypeStruct((n, d), F32),
        grid=(n // tm, dff // tf),
        in_specs=[pl.BlockSpec((tm, d), lambda i, f: (i, 0)),
                  pl.BlockSpec((1, d), lambda i, f: (0, 0)),
                  _mod_spec(d, layer, k_shift),
                  _mod_spec(d, layer, k_shift + 1),
                  _mod_spec(d, layer, k_shift + 2),
                  pl.BlockSpec((d, tf), lambda i, f: (0, f)),
                  pl.BlockSpec((d, tf), lambda i, f: (0, f)),
                  pl.BlockSpec((tf, d), lambda i, f: (f, 0))],
        out_specs=pl.BlockSpec((tm, d), lambda i, f: (i, 0)),
        scratch_shapes=[pltpu.VMEM((tm, d), BF16), pltpu.VMEM((tm, d), F32)],
        compiler_params=_cparams("parallel", "arbitrary"),
        name="ffn_half",
    )(x, g.reshape(1, d), mods, mods, mods, w1, w3, w2)


def _mm_kernel(x_ref, w_ref, *rest, epilogue, n_extra):
    extra, o_refs = rest[:n_extra], rest[n_extra:]
    acc = _bdot(x_ref[...], w_ref[...])
    outs = epilogue(acc, *extra) if epilogue is not None else acc
    if not isinstance(outs, tuple):
        outs = (outs,)
    for o_ref, o in zip(o_refs, outs):
        o_ref[...] = o.astype(o_ref.dtype)


def matmul(x, w, *, tm, tn, out_dtype=F32, epilogue=None, extras=(), k_block=None, name="matmul"):
    n, kx = x.shape
    if k_block is None:
        kb, m = kx, w.shape[1]
        x_map = lambda i, j: (i, 0)
        w_map = lambda i, j: (0, j)
    else:
        kb = k_block
        per = w.shape[1] // tn
        m = (kx // kb) * w.shape[1]
        x_map = lambda i, j: (i, j // per)
        w_map = lambda i, j: (j // per, j % per)
    dts = out_dtype if isinstance(out_dtype, tuple) else (out_dtype,)
    out_shape = tuple(jax.ShapeDtypeStruct((n, m), dt) for dt in dts)
    out_specs = tuple(pl.BlockSpec((tm, tn), lambda i, j: (i, j)) for _ in dts)
    res = pl.pallas_call(
        functools.partial(_mm_kernel, epilogue=epilogue, n_extra=len(extras)),
        out_shape=out_shape,
        grid=(n // tm, m // tn),
        in_specs=[pl.BlockSpec((tm, kb), x_map), pl.BlockSpec((kb, tn), w_map)]
        + [pl.BlockSpec(bs, im) for _, bs, im in extras],
        out_specs=out_specs,
        compiler_params=_cparams("parallel", "parallel"),
        name=name,
    )(x, w, *[a for a, _, _ in extras])
    return res if isinstance(out_dtype, tuple) else res[0]


S5_CHUNK = 16
S5_GROUPS_PER_STEP = 8


def _s5_operators(a_re, a_im, log_dt, b_re, b_im, c_re, c_im):
    hp = lax.Precision.HIGHEST
    t = S5_CHUNK
    a_re, a_im, b_re, b_im, c_re, c_im = (z.astype(F32) for z in (a_re, a_im, b_re, b_im, c_re, c_im))
    dt = jnp.exp(log_dt.astype(F32))[..., None]
    mag = jnp.exp(a_re * dt)
    lr = mag * jnp.cos(a_im * dt)
    li = mag * jnp.sin(a_im * dt)
    den = a_re * a_re + a_im * a_im
    cr = ((lr - 1.0) * a_re + li * a_im) / den
    ci = (li * a_re - (lr - 1.0) * a_im) / den
    bbr = cr[..., None] * b_re - ci[..., None] * b_im
    bbi = cr[..., None] * b_im + ci[..., None] * b_re
    k = jnp.arange(t + 1, dtype=F32)[None, None, :, None]
    pmag = jnp.exp(k * (a_re * dt)[:, :, None, :])
    pr = pmag * jnp.cos(k * (a_im * dt)[:, :, None, :])
    pi = pmag * jnp.sin(k * (a_im * dt)[:, :, None, :])
    pbr = pr[..., None] * bbr[:, :, None] - pi[..., None] * bbi[:, :, None]
    pbi = pr[..., None] * bbi[:, :, None] + pi[..., None] * bbr[:, :, None]
    m = (jnp.einsum('dgop,dgkpi->dgkoi', c_re, pbr, precision=hp)
         - jnp.einsum('dgop,dgkpi->dgkoi', c_im, pbi, precision=hp))
    m = m.at[:, :, t].set(0.0)
    s_idx = jnp.arange(t)[:, None]
    t_idx = jnp.arange(t)[None, :]
    lag_f = jnp.where(t_idx >= s_idx, t_idx - s_idx, t)
    lag_b = jnp.where(s_idx >= t_idx, s_idx - t_idx, t)

    def toeplitz(md, lag):
        x = md[:, lag]
        return x.transpose(0, 1, 4, 2, 3).reshape(md.shape[0], t * x.shape[-1], t * x.shape[-2])

    toep = toeplitz(m[0], lag_f) + toeplitz(m[1], lag_b)
    pow_s = jnp.stack([t - 1 - jnp.arange(t), jnp.arange(t)])

    def smat(d):
        r = pbr[d][:, pow_s[d]]
        i = pbi[d][:, pow_s[d]]
        x = jnp.concatenate([r, i], axis=2)
        return x.transpose(0, 1, 3, 2).reshape(x.shape[0], t * x.shape[3], x.shape[2])

    s_op = jnp.stack([smat(0), smat(1)])
    pow_w = jnp.stack([jnp.arange(t) + 1, t - jnp.arange(t)])

    def wmat(d):
        zr = (c_re[d][:, None] * pr[d][:, pow_w[d]][:, :, None, :]
              - c_im[d][:, None] * pi[d][:, pow_w[d]][:, :, None, :])
        zi = (c_re[d][:, None] * pi[d][:, pow_w[d]][:, :, None, :]
              + c_im[d][:, None] * pr[d][:, pow_w[d]][:, :, None, :])
        x = jnp.concatenate([zr, -zi], axis=3)
        return x.transpose(0, 3, 1, 2).reshape(x.shape[0], x.shape[3], t * x.shape[2])

    w_op = jnp.stack([wmat(0), wmat(1)])
    mul_a = jnp.concatenate([pr[:, :, t], pr[:, :, t]], axis=-1)
    mul_b = jnp.concatenate([-pi[:, :, t], pi[:, :, t]], axis=-1)
    return toep.astype(BF16), s_op.astype(BF16), w_op.astype(BF16), mul_a, mul_b


def _s5_intra_kernel(u_ref, toep_ref, s_ref, y_ref, st_ref):
    for g in range(u_ref.shape[0]):
        u = u_ref[g]
        y_ref[g] = jnp.dot(u, toep_ref[g], preferred_element_type=F32)
        for d in range(2):
            st_ref[d, g] = jnp.dot(u, s_ref[d, g], preferred_element_type=F32)


def _s5_recur_kernel(st_ref, a_ref, b_ref, h0_ref, hin_ref, fin_ref, *, n_chunks):
    rev = pl.program_id(0) == 1
    half = a_ref.shape[-1] // 2
    a = a_ref[...]
    b = b_ref[...]

    def body(i, h):
        j = jnp.where(rev, n_chunks - 1 - i, i)
        hin_ref[j] = h
        return h * a + pltpu.roll(h, half, axis=2) * b + st_ref[j]

    fin_ref[...] = lax.fori_loop(0, n_chunks, body, h0_ref[...])


def _s5_inter_kernel(y_ref, hin_ref, w_ref, o_ref):
    for g in range(y_ref.shape[0]):
        acc = y_ref[g]
        for d in range(2):
            acc = acc + jnp.dot(hin_ref[d, g].astype(BF16), w_ref[d, g], preferred_element_type=F32)
        o_ref[g] = acc


def s5_scan(u, h0, ops, batch, seq):
    toep, s_op, w_op, mul_a, mul_b = ops
    ng, tc, _ = toep.shape
    t = S5_CHUNK
    c = tc // t
    p2 = s_op.shape[-1]
    nj = seq // t
    rows = nj * batch
    gs = S5_GROUPS_PER_STEP
    ug = u.astype(BF16).reshape(batch, nj, t, ng, c).transpose(3, 1, 0, 2, 4).reshape(ng, rows, tc)
    y_intra, st = pl.pallas_call(
        _s5_intra_kernel,
        out_shape=(jax.ShapeDtypeStruct((ng, rows, tc), F32), jax.ShapeDtypeStruct((2, ng, rows, p2), F32)),
        grid=(ng // gs,),
        in_specs=[pl.BlockSpec((gs, rows, tc), lambda g: (g, 0, 0)),
                  pl.BlockSpec((gs, tc, tc), lambda g: (g, 0, 0)),
                  pl.BlockSpec((2, gs, tc, p2), lambda g: (0, g, 0, 0))],
        out_specs=(pl.BlockSpec((gs, rows, tc), lambda g: (g, 0, 0)),
                   pl.BlockSpec((2, gs, rows, p2), lambda g: (0, g, 0, 0))),
        compiler_params=_cparams("parallel"),
        name="s5_intra",
    )(ug, toep, s_op)
    st5 = st.reshape(2, ng, nj, batch, p2).transpose(0, 2, 1, 3, 4)
    hin, fin = pl.pallas_call(
        functools.partial(_s5_recur_kernel, n_chunks=nj),
        out_shape=(jax.ShapeDtypeStruct((2, nj, ng, batch, p2), F32),
                   jax.ShapeDtypeStruct((2, ng, batch, p2), F32)),
        grid=(2, ng // gs),
        in_specs=[pl.BlockSpec((None, nj, gs, batch, p2), lambda d, g: (d, 0, g, 0, 0)),
                  pl.BlockSpec((None, gs, 1, p2), lambda d, g: (d, g, 0, 0)),
                  pl.BlockSpec((None, gs, 1, p2), lambda d, g: (d, g, 0, 0)),
                  pl.BlockSpec((None, gs, batch, p2), lambda d, g: (d, g, 0, 0))],
        out_specs=(pl.BlockSpec((None, nj, gs, batch, p2), lambda d, g: (d, 0, g, 0, 0)),
                   pl.BlockSpec((None, gs, batch, p2), lambda d, g: (d, g, 0, 0))),
        compiler_params=_cparams("parallel", "parallel"),
        name="s5_recur",
    )(st5, mul_a.reshape(2, ng, 1, p2), mul_b.reshape(2, ng, 1, p2), h0)
    hin = hin.transpose(0, 2, 1, 3, 4).reshape(2, ng, rows, p2)
    yg = pl.pallas_call(
        _s5_inter_kernel,
        out_shape=jax.ShapeDtypeStruct((ng, rows, tc), F32),
        grid=(ng // gs,),
        in_specs=[pl.BlockSpec((gs, rows, tc), lambda g: (g, 0, 0)),
                  pl.BlockSpec((2, gs, rows, p2), lambda g: (0, g, 0, 0)),
                  pl.BlockSpec((2, gs, p2, tc), lambda g: (0, g, 0, 0))],
        out_specs=pl.BlockSpec((gs, rows, tc), lambda g: (g, 0, 0)),
        compiler_params=_cparams("parallel"),
        name="s5_inter",
    )(y_intra, hin, w_op)
    y = yg.reshape(ng, nj, batch, t, c).transpose(2, 1, 3, 0, 4).reshape(batch * seq, ng * c)
    return y, fin


def _s5_post_kernel(u_ref, y_ref, d_ref, o_ref):
    y = d_ref[...] * u_ref[...] + y_ref[...]
    o_ref[...] = jax.nn.gelu(y)


def s5_post(u, y, d_skip, *, tm=512):
    n, w = u.shape
    return pl.pallas_call(
        _s5_post_kernel,
        out_shape=jax.ShapeDtypeStruct((n, w), F32),
        grid=(n // tm,),
        in_specs=[pl.BlockSpec((tm, w), lambda i: (i, 0)), pl.BlockSpec((tm, w), lambda i: (i, 0)),
                  pl.BlockSpec((1, w), lambda i: (0, 0))],
        out_specs=pl.BlockSpec((tm, w), lambda i: (i, 0)),
        compiler_params=_cparams("parallel"),
        name="s5_post",
    )(u, y, d_skip.reshape(1, w).astype(F32))


def _residual_epilogue(acc, x_ref, gt_ref, *, stream, tm):
    row = stream.mod_row(pl.program_id(0) * tm)
    return x_ref[...] + _mod_row(gt_ref, row) * acc


def out_proj_residual(z, w, x, mods, layer, stream, *, tm=512, tn=512, name="out_proj"):
    d = x.shape[1]
    return matmul(z, w, tm=tm, tn=tn, name=name,
                  epilogue=functools.partial(_residual_epilogue, stream=stream, tm=tm),
                  extras=[(x, (tm, tn), lambda i, j: (i, j)),
                          (mods, (None, MOD_ROWS, tn), lambda i, j: (layer, 0, 5 * (d // tn) + j))])


def _glu_epilogue(acc, z_ref, b_ref):
    return z_ref[...] * jax.nn.sigmoid(acc + b_ref[...])


def s5_mixer(h, x, mods, layer, stream, state, w_in, ops, d_skip, w_glu, b_glu, w_out):
    u = matmul(h, w_in, tm=512, tn=512, name="s5_in")
    y, fin = s5_scan(u, state, ops, stream.batch, stream.seq)
    z = s5_post(u, y, d_skip)
    w = z.shape[1]
    zz = matmul(z, w_glu, tm=512, tn=512, out_dtype=BF16, name="s5_glu", epilogue=_glu_epilogue,
                extras=[(z, (512, 512), lambda i, j: (i, j)),
                        (b_glu.reshape(1, w).astype(F32), (1, 512), lambda i, j: (0, j))])
    return out_proj_residual(zz, w_out, x, mods, layer, stream, name="s5_out"), fin


SUBLANES = 8
LANES = 128
SCAN_BLOCK = 256


def _conv_rows(x_ref, t0, tb, seq, cw, cb):
    x = x_ref[pl.ds(t0, tb), :]
    prev = x_ref[pl.ds(pl.multiple_of(jnp.maximum(t0 - SUBLANES, 0), SUBLANES), SUBLANES), :]
    nxt = x_ref[pl.ds(pl.multiple_of(jnp.minimum(t0 + tb, seq - SUBLANES), SUBLANES), SUBLANES), :]
    has_prev = t0 > 0
    has_next = t0 + tb < seq
    p1 = jnp.where(has_prev, prev[SUBLANES - 1:SUBLANES, :], 0.0)
    n0 = jnp.where(has_next, nxt[0:1, :], 0.0)
    n1 = jnp.where(has_next, nxt[1:2, :], 0.0)
    rows = lax.broadcasted_iota(jnp.int32, x.shape, 0)
    xm1 = jnp.where(rows == 0, p1, pltpu.roll(x, 1, axis=0))
    xp1 = jnp.where(rows == tb - 1, n0, pltpu.roll(x, tb - 1, axis=0))
    xp2 = jnp.where(rows == tb - 1, n1, jnp.where(rows == tb - 2, n0, pltpu.roll(x, tb - 2, axis=0)))
    return cb + xm1 * cw[0:1, :] + x * cw[1:2, :] + xp1 * cw[2:3, :] + xp2 * cw[3:4, :]


def _group_scan(a, b, reverse):
    tb = a.shape[0]
    pos = lax.broadcasted_iota(jnp.int32, a.shape, 0) % SUBLANES
    s = 1
    while s < SUBLANES:
        if reverse:
            ok = pos < SUBLANES - s
            a_s, b_s = pltpu.roll(a, tb - s, axis=0), pltpu.roll(b, tb - s, axis=0)
        else:
            ok = pos >= s
            a_s, b_s = pltpu.roll(a, s, axis=0), pltpu.roll(b, s, axis=0)
        b = b + a * jnp.where(ok, b_s, 0.0)
        a = a * jnp.where(ok, a_s, 1.0)
        s *= 2
    return a, b


def _softplus(x):
    return jnp.maximum(x, 0.0) + jnp.log1p(jnp.exp(-jnp.abs(x)))


RG_C = 8.0


def _rg_core_kernel(x_ref, gate_ref, cw_ref, cb_ref, w_ref, bias_ref, lam_ref, h0_ref, o_ref, fin_ref,
                    y_ref, a_ref, b_ref, *, seq):
    tb = a_ref.shape[0]
    tc = a_ref.shape[1]
    n_blocks = seq // tb
    n_groups = tb // SUBLANES
    cw = cw_ref[...]
    cb = cb_ref[...]
    for d in range(2):
        reverse = d == 1
        sp = _softplus(-lam_ref[d:d + 1, :])
        bias_a = bias_ref[d, 0:1, :]
        bias_x = bias_ref[d, 1:2, :]

        def block(k, carry, d=d, reverse=reverse, sp=sp, bias_a=bias_a, bias_x=bias_x):
            t0 = pl.multiple_of((n_blocks - 1 - k if reverse else k) * tb, tb)
            xc = _conv_rows(x_ref, t0, tb, seq, cw, cb)
            pre = [_bdot(xc[:, n * LANES:(n + 1) * LANES], w_ref[d, n]) for n in range(tc // LANES)]
            r = jax.nn.sigmoid(jnp.concatenate([p[:, :LANES] for p in pre], axis=1) + bias_a)
            ig = jax.nn.sigmoid(jnp.concatenate([p[:, LANES:] for p in pre], axis=1) + bias_x)
            log_a = -RG_C * r * sp
            a = jnp.exp(log_a)
            bv = jnp.sqrt(-jnp.tanh(log_a) * (a * a + 1.0)) * (ig * xc)
            a_ref[...], b_ref[...] = _group_scan(a, bv, reverse)

            def group(i, c):
                r0 = pl.multiple_of((n_groups - 1 - i if reverse else i) * SUBLANES, SUBLANES)
                h = b_ref[pl.ds(r0, SUBLANES), :] + a_ref[pl.ds(r0, SUBLANES), :] * c
                rows = pl.ds(pl.multiple_of(t0 + r0, SUBLANES), SUBLANES)
                if reverse:
                    y_ref[rows, :] += h
                    return h[0:1, :]
                y_ref[rows, :] = h
                return h[SUBLANES - 1:SUBLANES, :]

            carry = lax.fori_loop(0, n_groups, group, carry)
            if reverse:
                o_ref[pl.ds(t0, tb), :] = (y_ref[pl.ds(t0, tb), :] * gate_ref[pl.ds(t0, tb), :]).astype(o_ref.dtype)
            return carry

        fin_ref[d:d + 1, :] = lax.fori_loop(0, n_blocks, block, h0_ref[d:d + 1, :])


def rg_core(xin, gate, conv_w, conv_b, w_gates, bias, lam, h0, stream, *, tc=256):
    n, c = xin.shape
    seq, batch = stream.seq, stream.batch
    tb = min(SCAN_BLOCK, seq)
    nb = tc // LANES
    return pl.pallas_call(
        functools.partial(_rg_core_kernel, seq=seq),
        out_shape=(jax.ShapeDtypeStruct((n, c), BF16), jax.ShapeDtypeStruct((batch, 2, c), F32)),
        grid=(batch, c // tc),
        in_specs=[pl.BlockSpec((seq, tc), lambda b, j: (b, j)),
                  pl.BlockSpec((seq, tc), lambda b, j: (b, j)),
                  pl.BlockSpec((4, tc), lambda b, j: (0, j)),
                  pl.BlockSpec((1, tc), lambda b, j: (0, j)),
                  pl.BlockSpec((2, nb, LANES, 2 * LANES), lambda b, j: (0, j, 0, 0)),
                  pl.BlockSpec((2, 2, tc), lambda b, j: (0, 0, j)),
                  pl.BlockSpec((2, tc), lambda b, j: (0, j)),
                  pl.BlockSpec((None, 2, tc), lambda b, j: (b, 0, j))],
        out_specs=(pl.BlockSpec((seq, tc), lambda b, j: (b, j)),
                   pl.BlockSpec((None, 2, tc), lambda b, j: (b, 0, j))),
        scratch_shapes=[pltpu.VMEM((seq, tc), F32), pltpu.VMEM((tb, tc), F32), pltpu.VMEM((tb, tc), F32)],
        compiler_params=_cparams("parallel", "parallel"),
        name="rg_core",
    )(xin, gate, conv_w, conv_b.reshape(1, c), w_gates, bias, lam, h0)


def _gelu_epilogue(acc):
    return jax.nn.gelu(acc)


def rg_mixer(h, x, mods, layer, stream, state, w_in, w_gate, conv_w, conv_b, w_gates, bias, lam, w_out):
    xin = matmul(h, w_in, tm=512, tn=512, name="rg_in")
    gate = matmul(h, w_gate, tm=512, tn=512, epilogue=_gelu_epilogue, name="rg_gate")
    yg, fin = rg_core(xin, gate, conv_w, conv_b, w_gates, bias, lam, state, stream)
    return out_proj_residual(yg, w_out, x, mods, layer, stream, name="rg_out"), fin


GRID_W = 64
ROPE_BASE = 10000.0
ATTN_TQ = 256
ATTN_TK = 512


def rope_tables(seq, dk):
    half = dk // 2
    quarter = half // 2
    pos = jnp.arange(seq)
    inv = ROPE_BASE ** (-jnp.arange(0, half, 2, dtype=F32) / half)
    ang_r = (pos // GRID_W).astype(F32)[:, None] * inv
    ang_c = (pos % GRID_W).astype(F32)[:, None] * inv
    cos = jnp.concatenate([jnp.cos(ang_r)] * 2 + [jnp.cos(ang_c)] * 2, axis=1)
    sin = jnp.concatenate([-jnp.sin(ang_r), jnp.sin(ang_r), -jnp.sin(ang_c), jnp.sin(ang_c)], axis=1)
    assert cos.shape == (seq, 4 * quarter)
    return cos, sin


def _rope_epilogue(acc, cos_ref, sin_ref):
    cos = cos_ref[...]
    sin = sin_ref[...]
    dk = cos.shape[1]
    lane = lax.broadcasted_iota(jnp.int32, cos.shape, 1)
    first = (lane % (dk // 2)) < dk // 4
    outs = []
    for g in range(acc.shape[1] // dk):
        x = acc[:, g * dk:(g + 1) * dk]
        rot = jnp.where(first, pltpu.roll(x, dk - dk // 4, axis=1), pltpu.roll(x, dk // 4, axis=1))
        outs.append(x * cos + rot * sin)
    return jnp.concatenate(outs, axis=1)


def _attn_kernel(*refs, n_tiles, tk, has_cache, lam_init, scale):
    if has_cache:
        q_ref, k_ref, v_ref, ck_ref, cv_ref, lam_ref, g_ref, o_ref = refs
    else:
        q_ref, k_ref, v_ref, lam_ref, g_ref, o_ref = refs
    q = q_ref[...]
    tq = q.shape[0]
    dk = q.shape[1] // 2
    dv = v_ref.shape[1]

    def step(kt, vt, st):
        new = []
        for c in range(2):
            m, l, acc = st[c]
            s = lax.dot_general(q[:, c * dk:(c + 1) * dk], kt[:, c * dk:(c + 1) * dk],
                                (((1,), (1,)), ((), ())), preferred_element_type=F32) * scale
            m_new = jnp.maximum(m, jnp.max(s, axis=-1, keepdims=True))
            alpha = jnp.exp(m - m_new)
            p = jnp.exp(s - m_new)
            l = alpha * l + jnp.sum(p, axis=-1, keepdims=True)
            acc = alpha * acc + jnp.dot(p.astype(BF16), vt, preferred_element_type=F32)
            new.append((m_new, l, acc))
        return tuple(new)

    st = tuple((jnp.full((tq, 1), -jnp.inf, F32), jnp.zeros((tq, 1), F32), jnp.zeros((tq, dv), F32))
               for _ in range(2))
    if has_cache:
        st = step(ck_ref[...], cv_ref[...], st)

    def body(i, st):
        r0 = pl.multiple_of(i * tk, tk)
        return step(k_ref[pl.ds(r0, tk), :], v_ref[pl.ds(r0, tk), :], st)

    st = lax.fori_loop(0, n_tiles, body, st)
    lp = lam_ref[...]
    lam = (jnp.exp(jnp.sum(lp[0:1, :] * lp[1:2, :], axis=-1, keepdims=True))
           - jnp.exp(jnp.sum(lp[2:3, :] * lp[3:4, :], axis=-1, keepdims=True)) + lam_init)
    diff = st[0][2] / st[0][1] - lam * (st[1][2] / st[1][1])
    diff = diff * lax.rsqrt(jnp.mean(diff * diff, axis=-1, keepdims=True) + NORM_EPS)
    o_ref[...] = (diff * g_ref[...] * (1.0 - lam_init)).astype(o_ref.dtype)


def diff_attention(q, k, v, *rest, stream, lam_init):
    has_cache = len(rest) == 4
    ck, cv = (rest[0], rest[1]) if has_cache else (None, None)
    da_lam, subln_g = rest[-2], rest[-1]
    n, width = q.shape
    dv = subln_g.shape[0]
    heads = width // dv
    dk = dv // 2
    batch, seq = stream.batch, stream.seq
    tq = min(ATTN_TQ, seq)
    tk = min(ATTN_TK, seq)
    nq = seq // tq
    in_specs = [pl.BlockSpec((tq, dv), lambda b, h, i: (b * nq + i, h)),
                pl.BlockSpec((seq, dv), lambda b, h, i: (b, h)),
                pl.BlockSpec((seq, dv), lambda b, h, i: (b, h))]
    args = [q, k, v]
    if has_cache:
        past = ck.shape[0] // batch
        in_specs += [pl.BlockSpec((past, dv), lambda b, h, i: (b, h))] * 2
        args += [ck, cv]
    in_specs += [pl.BlockSpec(da_lam.shape, lambda b, h, i: (0, 0)), pl.BlockSpec((1, dv), lambda b, h, i: (0, 0))]
    args += [da_lam.astype(F32), subln_g.reshape(1, dv).astype(F32)]
    return pl.pallas_call(
        functools.partial(_attn_kernel, n_tiles=seq // tk, tk=tk, has_cache=has_cache, lam_init=lam_init,
                          scale=1.0 / math.sqrt(dk)),
        out_shape=jax.ShapeDtypeStruct((n, width), BF16),
        grid=(batch, heads, nq),
        in_specs=in_specs,
        out_specs=pl.BlockSpec((tq, dv), lambda b, h, i: (b * nq + i, h)),
        compiler_params=_cparams("parallel", "parallel", "parallel"),
        name="diff_attention",
    )(*args)


def _dup_epilogue(acc):
    return acc, acc


def dattn_mixer(h, x, mods, layer, stream, cache, wq, wk, wv, wo, da_lam, subln_g, lam_init):
    dk = subln_g.shape[0] // 2
    if cache is None:
        q = matmul(h, wq, tm=512, tn=512, out_dtype=BF16, name="da_q")
        k32, k = matmul(h, wk, tm=512, tn=512, out_dtype=(F32, BF16), epilogue=_dup_epilogue, name="da_k")
        v32, v = matmul(h, wv, tm=512, tn=512, out_dtype=(F32, BF16), epilogue=_dup_epilogue, name="da_v")
        o = diff_attention(q, k, v, da_lam, subln_g, stream=stream, lam_init=lam_init)
        new = (k32, v32)
    else:
        cos, sin = rope_tables(stream.seq, dk)
        per = stream.seq // 512
        rope = [(cos, (512, dk), lambda i, j: (i % per, 0)), (sin, (512, dk), lambda i, j: (i % per, 0))]
        q = matmul(h, wq, tm=512, tn=512, out_dtype=BF16, epilogue=_rope_epilogue, extras=rope, name="da_q_rope")
        k = matmul(h, wk, tm=512, tn=512, out_dtype=BF16, epilogue=_rope_epilogue, extras=rope, name="da_k_rope")
        v = matmul(h, wv, tm=512, tn=512, out_dtype=BF16, name="da_v")
        o = diff_attention(q, k, v, cache[0], cache[1], da_lam, subln_g, stream=stream, lam_init=lam_init)
        new = None
    return out_proj_residual(o, wo, x, mods, layer, stream, name="da_out"), new


ML_CHUNK = 256


def _ml_conv_kernel(x_ref, cw_ref, cb_ref, o32_ref, o16_ref, *, seq):
    tb = min(SCAN_BLOCK, seq)
    cw = cw_ref[...]
    cb = cb_ref[...]

    def body(i, carry):
        t0 = pl.multiple_of(i * tb, tb)
        y = _conv_rows(x_ref, t0, tb, seq, cw, cb)
        y = y * jax.nn.sigmoid(y)
        o32_ref[pl.ds(t0, tb), :] = y
        o16_ref[pl.ds(t0, tb), :] = y.astype(BF16)
        return carry

    lax.fori_loop(0, seq // tb, body, 0)


def ml_conv(xi, conv_w, conv_b, stream, *, tc=256):
    n, w = xi.shape
    seq = stream.seq
    return pl.pallas_call(
        functools.partial(_ml_conv_kernel, seq=seq),
        out_shape=(jax.ShapeDtypeStruct((n, w), F32), jax.ShapeDtypeStruct((n, w), BF16)),
        grid=(stream.batch, w // tc),
        in_specs=[pl.BlockSpec((seq, tc), lambda b, j: (b, j)),
                  pl.BlockSpec((conv_w.shape[0], tc), lambda b, j: (0, j)),
                  pl.BlockSpec((1, tc), lambda b, j: (0, j))],
        out_specs=(pl.BlockSpec((seq, tc), lambda b, j: (b, j)), pl.BlockSpec((seq, tc), lambda b, j: (b, j))),
        compiler_params=_cparams("parallel", "parallel"),
        name="ml_conv",
    )(xi, conv_w.astype(F32), conv_b.reshape(1, w).astype(F32))


def _ml_gates_kernel(q_ref, k_ref, v_ref, w_ref, b_ref, o_ref):
    acc = jnp.dot(q_ref[...], w_ref[0], preferred_element_type=F32)
    acc = acc + jnp.dot(k_ref[...], w_ref[1], preferred_element_type=F32)
    acc = acc + jnp.dot(v_ref[...], w_ref[2], preferred_element_type=F32)
    o_ref[...] = acc + b_ref[...]


def ml_gates(q, k, v, w_gate, b_gate, *, tm=512):
    n, w = q.shape
    ng = w_gate.shape[2]
    row = pl.BlockSpec((tm, w), lambda i: (i, 0))
    return pl.pallas_call(
        _ml_gates_kernel,
        out_shape=jax.ShapeDtypeStruct((n, ng), F32),
        grid=(n // tm,),
        in_specs=[row, row, row, pl.BlockSpec((3, w, ng), lambda i: (0, 0, 0)), pl.BlockSpec((1, ng), lambda i: (0, 0))],
        out_specs=pl.BlockSpec((tm, ng), lambda i: (i, 0)),
        compiler_params=_cparams("parallel"),
        name="ml_gates",
    )(q, k, v, w_gate, b_gate.reshape(1, ng).astype(F32))


def _mlstm_cell_kernel(q_ref, k_ref, v_ref, g_ref, gt_ref, c0_ref, n0_ref, m0_ref,
                       h_ref, cf_ref, nf_ref, mf_ref, c_s, n_s, m_s, *, n_heads):
    hd = pl.program_id(1)
    d = pl.program_id(2)
    ci = pl.program_id(3)
    reverse = d == 1
    t = q_ref.shape[0]

    @pl.when(ci == 0)
    def _():
        c_s[...] = c0_ref[...]
        n_s[...] = n0_ref[...]
        m_s[...] = m0_ref[...]

    q = q_ref[...]
    k = k_ref[...]
    v = v_ref[...]
    col_i = d * (2 * n_heads) + hd
    col_f = col_i + n_heads
    g = g_ref[...]
    gt = gt_ref[...]
    lane = lax.broadcasted_iota(jnp.int32, g.shape, 1)
    sub = lax.broadcasted_iota(jnp.int32, gt.shape, 0)
    ig_col = jnp.sum(jnp.where(lane == col_i, g, 0.0), axis=1, keepdims=True)
    fg_col = jnp.sum(jnp.where(lane == col_f, g, 0.0), axis=1, keepdims=True)
    ig_row = jnp.sum(jnp.where(sub == col_i, gt, 0.0), axis=0, keepdims=True)
    fg_row = jnp.sum(jnp.where(sub == col_f, gt, 0.0), axis=0, keepdims=True)
    lf_col = -_softplus(-fg_col)
    lf_row = -_softplus(-fg_row)
    r_idx = lax.broadcasted_iota(jnp.int32, (t, t), 0)
    s_idx = lax.broadcasted_iota(jnp.int32, (t, t), 1)
    ahead = (s_idx - r_idx) * jnp.where(reverse, -1, 1)
    mask = ahead <= 0
    mask_t = ahead >= 0
    b_col = jnp.sum(jnp.where(mask, lf_row, 0.0), axis=1, keepdims=True)
    b_row = jnp.sum(jnp.where(mask_t, lf_col, 0.0), axis=0, keepdims=True)
    m_prev = m_s[...]
    dlog = jnp.where(mask, b_col - b_row + ig_row, -jnp.inf)
    inter = b_col + m_prev
    m_t = jnp.maximum(inter, jnp.max(dlog, axis=1, keepdims=True))
    w_intra = jnp.exp(dlog - m_t)
    w_inter = jnp.exp(inter - m_t)
    s_mat = lax.dot_general(q, k, (((1,), (1,)), ((), ())), preferred_element_type=F32) * w_intra
    c_old = c_s[...]
    n_old = n_s[...]
    qc = lax.dot_general(q, c_old.astype(BF16), (((1,), (1,)), ((), ())), preferred_element_type=F32)
    num = jnp.dot(s_mat.astype(BF16), v, preferred_element_type=F32) + w_inter * qc
    qn = jnp.sum(q.astype(F32) * n_old, axis=1, keepdims=True)
    den = jnp.sum(s_mat, axis=1, keepdims=True) + w_inter * qn
    h_ref[...] = num / jnp.maximum(jnp.abs(den), jnp.exp(-m_t))
    row = lax.broadcasted_iota(jnp.int32, (t, 1), 0)
    last = row == jnp.where(reverse, 0, t - 1)
    m_new = jnp.sum(jnp.where(last, m_t, 0.0), axis=0, keepdims=True)
    b_last = jnp.sum(jnp.where(last, b_col, 0.0), axis=0, keepdims=True)
    w_old = jnp.exp(b_last + m_prev - m_new)
    w_in = jnp.exp(b_last - b_col + ig_col - m_new)
    vw = (v.astype(F32) * w_in).astype(BF16)
    c_s[...] = w_old * c_old + lax.dot_general(vw, k, (((0,), (0,)), ((), ())), preferred_element_type=F32)
    n_s[...] = w_old * n_old + jnp.sum(w_in * k.astype(F32), axis=0, keepdims=True)
    m_s[...] = m_new

    @pl.when(ci == pl.num_programs(3) - 1)
    def _():
        cf_ref[...] = c_s[...]
        nf_ref[...] = n_s[...]
        mf_ref[...] = m_s[...]


def mlstm_cell(q, k, v, g, g_t, c0, n0, m0, stream):
    n, w = q.shape
    batch, seq = stream.batch, stream.seq
    heads, dh = c0.shape[2], c0.shape[3]
    t = min(ML_CHUNK, seq)
    nc = seq // t

    def blk(b, d, ci):
        return b * nc + ci + d * (nc - 1 - 2 * ci)

    tok = pl.BlockSpec((t, dh), lambda b, h, d, ci: (blk(b, d, ci), h))
    st5 = lambda r, c: pl.BlockSpec((None, None, None, r, c), lambda b, h, d, ci: (b, d, h, 0, 0))
    return pl.pallas_call(
        functools.partial(_mlstm_cell_kernel, n_heads=heads),
        out_shape=(jax.ShapeDtypeStruct((2, n, w), F32), jax.ShapeDtypeStruct(c0.shape, F32),
                   jax.ShapeDtypeStruct(n0.shape, F32), jax.ShapeDtypeStruct(m0.shape, F32)),
        grid=(batch, heads, 2, nc),
        in_specs=[tok, tok, tok,
                  pl.BlockSpec((t, g.shape[1]), lambda b, h, d, ci: (blk(b, d, ci), 0)),
                  pl.BlockSpec((g.shape[1], t), lambda b, h, d, ci: (0, blk(b, d, ci))),
                  st5(dh, dh), st5(1, dh), st5(1, 1)],
        out_specs=(pl.BlockSpec((None, t, dh), lambda b, h, d, ci: (d, blk(b, d, ci), h)),
                   st5(dh, dh), st5(1, dh), st5(1, 1)),
        scratch_shapes=[pltpu.VMEM((dh, dh), F32), pltpu.VMEM((1, dh), F32), pltpu.VMEM((1, 1), F32)],
        compiler_params=_cparams("parallel", "parallel", "arbitrary", "arbitrary"),
        name="mlstm_cell",
    )(q, k, v, g, g_t, c0, n0, m0)


def _mlstm_post_kernel(h_ref, o_ref, xc_ref, gn_ref, skip_ref, y_ref):
    hc = o_ref[...] * (h_ref[0] + h_ref[1])
    hn = hc * lax.rsqrt(jnp.mean(hc * hc, axis=-1, keepdims=True) + NORM_EPS)
    y_ref[...] = (hn * gn_ref[...] + skip_ref[...] * xc_ref[...]).astype(y_ref.dtype)


def mlstm_post(hdir, o, xc, gn, skip, dh, *, tm=512):
    n, w = o.shape
    tile = pl.BlockSpec((tm, dh), lambda i, h: (i, h))
    vec = pl.BlockSpec((1, dh), lambda i, h: (0, h))
    return pl.pallas_call(
        _mlstm_post_kernel,
        out_shape=jax.ShapeDtypeStruct((n, w), BF16),
        grid=(n // tm, w // dh),
        in_specs=[pl.BlockSpec((2, tm, dh), lambda i, h: (0, i, h)), tile, tile, vec, vec],
        out_specs=tile,
        compiler_params=_cparams("parallel", "parallel"),
        name="mlstm_post",
    )(hdir, o, xc, gn.reshape(1, w).astype(F32), skip.reshape(1, w).astype(F32))


def _scale_epilogue(acc, *, scale):
    return acc * scale


def _sigmoid_bias_epilogue(acc, b_ref):
    return jax.nn.sigmoid(acc + b_ref[...])


def mlstm_mixer(h, x, mods, layer, stream, c0, n0, m0, w_up, conv_w, conv_b, wq, wk, wv, w_gate, b_gate,
                w_o, b_o, gn, skip, w_down):
    dh = wq.shape[1]
    xi = matmul(h, w_up, tm=512, tn=512, name="ml_up")
    xc, xc16 = ml_conv(xi, conv_w, conv_b, stream)
    q = matmul(xc16, wq, tm=512, tn=dh, out_dtype=BF16, k_block=dh, name="ml_q")
    k = matmul(xc16, wk, tm=512, tn=dh, out_dtype=BF16, k_block=dh, name="ml_k",
               epilogue=functools.partial(_scale_epilogue, scale=1.0 / math.sqrt(dh)))
    v = matmul(xi, wv, tm=512, tn=dh, out_dtype=BF16, k_block=dh, name="ml_v")
    g = ml_gates(q, k, v, w_gate, b_gate)
    hdir, cf, nf, mf = mlstm_cell(q, k, v, g, g.T, c0, n0, m0, stream)
    w = xi.shape[1]
    o = matmul(h, w_o, tm=512, tn=512, name="ml_o", epilogue=_sigmoid_bias_epilogue,
               extras=[(b_o.reshape(1, w).astype(F32), (1, 512), lambda i, j: (0, j))])
    y = mlstm_post(hdir, o, xc, gn, skip, dh)
    return out_proj_residual(y, w_down, x, mods, layer, stream, name="ml_down"), (cf, nf, mf)


def kernel(x_prompt, x_sample, state_s5, state_rglru, cache_dattn_k, cache_dattn_v, state_mlstm_C, state_mlstm_n, state_mlstm_m, c, c_ctx, ada_w, ada_b, norm_g, ffn1_w1, ffn1_w3, ffn1_w2, ffn2_w1, ffn2_w3, ffn2_w2, final_norm_g, s5_w_in, s5_a_re, s5_a_im, s5_log_dt, s5_b_re, s5_b_im, s5_c_re, s5_c_im, s5_d, s5_w_glu, s5_b_glu, s5_w_out, rg_w_in, rg_w_gate, rg_conv_w, rg_conv_b, rg_wa, rg_ba, rg_wx, rg_bx, rg_lam, rg_w_out, da_wq, da_wk, da_wv, da_wo, da_lam, da_subln_g, ml_w_up, ml_conv_w, ml_conv_b, ml_wq, ml_wk, ml_wv, ml_w_gate, ml_b_gate, ml_w_o, ml_b_o, ml_gn, ml_skip, ml_w_down):
    bc, lc, d = x_prompt.shape
    bl, ll, _ = x_sample.shape
    depth = ada_w.shape[0]
    cs = Stream(bc, lc, 0, False)
    ls = Stream(bl, ll, 1, True)
    ctx = x_prompt.reshape(cs.n, d)
    lat = x_sample.reshape(ls.n, d)
    c_all = jnp.concatenate([c_ctx[None, :], c, jnp.zeros((MOD_ROWS - 1 - bl, d), F32)], axis=0)
    mods = adaln(c_all, ada_w, ada_b)
    bf = lambda a: a.astype(BF16)

    new_s5 = new_rg = None
    for i in range(depth):
        w1, w3, w2 = bf(ffn1_w1[i]), bf(ffn1_w3[i]), bf(ffn1_w2[i])
        ctx = ffn_half(ctx, norm_g[i, 0], mods, i, 0, w1, w3, w2, cs)
        lat = ffn_half(lat, norm_g[i, 0], mods, i, 0, w1, w3, w2, ls)
        hc = modnorm(ctx, norm_g[i, 1], mods, i, 3, cs)
        hl = modnorm(lat, norm_g[i, 1], mods, i, 3, ls)
        kind = i % 4
        if kind == 0:
            ops = _s5_operators(s5_a_re, s5_a_im, s5_log_dt, s5_b_re, s5_b_im, s5_c_re, s5_c_im)
            ng, p = s5_a_re.shape[1], s5_a_re.shape[2]
            h0 = jnp.concatenate([state_s5[:, :, 0], state_s5[:, :, 1]], axis=-1).transpose(1, 2, 0, 3)
            args = (bf(s5_w_in), ops, s5_d, bf(s5_w_glu), s5_b_glu, bf(s5_w_out))
            ctx, fin = s5_mixer(hc, ctx, mods, i, cs, jnp.zeros((2, ng, bc, 2 * p), F32), *args)
            lat, _ = s5_mixer(hl, lat, mods, i, ls, h0.astype(F32), *args)
            fin = fin.transpose(2, 0, 1, 3)
            new_s5 = jnp.stack([fin[..., :p], fin[..., p:]], axis=2)
        elif kind == 1:
            w_gates = bf(jnp.concatenate([rg_wa, rg_wx], axis=-1))
            bias = jnp.stack([rg_ba, rg_bx], axis=1).astype(F32)
            args = (bf(rg_w_in), bf(rg_w_gate), rg_conv_w, rg_conv_b, w_gates, bias, rg_lam, bf(rg_w_out))
            ctx, new_rg = rg_mixer(hc, ctx, mods, i, cs, jnp.zeros((bc, 2, rg_lam.shape[1]), F32), *args)
            lat, _ = rg_mixer(hl, lat, mods, i, ls, state_rglru, *args)
        elif kind == 2:
            lam_init = 0.8 - 0.6 * math.exp(-0.3 * i)
            heads, dv = cache_dattn_v.shape[2], cache_dattn_v.shape[3]
            args = (bf(da_wq), bf(da_wk), bf(da_wv), bf(da_wo), da_lam, da_subln_g, lam_init)
            ctx, (k32, v32) = dattn_mixer(hc, ctx, mods, i, cs, None, *args)
            new_k = k32.reshape(bc, lc, heads, 2, dv // 2)
            new_v = v32.reshape(bc, lc, heads, dv)
            cache = (bf(cache_dattn_k).reshape(-1, heads * dv), bf(cache_dattn_v).reshape(-1, heads * dv))
            lat, _ = dattn_mixer(hl, lat, mods, i, ls, cache, *args)
        else:
            heads, dh = ml_wq.shape[0], ml_wq.shape[1]
            args = (bf(ml_w_up), ml_conv_w, ml_conv_b, bf(ml_wq).reshape(heads * dh, dh),
                    bf(ml_wk).reshape(heads * dh, dh), bf(ml_wv).reshape(heads * dh, dh), bf(ml_w_gate), ml_b_gate,
                    bf(ml_w_o), ml_b_o, ml_gn, ml_skip, bf(ml_w_down))
            zeros = lambda *s: jnp.zeros(s, F32)
            ctx, (cf, nf, mf) = mlstm_mixer(hc, ctx, mods, i, cs, zeros(bc, 2, heads, dh, dh),
                                            zeros(bc, 2, heads, 1, dh), zeros(bc, 2, heads, 1, 1), *args)
            lat, _ = mlstm_mixer(hl, lat, mods, i, ls, state_mlstm_C.astype(F32),
                                 state_mlstm_n.astype(F32).reshape(bl, 2, heads, 1, dh),
                                 state_mlstm_m.astype(F32).reshape(bl, 2, heads, 1, 1), *args)
            new_c, new_n, new_m = cf, nf.reshape(bc, 2, heads, dh), mf.reshape(bc, 2, heads)
        w1, w3, w2 = bf(ffn2_w1[i]), bf(ffn2_w3[i]), bf(ffn2_w2[i])
        ctx = ffn_half(ctx, norm_g[i, 2], mods, i, 6, w1, w3, w2, cs)
        lat = ffn_half(lat, norm_g[i, 2], mods, i, 6, w1, w3, w2, ls)
    y_prompt = rmsnorm(ctx, final_norm_g).reshape(bc, lc, d)
    y_sample = rmsnorm(lat, final_norm_g).reshape(bl, ll, d)
    return (y_prompt, y_sample, new_s5, new_rg, new_k, new_v, new_c, new_n, new_m)
```

```python
import functools
import math

import jax
import jax.numpy as jnp
from jax import lax
from jax.experimental import pallas as pl
from jax.experimental.pallas import tpu as pltpu

F32 = jnp.float32
BF16 = jnp.bfloat16

NORM_EPS = 1e-6
VMEM_LIMIT_BYTES = 56 * 1024 * 1024
MOD_ROWS = 8


def _cparams(*sem):
    return pltpu.CompilerParams(dimension_semantics=sem, vmem_limit_bytes=VMEM_LIMIT_BYTES)


def _bdot(a, b):
    return jnp.dot(a.astype(BF16), b.astype(BF16), preferred_element_type=F32)


def _mod_row(ref, row):
    return ref[pl.ds(row, 1), :]


class Stream:
    def __init__(self, batch, seq, row0, per_batch_mod):
        self.batch, self.seq, self.row0 = batch, seq, row0
        self.n = batch * seq
        self.rows_per_mod = seq if per_batch_mod else self.n

    def mod_row(self, first_token):
        return self.row0 + first_token // self.rows_per_mod


def _adaln_kernel(c_ref, w_ref, b_ref, o_ref):
    c = c_ref[...]
    s = c * jax.nn.sigmoid(c)
    o_ref[...] = _bdot(s, w_ref[...]) + b_ref[...]


def adaln(c_all, ada_w, ada_b, *, tn=1024):
    nl, d, m = ada_w.shape
    return pl.pallas_call(
        _adaln_kernel,
        out_shape=jax.ShapeDtypeStruct((nl, MOD_ROWS, m), F32),
        grid=(nl, m // tn),
        in_specs=[pl.BlockSpec((MOD_ROWS, d), lambda l, j: (0, 0)),
                  pl.BlockSpec((None, d, tn), lambda l, j: (l, 0, j)),
                  pl.BlockSpec((None, 1, tn), lambda l, j: (l, 0, j))],
        out_specs=pl.BlockSpec((None, MOD_ROWS, tn), lambda l, j: (l, 0, j)),
        compiler_params=_cparams("parallel", "parallel"),
        name="adaln",
    )(c_all, ada_w, ada_b.reshape(nl, 1, m))


def _mod_spec(d, layer, k):
    return pl.BlockSpec((None, MOD_ROWS, d), lambda *_: (layer, 0, k))


def _rms(x, g):
    return x * lax.rsqrt(jnp.mean(x * x, axis=-1, keepdims=True) + NORM_EPS) * g


def _modnorm_kernel(x_ref, g_ref, sh_ref, sc_ref, o_ref, *, stream, tm):
    row = stream.mod_row(pl.program_id(0) * tm)
    y = _rms(x_ref[...], g_ref[...])
    o_ref[...] = (y * (1.0 + _mod_row(sc_ref, row)) + _mod_row(sh_ref, row)).astype(o_ref.dtype)


def modnorm(x, g, mods, layer, k_shift, stream, *, tm=512):
    n, d = x.shape
    return pl.pallas_call(
        functools.partial(_modnorm_kernel, stream=stream, tm=tm),
        out_shape=jax.ShapeDtypeStruct((n, d), BF16),
        grid=(n // tm,),
        in_specs=[pl.BlockSpec((tm, d), lambda i: (i, 0)),
                  pl.BlockSpec((1, d), lambda i: (0, 0)),
                  _mod_spec(d, layer, k_shift),
                  _mod_spec(d, layer, k_shift + 1)],
        out_specs=pl.BlockSpec((tm, d), lambda i: (i, 0)),
        compiler_params=_cparams("parallel"),
        name="modnorm",
    )(x, g.reshape(1, d), mods, mods)


def _rmsnorm_kernel(x_ref, g_ref, o_ref):
    o_ref[...] = _rms(x_ref[...], g_ref[...])


def rmsnorm(x, g, *, tm=512):
    n, d = x.shape
    return pl.pallas_call(
        _rmsnorm_kernel,
        out_shape=jax.ShapeDtypeStruct((n, d), F32),
        grid=(n // tm,),
        in_specs=[pl.BlockSpec((tm, d), lambda i: (i, 0)), pl.BlockSpec((1, d), lambda i: (0, 0))],
        out_specs=pl.BlockSpec((tm, d), lambda i: (i, 0)),
        compiler_params=_cparams("parallel"),
        name="rmsnorm",
    )(x, g.reshape(1, d))


def _ffn_kernel(x_ref, g_ref, sh_ref, sc_ref, gt_ref, w1_ref, w3_ref, w2_ref, o_ref, h_ref, acc_ref,
                *, stream, tm):
    f = pl.program_id(1)
    row = stream.mod_row(pl.program_id(0) * tm)

    @pl.when(f == 0)
    def _():
        y = _rms(x_ref[...], g_ref[...])
        h_ref[...] = (y * (1.0 + _mod_row(sc_ref, row)) + _mod_row(sh_ref, row)).astype(BF16)
        acc_ref[...] = jnp.zeros_like(acc_ref)

    h = h_ref[...]
    a = jnp.dot(h, w1_ref[...], preferred_element_type=F32)
    b = jnp.dot(h, w3_ref[...], preferred_element_type=F32)
    u = (a * jax.nn.sigmoid(a)) * b
    acc_ref[...] += jnp.dot(u.astype(BF16), w2_ref[...], preferred_element_type=F32)

    @pl.when(f == pl.num_programs(1) - 1)
    def _():
        o_ref[...] = x_ref[...] + (0.5 * _mod_row(gt_ref, row)) * acc_ref[...]


def ffn_half(x, g, mods, layer, k_shift, w1, w3, w2, stream, *, tm=512, tf=512):
    n, d = x.shape
    dff = w1.shape[1]
    return pl.pallas_call(
        functools.partial(_ffn_kernel, stream=stream, tm=tm),
        out_shape=jax.ShapeDtypeStruct((n, d), F32),
        grid=(n // tm, dff // tf),
        in_specs=[pl.BlockSpec((tm, d), lambda i, f: (i, 0)),
                  pl.BlockSpec((1, d), lambda i, f: (0, 0)),
                  _mod_spec(d, layer, k_shift),
                  _mod_spec(d, layer, k_shift + 1),
                  _mod_spec(d, layer, k_shift + 2),
                  pl.BlockSpec((d, tf), lambda i, f: (0, f)),
                  pl.BlockSpec((d, tf), lambda i, f: (0, f)),
                  pl.BlockSpec((tf, d), lambda i, f: (f, 0))],
        out_specs=pl.BlockSpec((tm, d), lambda i, f: (i, 0)),
        scratch_shapes=[pltpu.VMEM((tm, d), BF16), pltpu.VMEM((tm, d), F32)],
        compiler_params=_cparams("parallel", "arbitrary"),
        name="ffn_half",
    )(x, g.reshape(1, d), mods, mods, mods, w1, w3, w2)


def _mm_kernel(x_ref, w_ref, *rest, epilogue, n_extra):
    extra, o_refs = rest[:n_extra], rest[n_extra:]
    acc = _bdot(x_ref[...], w_ref[...])
    outs = epilogue(acc, *extra) if epilogue is not None else acc
    if not isinstance(outs, tuple):
        outs = (outs,)
    for o_ref, o in zip(o_refs, outs):
        o_ref[...] = o.astype(o_ref.dtype)


MM_TM = 1024
MM_TN = 512
MM_TM_DIAG = 2048


def matmul(x, w, *, tm=MM_TM, tn=MM_TN, out_dtype=F32, epilogue=None, extras=(), k_block=None, name="matmul"):
    n, kx = x.shape
    tm = min(tm, n)
    if k_block is None:
        kb, m = kx, w.shape[1]
        x_map = lambda i, j: (i, 0)
        w_map = lambda i, j: (0, j)
    else:
        kb = k_block
        per = w.shape[1] // tn
        m = (kx // kb) * w.shape[1]
        x_map = lambda i, j: (i, j // per)
        w_map = lambda i, j: (j // per, j % per)
    dts = out_dtype if isinstance(out_dtype, tuple) else (out_dtype,)
    out_shape = tuple(jax.ShapeDtypeStruct((n, m), dt) for dt in dts)
    out_specs = tuple(pl.BlockSpec((tm, tn), lambda i, j: (i, j)) for _ in dts)
    res = pl.pallas_call(
        functools.partial(_mm_kernel, epilogue=epilogue, n_extra=len(extras)),
        out_shape=out_shape,
        grid=(n // tm, m // tn),
        in_specs=[pl.BlockSpec((tm, kb), x_map), pl.BlockSpec((kb, tn), w_map)]
        + [pl.BlockSpec(bs, im) for _, bs, im in extras],
        out_specs=out_specs,
        compiler_params=_cparams("parallel", "parallel"),
        name=name,
    )(x, w, *[a for a, _, _ in extras])
    return res if isinstance(out_dtype, tuple) else res[0]


S5_CHUNK = 16
S5_GROUPS_PER_STEP = 8


def _s5_operators(a_re, a_im, log_dt, b_re, b_im, c_re, c_im):
    hp = lax.Precision.HIGHEST
    t = S5_CHUNK
    a_re, a_im, b_re, b_im, c_re, c_im = (z.astype(F32) for z in (a_re, a_im, b_re, b_im, c_re, c_im))
    dt = jnp.exp(log_dt.astype(F32))[..., None]
    mag = jnp.exp(a_re * dt)
    lr = mag * jnp.cos(a_im * dt)
    li = mag * jnp.sin(a_im * dt)
    den = a_re * a_re + a_im * a_im
    cr = ((lr - 1.0) * a_re + li * a_im) / den
    ci = (li * a_re - (lr - 1.0) * a_im) / den
    bbr = cr[..., None] * b_re - ci[..., None] * b_im
    bbi = cr[..., None] * b_im + ci[..., None] * b_re
    k = jnp.arange(t + 1, dtype=F32)[None, None, :, None]
    pmag = jnp.exp(k * (a_re * dt)[:, :, None, :])
    pr = pmag * jnp.cos(k * (a_im * dt)[:, :, None, :])
    pi = pmag * jnp.sin(k * (a_im * dt)[:, :, None, :])
    pbr = pr[..., None] * bbr[:, :, None] - pi[..., None] * bbi[:, :, None]
    pbi = pr[..., None] * bbi[:, :, None] + pi[..., None] * bbr[:, :, None]
    m = (jnp.einsum('dgop,dgkpi->dgkoi', c_re, pbr, precision=hp)
         - jnp.einsum('dgop,dgkpi->dgkoi', c_im, pbi, precision=hp))
    m = m.at[:, :, t].set(0.0)
    s_idx = jnp.arange(t)[:, None]
    t_idx = jnp.arange(t)[None, :]
    lag_f = jnp.where(t_idx >= s_idx, t_idx - s_idx, t)
    lag_b = jnp.where(s_idx >= t_idx, s_idx - t_idx, t)

    def toeplitz(md, lag):
        x = md[:, lag]
        return x.transpose(0, 1, 4, 2, 3).reshape(md.shape[0], t * x.shape[-1], t * x.shape[-2])

    toep = toeplitz(m[0], lag_f) + toeplitz(m[1], lag_b)
    pow_s = jnp.stack([t - 1 - jnp.arange(t), jnp.arange(t)])

    def smat(d):
        r = pbr[d][:, pow_s[d]]
        i = pbi[d][:, pow_s[d]]
        x = jnp.concatenate([r, i], axis=2)
        return x.transpose(0, 1, 3, 2).reshape(x.shape[0], t * x.shape[3], x.shape[2])

    s_op = jnp.stack([smat(0), smat(1)])
    pow_w = jnp.stack([jnp.arange(t) + 1, t - jnp.arange(t)])

    def wmat(d):
        zr = (c_re[d][:, None] * pr[d][:, pow_w[d]][:, :, None, :]
              - c_im[d][:, None] * pi[d][:, pow_w[d]][:, :, None, :])
        zi = (c_re[d][:, None] * pi[d][:, pow_w[d]][:, :, None, :]
              + c_im[d][:, None] * pr[d][:, pow_w[d]][:, :, None, :])
        x = jnp.concatenate([zr, -zi], axis=3)
        return x.transpose(0, 3, 1, 2).reshape(x.shape[0], x.shape[3], t * x.shape[2])

    w_op = jnp.stack([wmat(0), wmat(1)])
    mul_a = jnp.concatenate([pr[:, :, t], pr[:, :, t]], axis=-1)
    mul_b = jnp.concatenate([-pi[:, :, t], pi[:, :, t]], axis=-1)
    return toep.astype(BF16), s_op.astype(BF16), w_op.astype(BF16), mul_a, mul_b


def _s5_intra_kernel(u_ref, toep_ref, s_ref, y_ref, st_ref):
    for g in range(u_ref.shape[0]):
        u = u_ref[g]
        y_ref[g] = jnp.dot(u, toep_ref[g], preferred_element_type=F32)
        for d in range(2):
            st_ref[d, g] = jnp.dot(u, s_ref[d, g], preferred_element_type=F32)


def _s5_recur_kernel(st_ref, a_ref, b_ref, h0_ref, hin_ref, fin_ref, *, n_chunks):
    rev = pl.program_id(0) == 1
    half = a_ref.shape[-1] // 2
    a = a_ref[...]
    b = b_ref[...]
    b_sw = pltpu.roll(b, half, axis=2)

    def body(i, carry):
        h, h_sw = carry
        j = jnp.where(rev, n_chunks - 1 - i, i)
        hin_ref[j] = h
        s = st_ref[j]
        return h * a + h_sw * b + s, h_sw * a + h * b_sw + pltpu.roll(s, half, axis=2)

    h0 = h0_ref[...]
    fin_ref[...] = lax.fori_loop(0, n_chunks, body, (h0, pltpu.roll(h0, half, axis=2)))[0]


def _s5_inter_kernel(y_ref, hin_ref, w_ref, o_ref):
    for g in range(y_ref.shape[0]):
        acc = y_ref[g]
        for d in range(2):
            acc = acc + jnp.dot(hin_ref[d, g].astype(BF16), w_ref[d, g], preferred_element_type=F32)
        o_ref[g] = acc


def s5_scan(u, h0, ops, batch, seq):
    toep, s_op, w_op, mul_a, mul_b = ops
    ng, tc, _ = toep.shape
    t = S5_CHUNK
    c = tc // t
    p2 = s_op.shape[-1]
    nj = seq // t
    rows = nj * batch
    gs = S5_GROUPS_PER_STEP
    ug = u.astype(BF16).reshape(batch, nj, t, ng, c).transpose(3, 1, 0, 2, 4).reshape(ng, rows, tc)
    y_intra, st = pl.pallas_call(
        _s5_intra_kernel,
        out_shape=(jax.ShapeDtypeStruct((ng, rows, tc), F32), jax.ShapeDtypeStruct((2, ng, rows, p2), F32)),
        grid=(ng // gs,),
        in_specs=[pl.BlockSpec((gs, rows, tc), lambda g: (g, 0, 0)),
                  pl.BlockSpec((gs, tc, tc), lambda g: (g, 0, 0)),
                  pl.BlockSpec((2, gs, tc, p2), lambda g: (0, g, 0, 0))],
        out_specs=(pl.BlockSpec((gs, rows, tc), lambda g: (g, 0, 0)),
                   pl.BlockSpec((2, gs, rows, p2), lambda g: (0, g, 0, 0))),
        compiler_params=_cparams("parallel"),
        name="s5_intra",
    )(ug, toep, s_op)
    st5 = st.reshape(2, ng, nj, batch, p2).transpose(0, 2, 1, 3, 4)
    hin, fin = pl.pallas_call(
        functools.partial(_s5_recur_kernel, n_chunks=nj),
        out_shape=(jax.ShapeDtypeStruct((2, nj, ng, batch, p2), F32),
                   jax.ShapeDtypeStruct((2, ng, batch, p2), F32)),
        grid=(2, ng // gs),
        in_specs=[pl.BlockSpec((None, nj, gs, batch, p2), lambda d, g: (d, 0, g, 0, 0)),
                  pl.BlockSpec((None, gs, 1, p2), lambda d, g: (d, g, 0, 0)),
                  pl.BlockSpec((None, gs, 1, p2), lambda d, g: (d, g, 0, 0)),
                  pl.BlockSpec((None, gs, batch, p2), lambda d, g: (d, g, 0, 0))],
        out_specs=(pl.BlockSpec((None, nj, gs, batch, p2), lambda d, g: (d, 0, g, 0, 0)),
                   pl.BlockSpec((None, gs, batch, p2), lambda d, g: (d, g, 0, 0))),
        compiler_params=_cparams("parallel", "parallel"),
        name="s5_recur",
    )(st5, mul_a.reshape(2, ng, 1, p2), mul_b.reshape(2, ng, 1, p2), h0)
    hin = hin.transpose(0, 2, 1, 3, 4).reshape(2, ng, rows, p2)
    yg = pl.pallas_call(
        _s5_inter_kernel,
        out_shape=jax.ShapeDtypeStruct((ng, rows, tc), F32),
        grid=(ng // gs,),
        in_specs=[pl.BlockSpec((gs, rows, tc), lambda g: (g, 0, 0)),
                  pl.BlockSpec((2, gs, rows, p2), lambda g: (0, g, 0, 0)),
                  pl.BlockSpec((2, gs, p2, tc), lambda g: (0, g, 0, 0))],
        out_specs=pl.BlockSpec((gs, rows, tc), lambda g: (g, 0, 0)),
        compiler_params=_cparams("parallel"),
        name="s5_inter",
    )(y_intra, hin, w_op)
    y = yg.reshape(ng, nj, batch, t, c).transpose(2, 1, 3, 0, 4).reshape(batch * seq, ng * c)
    return y, fin


def _s5_post_kernel(u_ref, y_ref, d_ref, o_ref):
    y = d_ref[...] * u_ref[...] + y_ref[...]
    o_ref[...] = jax.nn.gelu(y)


def s5_post(u, y, d_skip, *, tm=512):
    n, w = u.shape
    return pl.pallas_call(
        _s5_post_kernel,
        out_shape=jax.ShapeDtypeStruct((n, w), F32),
        grid=(n // tm,),
        in_specs=[pl.BlockSpec((tm, w), lambda i: (i, 0)), pl.BlockSpec((tm, w), lambda i: (i, 0)),
                  pl.BlockSpec((1, w), lambda i: (0, 0))],
        out_specs=pl.BlockSpec((tm, w), lambda i: (i, 0)),
        compiler_params=_cparams("parallel"),
        name="s5_post",
    )(u, y, d_skip.reshape(1, w).astype(F32))


def _residual_epilogue(acc, x_ref, gt_ref, *, stream, tm):
    row = stream.mod_row(pl.program_id(0) * tm)
    return x_ref[...] + _mod_row(gt_ref, row) * acc


def out_proj_residual(z, w, x, mods, layer, stream, *, name="out_proj"):
    d = x.shape[1]
    tm, tn = min(MM_TM, x.shape[0]), MM_TN
    return matmul(z, w, tm=tm, tn=tn, name=name,
                  epilogue=functools.partial(_residual_epilogue, stream=stream, tm=tm),
                  extras=[(x, (tm, tn), lambda i, j: (i, j)),
                          (mods, (None, MOD_ROWS, tn), lambda i, j: (layer, 0, 5 * (d // tn) + j))])


def _glu_epilogue(acc, z_ref, b_ref):
    return z_ref[...] * jax.nn.sigmoid(acc + b_ref[...])


def s5_mixer(h, x, mods, layer, stream, state, w_in, ops, d_skip, w_glu, b_glu, w_out):
    u = matmul(h, w_in, name="s5_in")
    y, fin = s5_scan(u, state, ops, stream.batch, stream.seq)
    z = s5_post(u, y, d_skip)
    w = z.shape[1]
    zz = matmul(z, w_glu, out_dtype=BF16, name="s5_glu", epilogue=_glu_epilogue,
                extras=[(z, (min(MM_TM, z.shape[0]), MM_TN), lambda i, j: (i, j)),
                        (b_glu.reshape(1, w).astype(F32), (1, MM_TN), lambda i, j: (0, j))])
    return out_proj_residual(zz, w_out, x, mods, layer, stream, name="s5_out"), fin


SUBLANES = 8
LANES = 128
SCAN_BLOCK = 256


def _conv_rows(x_ref, t0, tb, seq, cw, cb):
    x = x_ref[pl.ds(t0, tb), :]
    prev = x_ref[pl.ds(pl.multiple_of(jnp.maximum(t0 - SUBLANES, 0), SUBLANES), SUBLANES), :]
    nxt = x_ref[pl.ds(pl.multiple_of(jnp.minimum(t0 + tb, seq - SUBLANES), SUBLANES), SUBLANES), :]
    has_prev = t0 > 0
    has_next = t0 + tb < seq
    p1 = jnp.where(has_prev, prev[SUBLANES - 1:SUBLANES, :], 0.0)
    n0 = jnp.where(has_next, nxt[0:1, :], 0.0)
    n1 = jnp.where(has_next, nxt[1:2, :], 0.0)
    rows = lax.broadcasted_iota(jnp.int32, x.shape, 0)
    xm1 = jnp.where(rows == 0, p1, pltpu.roll(x, 1, axis=0))
    xp1 = jnp.where(rows == tb - 1, n0, pltpu.roll(x, tb - 1, axis=0))
    xp2 = jnp.where(rows == tb - 1, n1, jnp.where(rows == tb - 2, n0, pltpu.roll(x, tb - 2, axis=0)))
    return cb + xm1 * cw[0:1, :] + x * cw[1:2, :] + xp1 * cw[2:3, :] + xp2 * cw[3:4, :]


def _group_scan(a, b, reverse):
    tb, tc = a.shape
    a = a.reshape(tb // SUBLANES, SUBLANES, tc)
    b = b.reshape(tb // SUBLANES, SUBLANES, tc)
    pos = lax.broadcasted_iota(jnp.int32, a.shape, 1)
    s = 1
    while s < SUBLANES:
        ok = pos < SUBLANES - s if reverse else pos >= s
        shift = SUBLANES - s if reverse else s
        a_s, b_s = pltpu.roll(a, shift, axis=1), pltpu.roll(b, shift, axis=1)
        b = b + jnp.where(ok, a, 0.0) * b_s
        a = jnp.where(ok, a * a_s, a)
        s *= 2
    return a.reshape(tb, tc), b.reshape(tb, tc)


def _softplus(x):
    return jnp.maximum(x, 0.0) + jnp.log1p(jnp.exp(-jnp.abs(x)))


RG_C = 8.0


def _rg_core_kernel(x_ref, gate_ref, cw_ref, cb_ref, w_ref, bias_ref, lam_ref, h0_ref, o_ref, fin_ref,
                    y_ref, xc_ref, a_ref, b_ref, *, seq):
    tb = a_ref.shape[0]
    tc = a_ref.shape[1]
    n_blocks = seq // tb
    n_groups = tb // SUBLANES
    cw = cw_ref[...]
    cb = cb_ref[...]
    for d in range(2):
        reverse = d == 1
        sp = _softplus(-lam_ref[d:d + 1, :])
        bias_a = bias_ref[d, 0:1, :]
        bias_x = bias_ref[d, 1:2, :]

        def block(k, carry, d=d, reverse=reverse, sp=sp, bias_a=bias_a, bias_x=bias_x):
            t0 = pl.multiple_of((n_blocks - 1 - k if reverse else k) * tb, tb)
            if reverse:
                xc = xc_ref[pl.ds(t0, tb), :]
            else:
                xc = _conv_rows(x_ref, t0, tb, seq, cw, cb)
                xc_ref[pl.ds(t0, tb), :] = xc
            pre = [_bdot(xc[:, n * LANES:(n + 1) * LANES], w_ref[d, n]) for n in range(tc // LANES)]
            r = jax.nn.sigmoid(jnp.concatenate([p[:, :LANES] for p in pre], axis=1) + bias_a)
            ig = jax.nn.sigmoid(jnp.concatenate([p[:, LANES:] for p in pre], axis=1) + bias_x)
            log_a = -RG_C * r * sp
            a = jnp.exp(log_a)
            bv = jnp.sqrt(-jnp.tanh(log_a) * (a * a + 1.0)) * (ig * xc)
            a_ref[...], b_ref[...] = _group_scan(a, bv, reverse)

            def group(i, c):
                r0 = pl.multiple_of((n_groups - 1 - i if reverse else i) * SUBLANES, SUBLANES)
                h = b_ref[pl.ds(r0, SUBLANES), :] + a_ref[pl.ds(r0, SUBLANES), :] * c
                rows = pl.ds(pl.multiple_of(t0 + r0, SUBLANES), SUBLANES)
                if reverse:
                    y_ref[rows, :] += h
                    return h[0:1, :]
                y_ref[rows, :] = h
                return h[SUBLANES - 1:SUBLANES, :]

            carry = lax.fori_loop(0, n_groups, group, carry)
            if reverse:
                o_ref[pl.ds(t0, tb), :] = (y_ref[pl.ds(t0, tb), :] * gate_ref[pl.ds(t0, tb), :]).astype(o_ref.dtype)
            return carry

        fin_ref[d:d + 1, :] = lax.fori_loop(0, n_blocks, block, h0_ref[d:d + 1, :])


def rg_core(xin, gate, conv_w, conv_b, w_gates, bias, lam, h0, stream, *, tc=256):
    n, c = xin.shape
    seq, batch = stream.seq, stream.batch
    tb = min(SCAN_BLOCK, seq)
    nb = tc // LANES
    return pl.pallas_call(
        functools.partial(_rg_core_kernel, seq=seq),
        out_shape=(jax.ShapeDtypeStruct((n, c), BF16), jax.ShapeDtypeStruct((batch, 2, c), F32)),
        grid=(batch, c // tc),
        in_specs=[pl.BlockSpec((seq, tc), lambda b, j: (b, j)),
                  pl.BlockSpec((seq, tc), lambda b, j: (b, j)),
                  pl.BlockSpec((4, tc), lambda b, j: (0, j)),
                  pl.BlockSpec((1, tc), lambda b, j: (0, j)),
                  pl.BlockSpec((2, nb, LANES, 2 * LANES), lambda b, j: (0, j, 0, 0)),
                  pl.BlockSpec((2, 2, tc), lambda b, j: (0, 0, j)),
                  pl.BlockSpec((2, tc), lambda b, j: (0, j)),
                  pl.BlockSpec((None, 2, tc), lambda b, j: (b, 0, j))],
        out_specs=(pl.BlockSpec((seq, tc), lambda b, j: (b, j)),
                   pl.BlockSpec((None, 2, tc), lambda b, j: (b, 0, j))),
        scratch_shapes=[pltpu.VMEM((seq, tc), F32), pltpu.VMEM((seq, tc), F32),
                        pltpu.VMEM((tb, tc), F32), pltpu.VMEM((tb, tc), F32)],
        compiler_params=_cparams("parallel", "parallel"),
        name="rg_core",
    )(xin, gate, conv_w, conv_b.reshape(1, c), w_gates, bias, lam, h0)


def _gelu_epilogue(acc):
    return jax.nn.gelu(acc)


def rg_mixer(h, x, mods, layer, stream, state, w_in, w_gate, conv_w, conv_b, w_gates, bias, lam, w_out):
    xin = matmul(h, w_in, name="rg_in")
    gate = matmul(h, w_gate, epilogue=_gelu_epilogue, name="rg_gate")
    yg, fin = rg_core(xin, gate, conv_w, conv_b, w_gates, bias, lam, state, stream)
    return out_proj_residual(yg, w_out, x, mods, layer, stream, name="rg_out"), fin


GRID_W = 64
ROPE_BASE = 10000.0
ATTN_TQ = 512
ATTN_TK = 512


def rope_tables(seq, dk):
    half = dk // 2
    quarter = half // 2
    pos = jnp.arange(seq)
    inv = ROPE_BASE ** (-jnp.arange(0, half, 2, dtype=F32) / half)
    ang_r = (pos // GRID_W).astype(F32)[:, None] * inv
    ang_c = (pos % GRID_W).astype(F32)[:, None] * inv
    cos = jnp.concatenate([jnp.cos(ang_r)] * 2 + [jnp.cos(ang_c)] * 2, axis=1)
    sin = jnp.concatenate([-jnp.sin(ang_r), jnp.sin(ang_r), -jnp.sin(ang_c), jnp.sin(ang_c)], axis=1)
    assert cos.shape == (seq, 4 * quarter)
    return cos, sin


def _rope_epilogue(acc, cos_ref, sin_ref):
    cos = cos_ref[...]
    sin = sin_ref[...]
    dk = cos.shape[1]
    lane = lax.broadcasted_iota(jnp.int32, cos.shape, 1)
    first = (lane % (dk // 2)) < dk // 4
    outs = []
    for g in range(acc.shape[1] // dk):
        x = acc[:, g * dk:(g + 1) * dk]
        rot = jnp.where(first, pltpu.roll(x, dk - dk // 4, axis=1), pltpu.roll(x, dk // 4, axis=1))
        outs.append(x * cos + rot * sin)
    return jnp.concatenate(outs, axis=1)


def _attn_kernel(*refs, n_tiles, tk, has_cache, lam_init, scale):
    if has_cache:
        q_ref, k_ref, v_ref, ck_ref, cv_ref, lam_ref, g_ref, o_ref = refs[:8]
    else:
        q_ref, k_ref, v_ref, lam_ref, g_ref, o_ref = refs[:6]
    s_ref = refs[-2]
    sc_ref = refs[-1]
    q = q_ref[...]
    tq = q.shape[0]
    dk = q.shape[1] // 2
    dv = v_ref.shape[1]
    maps = range(2)

    def scores(kt):
        return [lax.dot_general(q[:, c * dk:(c + 1) * dk], kt[:, c * dk:(c + 1) * dk],
                                (((1,), (1,)), ((), ())), preferred_element_type=F32) for c in maps]

    m = [jnp.full((tq, 1), -jnp.inf, F32) for _ in maps]
    if has_cache:
        s = scores(ck_ref[...])
        for c in maps:
            sc_ref[c] = s[c]
            m[c] = jnp.maximum(m[c], jnp.max(s[c], axis=-1, keepdims=True))

    def max_pass(i, m):
        s = scores(k_ref[pl.ds(pl.multiple_of(i * tk, tk), tk), :])
        for c in maps:
            s_ref[c, i] = s[c]
        return tuple(jnp.maximum(m[c], jnp.max(s[c], axis=-1, keepdims=True)) for c in maps)

    m = lax.fori_loop(0, n_tiles, max_pass, tuple(m), unroll=2 if n_tiles % 2 == 0 else 1)

    c2 = scale * math.log2(math.e)

    def accumulate(s, vt, st):
        new = []
        for c in maps:
            l, acc = st[c]
            p = jnp.exp2((s[c] - m[c]) * c2)
            new.append((l + jnp.sum(p, axis=-1, keepdims=True),
                        acc + jnp.dot(p.astype(BF16), vt, preferred_element_type=F32)))
        return tuple(new)

    st = tuple((jnp.zeros((tq, 1), F32), jnp.zeros((tq, dv), F32)) for _ in maps)
    if has_cache:
        st = accumulate([sc_ref[c] for c in maps], cv_ref[...], st)

    def sum_pass(i, st):
        return accumulate([s_ref[c, i] for c in maps], v_ref[pl.ds(pl.multiple_of(i * tk, tk), tk), :], st)

    st = lax.fori_loop(0, n_tiles, sum_pass, st)
    lp = lam_ref[...]
    lam = (jnp.exp(jnp.sum(lp[0:1, :] * lp[1:2, :], axis=-1, keepdims=True))
           - jnp.exp(jnp.sum(lp[2:3, :] * lp[3:4, :], axis=-1, keepdims=True)) + lam_init)
    diff = st[0][1] / st[0][0] - lam * (st[1][1] / st[1][0])
    diff = diff * lax.rsqrt(jnp.mean(diff * diff, axis=-1, keepdims=True) + NORM_EPS)
    o_ref[...] = (diff * g_ref[...] * (1.0 - lam_init)).astype(o_ref.dtype)


def diff_attention(q, k, v, *rest, stream, lam_init):
    has_cache = len(rest) == 4
    ck, cv = (rest[0], rest[1]) if has_cache else (None, None)
    da_lam, subln_g = rest[-2], rest[-1]
    n, width = q.shape
    dv = subln_g.shape[0]
    heads = width // dv
    dk = dv // 2
    batch, seq = stream.batch, stream.seq
    tq = min(ATTN_TQ, seq)
    tk = min(ATTN_TK, seq)
    nq = seq // tq
    in_specs = [pl.BlockSpec((tq, dv), lambda b, h, i: (b * nq + i, h)),
                pl.BlockSpec((seq, dv), lambda b, h, i: (b, h)),
                pl.BlockSpec((seq, dv), lambda b, h, i: (b, h))]
    args = [q, k, v]
    if has_cache:
        past = ck.shape[0] // batch
        in_specs += [pl.BlockSpec((past, dv), lambda b, h, i: (b, h))] * 2
        args += [ck, cv]
    in_specs += [pl.BlockSpec(da_lam.shape, lambda b, h, i: (0, 0)), pl.BlockSpec((1, dv), lambda b, h, i: (0, 0))]
    args += [da_lam.astype(F32), subln_g.reshape(1, dv).astype(F32)]
    return pl.pallas_call(
        functools.partial(_attn_kernel, n_tiles=seq // tk, tk=tk, has_cache=has_cache, lam_init=lam_init,
                          scale=1.0 / math.sqrt(dk)),
        out_shape=jax.ShapeDtypeStruct((n, width), BF16),
        grid=(batch, heads, nq),
        in_specs=in_specs,
        out_specs=pl.BlockSpec((tq, dv), lambda b, h, i: (b * nq + i, h)),
        scratch_shapes=[pltpu.VMEM((2, seq // tk, tq, tk), F32),
                        pltpu.VMEM((2, tq, past if has_cache else SUBLANES), F32)],
        compiler_params=_cparams("parallel", "parallel", "parallel"),
        name="diff_attention",
    )(*args)


def _dup_epilogue(acc):
    return acc, acc


def dattn_mixer(h, x, mods, layer, stream, cache, wq, wk, wv, wo, da_lam, subln_g, lam_init):
    dk = subln_g.shape[0] // 2
    if cache is None:
        q = matmul(h, wq, out_dtype=BF16, name="da_q")
        k32, k = matmul(h, wk, out_dtype=(F32, BF16), epilogue=_dup_epilogue, name="da_k")
        v32, v = matmul(h, wv, out_dtype=(F32, BF16), epilogue=_dup_epilogue, name="da_v")
        o = diff_attention(q, k, v, da_lam, subln_g, stream=stream, lam_init=lam_init)
        new = (k32, v32)
    else:
        cos, sin = rope_tables(stream.seq, dk)
        tm = min(MM_TM, stream.seq)
        per = stream.seq // tm
        rope = [(cos, (tm, dk), lambda i, j: (i % per, 0)), (sin, (tm, dk), lambda i, j: (i % per, 0))]
        q = matmul(h, wq, tm=tm, out_dtype=BF16, epilogue=_rope_epilogue, extras=rope, name="da_q_rope")
        k = matmul(h, wk, tm=tm, out_dtype=BF16, epilogue=_rope_epilogue, extras=rope, name="da_k_rope")
        v = matmul(h, wv, out_dtype=BF16, name="da_v")
        o = diff_attention(q, k, v, cache[0], cache[1], da_lam, subln_g, stream=stream, lam_init=lam_init)
        new = None
    return out_proj_residual(o, wo, x, mods, layer, stream, name="da_out"), new


ML_CHUNK = 256


def _ml_conv_kernel(x_ref, cw_ref, cb_ref, o32_ref, o16_ref, *, seq):
    tb = min(SCAN_BLOCK, seq)
    cw = cw_ref[...]
    cb = cb_ref[...]

    def body(i, carry):
        t0 = pl.multiple_of(i * tb, tb)
        y = _conv_rows(x_ref, t0, tb, seq, cw, cb)
        y = y * jax.nn.sigmoid(y)
        o32_ref[pl.ds(t0, tb), :] = y
        o16_ref[pl.ds(t0, tb), :] = y.astype(BF16)
        return carry

    lax.fori_loop(0, seq // tb, body, 0)


def ml_conv(xi, conv_w, conv_b, stream, *, tc=256):
    n, w = xi.shape
    seq = stream.seq
    return pl.pallas_call(
        functools.partial(_ml_conv_kernel, seq=seq),
        out_shape=(jax.ShapeDtypeStruct((n, w), F32), jax.ShapeDtypeStruct((n, w), BF16)),
        grid=(stream.batch, w // tc),
        in_specs=[pl.BlockSpec((seq, tc), lambda b, j: (b, j)),
                  pl.BlockSpec((conv_w.shape[0], tc), lambda b, j: (0, j)),
                  pl.BlockSpec((1, tc), lambda b, j: (0, j))],
        out_specs=(pl.BlockSpec((seq, tc), lambda b, j: (b, j)), pl.BlockSpec((seq, tc), lambda b, j: (b, j))),
        compiler_params=_cparams("parallel", "parallel"),
        name="ml_conv",
    )(xi, conv_w.astype(F32), conv_b.reshape(1, w).astype(F32))


def _ml_gates_kernel(q_ref, k_ref, v_ref, w_ref, b_ref, o_ref, *, n_heads):
    acc = jnp.dot(q_ref[...], w_ref[0], preferred_element_type=F32)
    acc = acc + jnp.dot(k_ref[...], w_ref[1], preferred_element_type=F32)
    acc = acc + jnp.dot(v_ref[...], w_ref[2], preferred_element_type=F32) + b_ref[...]
    col = lax.broadcasted_iota(jnp.int32, acc.shape, 1)
    is_forget = (col // n_heads) % 2 == 1
    o_ref[...] = jnp.where(is_forget, -_softplus(-acc), acc)


def ml_gates(q, k, v, w_gate, b_gate, n_heads, *, tm=512):
    n, w = q.shape
    ng = w_gate.shape[2]
    row = pl.BlockSpec((tm, w), lambda i: (i, 0))
    return pl.pallas_call(
        functools.partial(_ml_gates_kernel, n_heads=n_heads),
        out_shape=jax.ShapeDtypeStruct((n, ng), F32),
        grid=(n // tm,),
        in_specs=[row, row, row, pl.BlockSpec((3, w, ng), lambda i: (0, 0, 0)), pl.BlockSpec((1, ng), lambda i: (0, 0))],
        out_specs=pl.BlockSpec((tm, ng), lambda i: (i, 0)),
        compiler_params=_cparams("parallel"),
        name="ml_gates",
    )(q, k, v, w_gate, b_gate.reshape(1, ng).astype(F32))


ML_HEADS_PER_STEP = 2


def _mlstm_cell_kernel(*refs, n_heads, has_init, want_final):
    q_ref, k_ref, kt_ref, v_ref, g_ref, gt_ref = refs[:6]
    init_refs = refs[6:9] if has_init else None
    h_ref = refs[9 if has_init else 6]
    ct_s, n_s, m_s = refs[-3:]
    final_refs = refs[-6:-3] if want_final else None
    hb = ct_s.shape[0]
    dh = ct_s.shape[1]
    h0 = pl.program_id(1) * hb
    d = pl.program_id(2)
    ci = pl.program_id(3)
    reverse = d == 1
    t = q_ref.shape[0]

    @pl.when(ci == 0)
    def _():
        for j in range(hb):
            if has_init:
                ct_s[j] = init_refs[0][j].T
                n_s[j] = init_refs[1][j]
                m_s[j] = init_refs[2][j]
            else:
                ct_s[j] = jnp.zeros((dh, dh), F32)
                n_s[j] = jnp.zeros((1, dh), F32)
                m_s[j] = jnp.zeros((1, 1), F32)

    g = g_ref[...]
    gt = gt_ref[...]
    lane = lax.broadcasted_iota(jnp.int32, g.shape, 1)
    sub = lax.broadcasted_iota(jnp.int32, gt.shape, 0)
    r_idx = lax.broadcasted_iota(jnp.int32, (t, t), 0)
    s_idx = lax.broadcasted_iota(jnp.int32, (t, t), 1)
    ahead = (s_idx - r_idx) * jnp.where(reverse, -1, 1)
    mask = ahead <= 0
    mask_t = ahead >= 0
    row = lax.broadcasted_iota(jnp.int32, (t, 1), 0)
    last = row == jnp.where(reverse, 0, t - 1)

    for j in range(hb):
        cols = slice(j * dh, (j + 1) * dh)
        q = q_ref[:, cols]
        k = k_ref[:, cols]
        v = v_ref[:, cols]
        kt = kt_ref[cols, :]
        col_i = d * (2 * n_heads) + h0 + j
        col_f = col_i + n_heads
        ig_col = jnp.sum(jnp.where(lane == col_i, g, 0.0), axis=1, keepdims=True)
        lf_col = jnp.sum(jnp.where(lane == col_f, g, 0.0), axis=1, keepdims=True)
        ig_row = jnp.sum(jnp.where(sub == col_i, gt, 0.0), axis=0, keepdims=True)
        lf_row = jnp.sum(jnp.where(sub == col_f, gt, 0.0), axis=0, keepdims=True)
        b_col = jnp.sum(jnp.where(mask, lf_row, 0.0), axis=1, keepdims=True)
        b_row = jnp.sum(jnp.where(mask_t, lf_col, 0.0), axis=0, keepdims=True)
        m_prev = m_s[j]
        dlog = jnp.where(mask, b_col - b_row + ig_row, -jnp.inf)
        inter = b_col + m_prev
        m_t = jnp.maximum(inter, jnp.max(dlog, axis=1, keepdims=True))
        w_intra = jnp.exp(dlog - m_t)
        w_inter = jnp.exp(inter - m_t)
        s_mat = lax.dot_general(q, k, (((1,), (1,)), ((), ())), preferred_element_type=F32) * w_intra
        ct_old = ct_s[j]
        n_old = n_s[j]
        qc = jnp.dot(q, ct_old.astype(BF16), preferred_element_type=F32)
        num = jnp.dot(s_mat.astype(BF16), v, preferred_element_type=F32) + w_inter * qc
        n_rows = jnp.broadcast_to(n_old.astype(BF16), (SUBLANES, dh))
        qn = lax.dot_general(q, n_rows, (((1,), (1,)), ((), ())), preferred_element_type=F32)[:, 0:1]
        den = jnp.sum(s_mat, axis=1, keepdims=True) + w_inter * qn
        h_ref[:, cols] = num / jnp.maximum(jnp.abs(den), jnp.exp(-m_t))
        m_new = jnp.sum(jnp.where(last, m_t, 0.0), axis=0, keepdims=True)
        b_last = jnp.sum(jnp.where(last, b_col, 0.0), axis=0, keepdims=True)
        w_old = jnp.exp(b_last + m_prev - m_new)
        w_in = jnp.exp(b_last - b_col + ig_col - m_new)
        vw = (v.astype(F32) * w_in).astype(BF16)
        ct_s[j] = w_old * ct_old + jnp.dot(kt, vw, preferred_element_type=F32)
        n_s[j] = w_old * n_old + jnp.sum(w_in * k.astype(F32), axis=0, keepdims=True)
        m_s[j] = m_new

    if want_final:
        @pl.when(ci == pl.num_programs(3) - 1)
        def _():
            for j in range(hb):
                final_refs[0][j] = ct_s[j].T
                final_refs[1][j] = n_s[j]
                final_refs[2][j] = m_s[j]


def mlstm_cell(q, k, k_t, v, g, g_t, init, stream, heads, want_final):
    n, w = q.shape
    batch, seq = stream.batch, stream.seq
    dh = w // heads
    hb = ML_HEADS_PER_STEP
    t = min(ML_CHUNK, seq)
    nc = seq // t

    def blk(b, d, ci):
        return b * nc + ci + d * (nc - 1 - 2 * ci)

    tok = pl.BlockSpec((t, hb * dh), lambda b, h, d, ci: (blk(b, d, ci), h))
    st5 = lambda r, c: pl.BlockSpec((None, None, hb, r, c), lambda b, h, d, ci: (b, d, h, 0, 0))
    state_dims = ((dh, dh), (1, dh), (1, 1))
    out_shape = [jax.ShapeDtypeStruct((2, n, w), F32)]
    out_specs = [pl.BlockSpec((None, t, hb * dh), lambda b, h, d, ci: (d, blk(b, d, ci), h))]
    if want_final:
        out_shape += [jax.ShapeDtypeStruct((batch, 2, heads) + rc, F32) for rc in state_dims]
        out_specs += [st5(*rc) for rc in state_dims]
    res = pl.pallas_call(
        functools.partial(_mlstm_cell_kernel, n_heads=heads, has_init=init is not None, want_final=want_final),
        out_shape=tuple(out_shape),
        grid=(batch, heads // hb, 2, nc),
        in_specs=[tok, tok,
                  pl.BlockSpec((hb * dh, t), lambda b, h, d, ci: (h, blk(b, d, ci))),
                  tok,
                  pl.BlockSpec((t, g.shape[1]), lambda b, h, d, ci: (blk(b, d, ci), 0)),
                  pl.BlockSpec((g.shape[1], t), lambda b, h, d, ci: (0, blk(b, d, ci)))]
        + ([st5(*rc) for rc in state_dims] if init is not None else []),
        out_specs=tuple(out_specs),
        scratch_shapes=[pltpu.VMEM((hb,) + rc, F32) for rc in state_dims],
        compiler_params=_cparams("parallel", "parallel", "arbitrary", "arbitrary"),
        name="mlstm_cell",
    )(q, k, k_t, v, g, g_t, *(init or ()))
    return res[0], (tuple(res[1:]) if want_final else None)


def _mlstm_post_kernel(h_ref, o_ref, xc_ref, gn_ref, skip_ref, y_ref):
    hc = o_ref[...] * (h_ref[0] + h_ref[1])
    hn = hc * lax.rsqrt(jnp.mean(hc * hc, axis=-1, keepdims=True) + NORM_EPS)
    y_ref[...] = (hn * gn_ref[...] + skip_ref[...] * xc_ref[...]).astype(y_ref.dtype)


def mlstm_post(hdir, o, xc, gn, skip, dh, *, tm=512):
    n, w = o.shape
    tile = pl.BlockSpec((tm, dh), lambda i, h: (i, h))
    vec = pl.BlockSpec((1, dh), lambda i, h: (0, h))
    return pl.pallas_call(
        _mlstm_post_kernel,
        out_shape=jax.ShapeDtypeStruct((n, w), BF16),
        grid=(n // tm, w // dh),
        in_specs=[pl.BlockSpec((2, tm, dh), lambda i, h: (0, i, h)), tile, tile, vec, vec],
        out_specs=tile,
        compiler_params=_cparams("parallel", "parallel"),
        name="mlstm_post",
    )(hdir, o, xc, gn.reshape(1, w).astype(F32), skip.reshape(1, w).astype(F32))


def _scale_epilogue(acc, *, scale):
    return acc * scale


def _sigmoid_bias_epilogue(acc, b_ref):
    return jax.nn.sigmoid(acc + b_ref[...])


def mlstm_mixer(h, x, mods, layer, stream, init, want_final, w_up, conv_w, conv_b, wq, wk, wv, w_gate, b_gate,
                w_o, b_o, gn, skip, w_down):
    dh = wq.shape[1]
    xi = matmul(h, w_up, name="ml_up")
    xc, xc16 = ml_conv(xi, conv_w, conv_b, stream)
    q = matmul(xc16, wq, tm=MM_TM_DIAG, tn=dh, out_dtype=BF16, k_block=dh, name="ml_q")
    k = matmul(xc16, wk, tm=MM_TM_DIAG, tn=dh, out_dtype=BF16, k_block=dh, name="ml_k",
               epilogue=functools.partial(_scale_epilogue, scale=1.0 / math.sqrt(dh)))
    v = matmul(xi, wv, tm=MM_TM_DIAG, tn=dh, out_dtype=BF16, k_block=dh, name="ml_v")
    w = xi.shape[1]
    g = ml_gates(q, k, v, w_gate, b_gate, w // dh)
    hdir, final = mlstm_cell(q, k, k.T, v, g, g.T, init, stream, w // dh, want_final)
    o = matmul(h, w_o, name="ml_o", epilogue=_sigmoid_bias_epilogue,
               extras=[(b_o.reshape(1, w).astype(F32), (1, MM_TN), lambda i, j: (0, j))])
    y = mlstm_post(hdir, o, xc, gn, skip, dh)
    return out_proj_residual(y, w_down, x, mods, layer, stream, name="ml_down"), final


def kernel(x_prompt, x_sample, state_s5, state_rglru, cache_dattn_k, cache_dattn_v, state_mlstm_C, state_mlstm_n, state_mlstm_m, c, c_ctx, ada_w, ada_b, norm_g, ffn1_w1, ffn1_w3, ffn1_w2, ffn2_w1, ffn2_w3, ffn2_w2, final_norm_g, s5_w_in, s5_a_re, s5_a_im, s5_log_dt, s5_b_re, s5_b_im, s5_c_re, s5_c_im, s5_d, s5_w_glu, s5_b_glu, s5_w_out, rg_w_in, rg_w_gate, rg_conv_w, rg_conv_b, rg_wa, rg_ba, rg_wx, rg_bx, rg_lam, rg_w_out, da_wq, da_wk, da_wv, da_wo, da_lam, da_subln_g, ml_w_up, ml_conv_w, ml_conv_b, ml_wq, ml_wk, ml_wv, ml_w_gate, ml_b_gate, ml_w_o, ml_b_o, ml_gn, ml_skip, ml_w_down):
    bc, lc, d = x_prompt.shape
    bl, ll, _ = x_sample.shape
    depth = ada_w.shape[0]
    cs = Stream(bc, lc, 0, False)
    ls = Stream(bl, ll, 1, True)
    ctx = x_prompt.reshape(cs.n, d)
    lat = x_sample.reshape(ls.n, d)
    c_all = jnp.concatenate([c_ctx[None, :], c, jnp.zeros((MOD_ROWS - 1 - bl, d), F32)], axis=0)
    mods = adaln(c_all, ada_w, ada_b)
    bf = lambda a: a.astype(BF16)

    new_s5 = new_rg = None
    for i in range(depth):
        w1, w3, w2 = bf(ffn1_w1[i]), bf(ffn1_w3[i]), bf(ffn1_w2[i])
        ctx = ffn_half(ctx, norm_g[i, 0], mods, i, 0, w1, w3, w2, cs)
        lat = ffn_half(lat, norm_g[i, 0], mods, i, 0, w1, w3, w2, ls)
        hc = modnorm(ctx, norm_g[i, 1], mods, i, 3, cs)
        hl = modnorm(lat, norm_g[i, 1], mods, i, 3, ls)
        kind = i % 4
        if kind == 0:
            ops = _s5_operators(s5_a_re, s5_a_im, s5_log_dt, s5_b_re, s5_b_im, s5_c_re, s5_c_im)
            ng, p = s5_a_re.shape[1], s5_a_re.shape[2]
            h0 = jnp.concatenate([state_s5[:, :, 0], state_s5[:, :, 1]], axis=-1).transpose(1, 2, 0, 3)
            args = (bf(s5_w_in), ops, s5_d, bf(s5_w_glu), s5_b_glu, bf(s5_w_out))
            ctx, fin = s5_mixer(hc, ctx, mods, i, cs, jnp.zeros((2, ng, bc, 2 * p), F32), *args)
            lat, _ = s5_mixer(hl, lat, mods, i, ls, h0.astype(F32), *args)
            fin = fin.transpose(2, 0, 1, 3)
            new_s5 = jnp.stack([fin[..., :p], fin[..., p:]], axis=2)
        elif kind == 1:
            w_gates = bf(jnp.concatenate([rg_wa, rg_wx], axis=-1))
            bias = jnp.stack([rg_ba, rg_bx], axis=1).astype(F32)
            args = (bf(rg_w_in), bf(rg_w_gate), rg_conv_w, rg_conv_b, w_gates, bias, rg_lam, bf(rg_w_out))
            ctx, new_rg = rg_mixer(hc, ctx, mods, i, cs, jnp.zeros((bc, 2, rg_lam.shape[1]), F32), *args)
            lat, _ = rg_mixer(hl, lat, mods, i, ls, state_rglru, *args)
        elif kind == 2:
            lam_init = 0.8 - 0.6 * math.exp(-0.3 * i)
            heads, dv = cache_dattn_v.shape[2], cache_dattn_v.shape[3]
            args = (bf(da_wq), bf(da_wk), bf(da_wv), bf(da_wo), da_lam, da_subln_g, lam_init)
            ctx, (k32, v32) = dattn_mixer(hc, ctx, mods, i, cs, None, *args)
            new_k = k32.reshape(bc, lc, heads, 2, dv // 2)
            new_v = v32.reshape(bc, lc, heads, dv)
            cache = (bf(cache_dattn_k).reshape(-1, heads * dv), bf(cache_dattn_v).reshape(-1, heads * dv))
            lat, _ = dattn_mixer(hl, lat, mods, i, ls, cache, *args)
        else:
            heads, dh = ml_wq.shape[0], ml_wq.shape[1]
            args = (bf(ml_w_up), ml_conv_w, ml_conv_b, bf(ml_wq).reshape(heads * dh, dh),
                    bf(ml_wk).reshape(heads * dh, dh), bf(ml_wv).reshape(heads * dh, dh), bf(ml_w_gate), ml_b_gate,
                    bf(ml_w_o), ml_b_o, ml_gn, ml_skip, bf(ml_w_down))
            ctx, (cf, nf, mf) = mlstm_mixer(hc, ctx, mods, i, cs, None, True, *args)
            init = (state_mlstm_C.astype(F32), state_mlstm_n.astype(F32).reshape(bl, 2, heads, 1, dh),
                    state_mlstm_m.astype(F32).reshape(bl, 2, heads, 1, 1))
            lat, _ = mlstm_mixer(hl, lat, mods, i, ls, init, False, *args)
            new_c, new_n, new_m = cf, nf.reshape(bc, 2, heads, dh), mf.reshape(bc, 2, heads)
        w1, w3, w2 = bf(ffn2_w1[i]), bf(ffn2_w3[i]), bf(ffn2_w2[i])
        ctx = ffn_half(ctx, norm_g[i, 2], mods, i, 6, w1, w3, w2, cs)
        lat = ffn_half(lat, norm_g[i, 2], mods, i, 6, w1, w3, w2, ls)
    y_prompt = rmsnorm(ctx, final_norm_g).reshape(bc, lc, d)
    y_sample = rmsnorm(lat, final_norm_g).reshape(bl, ll, d)
    return (y_prompt, y_sample, new_s5, new_rg, new_k, new_v, new_c, new_n, new_m)
```

```python
import functools
import math

import jax
import jax.numpy as jnp
from jax import lax
from jax.experimental import pallas as pl
from jax.experimental.pallas import tpu as pltpu

F32 = jnp.float32
BF16 = jnp.bfloat16

NORM_EPS = 1e-6
VMEM_LIMIT_BYTES = 56 * 1024 * 1024
MOD_ROWS = 8


def _cparams(*sem):
    return pltpu.CompilerParams(dimension_semantics=sem, vmem_limit_bytes=VMEM_LIMIT_BYTES)


def _bdot(a, b):
    return jnp.dot(a.astype(BF16), b.astype(BF16), preferred_element_type=F32)


def _mod_row(ref, row):
    return ref[pl.ds(row, 1), :]


class Stream:
    def __init__(self, batch, seq, row0, per_batch_mod):
        self.batch, self.seq, self.row0 = batch, seq, row0
        self.n = batch * seq
        self.rows_per_mod = seq if per_batch_mod else self.n

    def mod_row(self, first_token):
        return self.row0 + first_token // self.rows_per_mod


def _adaln_kernel(c_ref, w_ref, b_ref, o_ref):
    c = c_ref[...]
    s = c * jax.nn.sigmoid(c)
    o_ref[...] = _bdot(s, w_ref[...]) + b_ref[...]


def adaln(c_all, ada_w, ada_b, *, tn=1024):
    nl, d, m = ada_w.shape
    return pl.pallas_call(
        _adaln_kernel,
        out_shape=jax.ShapeDtypeStruct((nl, MOD_ROWS, m), F32),
        grid=(nl, m // tn),
        in_specs=[pl.BlockSpec((MOD_ROWS, d), lambda l, j: (0, 0)),
                  pl.BlockSpec((None, d, tn), lambda l, j: (l, 0, j)),
                  pl.BlockSpec((None, 1, tn), lambda l, j: (l, 0, j))],
        out_specs=pl.BlockSpec((None, MOD_ROWS, tn), lambda l, j: (l, 0, j)),
        compiler_params=_cparams("parallel", "parallel"),
        name="adaln",
    )(c_all, ada_w, ada_b.reshape(nl, 1, m))


def _mod_spec(d, layer, k):
    return pl.BlockSpec((None, MOD_ROWS, d), lambda *_: (layer, 0, k))


def _rms(x, g):
    return x * lax.rsqrt(jnp.mean(x * x, axis=-1, keepdims=True) + NORM_EPS) * g


def _modnorm_kernel(x_ref, g_ref, sh_ref, sc_ref, o_ref, *, stream, tm):
    row = stream.mod_row(pl.program_id(0) * tm)
    y = _rms(x_ref[...], g_ref[...])
    o_ref[...] = (y * (1.0 + _mod_row(sc_ref, row)) + _mod_row(sh_ref, row)).astype(o_ref.dtype)


def modnorm(x, g, mods, layer, k_shift, stream, *, tm=512):
    n, d = x.shape
    return pl.pallas_call(
        functools.partial(_modnorm_kernel, stream=stream, tm=tm),
        out_shape=jax.ShapeDtypeStruct((n, d), BF16),
        grid=(n // tm,),
        in_specs=[pl.BlockSpec((tm, d), lambda i: (i, 0)),
                  pl.BlockSpec((1, d), lambda i: (0, 0)),
                  _mod_spec(d, layer, k_shift),
                  _mod_spec(d, layer, k_shift + 1)],
        out_specs=pl.BlockSpec((tm, d), lambda i: (i, 0)),
        compiler_params=_cparams("parallel"),
        name="modnorm",
    )(x, g.reshape(1, d), mods, mods)


def _rmsnorm_kernel(x_ref, g_ref, o_ref):
    o_ref[...] = _rms(x_ref[...], g_ref[...])


def rmsnorm(x, g, *, tm=512):
    n, d = x.shape
    return pl.pallas_call(
        _rmsnorm_kernel,
        out_shape=jax.ShapeDtypeStruct((n, d), F32),
        grid=(n // tm,),
        in_specs=[pl.BlockSpec((tm, d), lambda i: (i, 0)), pl.BlockSpec((1, d), lambda i: (0, 0))],
        out_specs=pl.BlockSpec((tm, d), lambda i: (i, 0)),
        compiler_params=_cparams("parallel"),
        name="rmsnorm",
    )(x, g.reshape(1, d))


def _ffn_kernel(x_ref, g_ref, sh_ref, sc_ref, gt_ref, w1_ref, w3_ref, w2_ref, o_ref, h_ref, acc_ref,
                *, stream, tm):
    f = pl.program_id(1)
    row = stream.mod_row(pl.program_id(0) * tm)

    @pl.when(f == 0)
    def _():
        y = _rms(x_ref[...], g_ref[...])
        h_ref[...] = (y * (1.0 + _mod_row(sc_ref, row)) + _mod_row(sh_ref, row)).astype(BF16)
        acc_ref[...] = jnp.zeros_like(acc_ref)

    h = h_ref[...]
    a = jnp.dot(h, w1_ref[...], preferred_element_type=F32)
    b = jnp.dot(h, w3_ref[...], preferred_element_type=F32)
    u = (a * jax.nn.sigmoid(a)) * b
    acc_ref[...] += jnp.dot(u.astype(BF16), w2_ref[...], preferred_element_type=F32)

    @pl.when(f == pl.num_programs(1) - 1)
    def _():
        o_ref[...] = x_ref[...] + (0.5 * _mod_row(gt_ref, row)) * acc_ref[...]


def ffn_half(x, g, mods, layer, k_shift, w1, w3, w2, stream, *, tm=512, tf=512):
    n, d = x.shape
    dff = w1.shape[2]
    return pl.pallas_call(
        functools.partial(_ffn_kernel, stream=stream, tm=tm),
        out_shape=jax.ShapeDtypeStruct((n, d), F32),
        grid=(n // tm, dff // tf),
        in_specs=[pl.BlockSpec((tm, d), lambda i, f: (i, 0)),
                  pl.BlockSpec((1, d), lambda i, f: (0, 0)),
                  _mod_spec(d, layer, k_shift),
                  _mod_spec(d, layer, k_shift + 1),
                  _mod_spec(d, layer, k_shift + 2),
                  pl.BlockSpec((None, d, tf), lambda i, f: (layer, 0, f)),
                  pl.BlockSpec((None, d, tf), lambda i, f: (layer, 0, f)),
                  pl.BlockSpec((None, tf, d), lambda i, f: (layer, f, 0))],
        out_specs=pl.BlockSpec((tm, d), lambda i, f: (i, 0)),
        scratch_shapes=[pltpu.VMEM((tm, d), BF16), pltpu.VMEM((tm, d), F32)],
        compiler_params=_cparams("parallel", "arbitrary"),
        name="ffn_half",
    )(x, g.reshape(1, d), mods, mods, mods, w1, w3, w2)


def _mm_kernel(x_ref, w_ref, *rest, epilogue, n_extra):
    extra, o_refs = rest[:n_extra], rest[n_extra:]
    acc = _bdot(x_ref[...], w_ref[...])
    outs = epilogue(acc, *extra) if epilogue is not None else acc
    if not isinstance(outs, tuple):
        outs = (outs,)
    for o_ref, o in zip(o_refs, outs):
        o_ref[...] = o.astype(o_ref.dtype)


MM_TM = 1024
MM_TN = 512
MM_TM_DIAG = 2048


def matmul(x, w, *, tm=MM_TM, tn=MM_TN, out_dtype=F32, epilogue=None, extras=(), k_block=None, name="matmul"):
    n, kx = x.shape
    tm = min(tm, n)
    if k_block is None:
        kb, m = kx, w.shape[1]
        x_map = lambda i, j: (i, 0)
        w_map = lambda i, j: (0, j)
    else:
        kb = k_block
        per = w.shape[1] // tn
        m = (kx // kb) * w.shape[1]
        x_map = lambda i, j: (i, j // per)
        w_map = lambda i, j: (j // per, j % per)
    dts = out_dtype if isinstance(out_dtype, tuple) else (out_dtype,)
    out_shape = tuple(jax.ShapeDtypeStruct((n, m), dt) for dt in dts)
    out_specs = tuple(pl.BlockSpec((tm, tn), lambda i, j: (i, j)) for _ in dts)
    res = pl.pallas_call(
        functools.partial(_mm_kernel, epilogue=epilogue, n_extra=len(extras)),
        out_shape=out_shape,
        grid=(n // tm, m // tn),
        in_specs=[pl.BlockSpec((tm, kb), x_map), pl.BlockSpec((kb, tn), w_map)]
        + [pl.BlockSpec(bs, im) for _, bs, im in extras],
        out_specs=out_specs,
        compiler_params=_cparams("parallel", "parallel"),
        name=name,
    )(x, w, *[a for a, _, _ in extras])
    return res if isinstance(out_dtype, tuple) else res[0]


S5_CHUNK = 16
S5_GROUPS_PER_STEP = 8


def _s5_operators(a_re, a_im, log_dt, b_re, b_im, c_re, c_im):
    hp = lax.Precision.HIGHEST
    t = S5_CHUNK
    a_re, a_im, b_re, b_im, c_re, c_im = (z.astype(F32) for z in (a_re, a_im, b_re, b_im, c_re, c_im))
    dt = jnp.exp(log_dt.astype(F32))[..., None]
    mag = jnp.exp(a_re * dt)
    lr = mag * jnp.cos(a_im * dt)
    li = mag * jnp.sin(a_im * dt)
    den = a_re * a_re + a_im * a_im
    cr = ((lr - 1.0) * a_re + li * a_im) / den
    ci = (li * a_re - (lr - 1.0) * a_im) / den
    bbr = cr[..., None] * b_re - ci[..., None] * b_im
    bbi = cr[..., None] * b_im + ci[..., None] * b_re
    k = jnp.arange(t + 1, dtype=F32)[None, None, :, None]
    pmag = jnp.exp(k * (a_re * dt)[:, :, None, :])
    pr = pmag * jnp.cos(k * (a_im * dt)[:, :, None, :])
    pi = pmag * jnp.sin(k * (a_im * dt)[:, :, None, :])
    pbr = pr[..., None] * bbr[:, :, None] - pi[..., None] * bbi[:, :, None]
    pbi = pr[..., None] * bbi[:, :, None] + pi[..., None] * bbr[:, :, None]
    m = (jnp.einsum('dgop,dgkpi->dgkoi', c_re, pbr, precision=hp)
         - jnp.einsum('dgop,dgkpi->dgkoi', c_im, pbi, precision=hp))
    m = m.at[:, :, t].set(0.0)
    s_idx = jnp.arange(t)[:, None]
    t_idx = jnp.arange(t)[None, :]
    lag_f = jnp.where(t_idx >= s_idx, t_idx - s_idx, t)
    lag_b = jnp.where(s_idx >= t_idx, s_idx - t_idx, t)

    def toeplitz(md, lag):
        x = md[:, lag]
        return x.transpose(0, 1, 4, 2, 3).reshape(md.shape[0], t * x.shape[-1], t * x.shape[-2])

    toep = toeplitz(m[0], lag_f) + toeplitz(m[1], lag_b)
    pow_s = jnp.stack([t - 1 - jnp.arange(t), jnp.arange(t)])

    def smat(d):
        r = pbr[d][:, pow_s[d]]
        i = pbi[d][:, pow_s[d]]
        x = jnp.concatenate([r, i], axis=2)
        return x.transpose(0, 1, 3, 2).reshape(x.shape[0], t * x.shape[3], x.shape[2])

    s_op = jnp.stack([smat(0), smat(1)])
    pow_w = jnp.stack([jnp.arange(t) + 1, t - jnp.arange(t)])

    def wmat(d):
        zr = (c_re[d][:, None] * pr[d][:, pow_w[d]][:, :, None, :]
              - c_im[d][:, None] * pi[d][:, pow_w[d]][:, :, None, :])
        zi = (c_re[d][:, None] * pi[d][:, pow_w[d]][:, :, None, :]
              + c_im[d][:, None] * pr[d][:, pow_w[d]][:, :, None, :])
        x = jnp.concatenate([zr, -zi], axis=3)
        return x.transpose(0, 3, 1, 2).reshape(x.shape[0], x.shape[3], t * x.shape[2])

    w_op = jnp.stack([wmat(0), wmat(1)])
    mul_a = jnp.concatenate([pr[:, :, t], pr[:, :, t]], axis=-1)
    mul_b = jnp.concatenate([-pi[:, :, t], pi[:, :, t]], axis=-1)
    return toep.astype(BF16), s_op.astype(BF16), w_op.astype(BF16), mul_a, mul_b


def _s5_intra_kernel(u_ref, toep_ref, s_ref, y_ref, st_ref):
    for g in range(u_ref.shape[0]):
        u = u_ref[g]
        y_ref[g] = jnp.dot(u, toep_ref[g], preferred_element_type=F32)
        for d in range(2):
            st_ref[d, g] = jnp.dot(u, s_ref[d, g], preferred_element_type=F32)


def _s5_recur_kernel(st_ref, a_ref, b_ref, h0_ref, hin_ref, fin_ref, *, n_chunks):
    rev = pl.program_id(0) == 1
    half = a_ref.shape[-1] // 2
    a = a_ref[...]
    b = b_ref[...]
    b_sw = pltpu.roll(b, half, axis=2)

    def body(i, carry):
        h, h_sw = carry
        j = jnp.where(rev, n_chunks - 1 - i, i)
        hin_ref[j] = h
        s = st_ref[j]
        return h * a + h_sw * b + s, h_sw * a + h * b_sw + pltpu.roll(s, half, axis=2)

    h0 = h0_ref[...]
    fin_ref[...] = lax.fori_loop(0, n_chunks, body, (h0, pltpu.roll(h0, half, axis=2)))[0]


def _s5_inter_kernel(y_ref, hin_ref, w_ref, o_ref):
    for g in range(y_ref.shape[0]):
        acc = y_ref[g]
        for d in range(2):
            acc = acc + jnp.dot(hin_ref[d, g].astype(BF16), w_ref[d, g], preferred_element_type=F32)
        o_ref[g] = acc


def s5_scan(u, h0, ops, batch, seq):
    toep, s_op, w_op, mul_a, mul_b = ops
    ng, tc, _ = toep.shape
    t = S5_CHUNK
    c = tc // t
    p2 = s_op.shape[-1]
    nj = seq // t
    rows = nj * batch
    gs = S5_GROUPS_PER_STEP
    ug = u.astype(BF16).reshape(batch, nj, t, ng, c).transpose(3, 1, 0, 2, 4).reshape(ng, rows, tc)
    y_intra, st = pl.pallas_call(
        _s5_intra_kernel,
        out_shape=(jax.ShapeDtypeStruct((ng, rows, tc), F32), jax.ShapeDtypeStruct((2, ng, rows, p2), F32)),
        grid=(ng // gs,),
        in_specs=[pl.BlockSpec((gs, rows, tc), lambda g: (g, 0, 0)),
                  pl.BlockSpec((gs, tc, tc), lambda g: (g, 0, 0)),
                  pl.BlockSpec((2, gs, tc, p2), lambda g: (0, g, 0, 0))],
        out_specs=(pl.BlockSpec((gs, rows, tc), lambda g: (g, 0, 0)),
                   pl.BlockSpec((2, gs, rows, p2), lambda g: (0, g, 0, 0))),
        compiler_params=_cparams("parallel"),
        name="s5_intra",
    )(ug, toep, s_op)
    st5 = st.reshape(2, ng, nj, batch, p2).transpose(0, 2, 1, 3, 4)
    hin, fin = pl.pallas_call(
        functools.partial(_s5_recur_kernel, n_chunks=nj),
        out_shape=(jax.ShapeDtypeStruct((2, nj, ng, batch, p2), F32),
                   jax.ShapeDtypeStruct((2, ng, batch, p2), F32)),
        grid=(2, ng // gs),
        in_specs=[pl.BlockSpec((None, nj, gs, batch, p2), lambda d, g: (d, 0, g, 0, 0)),
                  pl.BlockSpec((None, gs, 1, p2), lambda d, g: (d, g, 0, 0)),
                  pl.BlockSpec((None, gs, 1, p2), lambda d, g: (d, g, 0, 0)),
                  pl.BlockSpec((None, gs, batch, p2), lambda d, g: (d, g, 0, 0))],
        out_specs=(pl.BlockSpec((None, nj, gs, batch, p2), lambda d, g: (d, 0, g, 0, 0)),
                   pl.BlockSpec((None, gs, batch, p2), lambda d, g: (d, g, 0, 0))),
        compiler_params=_cparams("parallel", "parallel"),
        name="s5_recur",
    )(st5, mul_a.reshape(2, ng, 1, p2), mul_b.reshape(2, ng, 1, p2), h0)
    hin = hin.transpose(0, 2, 1, 3, 4).reshape(2, ng, rows, p2)
    yg = pl.pallas_call(
        _s5_inter_kernel,
        out_shape=jax.ShapeDtypeStruct((ng, rows, tc), F32),
        grid=(ng // gs,),
        in_specs=[pl.BlockSpec((gs, rows, tc), lambda g: (g, 0, 0)),
                  pl.BlockSpec((2, gs, rows, p2), lambda g: (0, g, 0, 0)),
                  pl.BlockSpec((2, gs, p2, tc), lambda g: (0, g, 0, 0))],
        out_specs=pl.BlockSpec((gs, rows, tc), lambda g: (g, 0, 0)),
        compiler_params=_cparams("parallel"),
        name="s5_inter",
    )(y_intra, hin, w_op)
    y = yg.reshape(ng, nj, batch, t, c).transpose(2, 1, 3, 0, 4).reshape(batch * seq, ng * c)
    return y, fin


def _s5_post_kernel(u_ref, y_ref, d_ref, o_ref):
    y = d_ref[...] * u_ref[...] + y_ref[...]
    o_ref[...] = jax.nn.gelu(y)


def s5_post(u, y, d_skip, *, tm=512):
    n, w = u.shape
    return pl.pallas_call(
        _s5_post_kernel,
        out_shape=jax.ShapeDtypeStruct((n, w), F32),
        grid=(n // tm,),
        in_specs=[pl.BlockSpec((tm, w), lambda i: (i, 0)), pl.BlockSpec((tm, w), lambda i: (i, 0)),
                  pl.BlockSpec((1, w), lambda i: (0, 0))],
        out_specs=pl.BlockSpec((tm, w), lambda i: (i, 0)),
        compiler_params=_cparams("parallel"),
        name="s5_post",
    )(u, y, d_skip.reshape(1, w).astype(F32))


def _residual_epilogue(acc, x_ref, gt_ref, *, stream, tm):
    row = stream.mod_row(pl.program_id(0) * tm)
    return x_ref[...] + _mod_row(gt_ref, row) * acc


def out_proj_residual(z, w, x, mods, layer, stream, *, name="out_proj"):
    d = x.shape[1]
    tm, tn = min(MM_TM, x.shape[0]), MM_TN
    return matmul(z, w, tm=tm, tn=tn, name=name,
                  epilogue=functools.partial(_residual_epilogue, stream=stream, tm=tm),
                  extras=[(x, (tm, tn), lambda i, j: (i, j)),
                          (mods, (None, MOD_ROWS, tn), lambda i, j: (layer, 0, 5 * (d // tn) + j))])


def _glu_epilogue(acc, z_ref, b_ref):
    return z_ref[...] * jax.nn.sigmoid(acc + b_ref[...])


def s5_mixer(h, x, mods, layer, stream, state, w_in, ops, d_skip, w_glu, b_glu, w_out):
    u = matmul(h, w_in, name="s5_in")
    y, fin = s5_scan(u, state, ops, stream.batch, stream.seq)
    z = s5_post(u, y, d_skip)
    w = z.shape[1]
    zz = matmul(z, w_glu, out_dtype=BF16, name="s5_glu", epilogue=_glu_epilogue,
                extras=[(z, (min(MM_TM, z.shape[0]), MM_TN), lambda i, j: (i, j)),
                        (b_glu.reshape(1, w).astype(F32), (1, MM_TN), lambda i, j: (0, j))])
    return out_proj_residual(zz, w_out, x, mods, layer, stream, name="s5_out"), fin


SUBLANES = 8
LANES = 128
SCAN_BLOCK = 256


def _conv_rows(x_ref, t0, tb, seq, cw, cb):
    x = x_ref[pl.ds(t0, tb), :]
    prev = x_ref[pl.ds(pl.multiple_of(jnp.maximum(t0 - SUBLANES, 0), SUBLANES), SUBLANES), :]
    nxt = x_ref[pl.ds(pl.multiple_of(jnp.minimum(t0 + tb, seq - SUBLANES), SUBLANES), SUBLANES), :]
    has_prev = t0 > 0
    has_next = t0 + tb < seq
    p1 = jnp.where(has_prev, prev[SUBLANES - 1:SUBLANES, :], 0.0)
    n0 = jnp.where(has_next, nxt[0:1, :], 0.0)
    n1 = jnp.where(has_next, nxt[1:2, :], 0.0)
    rows = lax.broadcasted_iota(jnp.int32, x.shape, 0)
    xm1 = jnp.where(rows == 0, p1, pltpu.roll(x, 1, axis=0))
    xp1 = jnp.where(rows == tb - 1, n0, pltpu.roll(x, tb - 1, axis=0))
    xp2 = jnp.where(rows == tb - 1, n1, jnp.where(rows == tb - 2, n0, pltpu.roll(x, tb - 2, axis=0)))
    return cb + xm1 * cw[0:1, :] + x * cw[1:2, :] + xp1 * cw[2:3, :] + xp2 * cw[3:4, :]


def _group_scan(a, b, reverse):
    tb, tc = a.shape
    a = a.reshape(tb // SUBLANES, SUBLANES, tc)
    b = b.reshape(tb // SUBLANES, SUBLANES, tc)
    pos = lax.broadcasted_iota(jnp.int32, a.shape, 1)
    s = 1
    while s < SUBLANES:
        ok = pos < SUBLANES - s if reverse else pos >= s
        shift = SUBLANES - s if reverse else s
        a_s, b_s = pltpu.roll(a, shift, axis=1), pltpu.roll(b, shift, axis=1)
        b = b + jnp.where(ok, a, 0.0) * b_s
        a = jnp.where(ok, a * a_s, a)
        s *= 2
    return a.reshape(tb, tc), b.reshape(tb, tc)


def _softplus(x):
    return jnp.maximum(x, 0.0) + jnp.log1p(jnp.exp(-jnp.abs(x)))


RG_C = 8.0


def _rg_core_kernel(x_ref, gate_ref, cw_ref, cb_ref, w_ref, bias_ref, lam_ref, h0_ref, o_ref, fin_ref,
                    y_ref, xc_ref, a_ref, b_ref, *, seq):
    tb = a_ref.shape[0]
    tc = a_ref.shape[1]
    n_blocks = seq // tb
    n_groups = tb // SUBLANES
    cw = cw_ref[...]
    cb = cb_ref[...]
    for d in range(2):
        reverse = d == 1
        sp = _softplus(-lam_ref[d:d + 1, :])
        bias_a = bias_ref[d, 0:1, :]
        bias_x = bias_ref[d, 1:2, :]

        def block(k, carry, d=d, reverse=reverse, sp=sp, bias_a=bias_a, bias_x=bias_x):
            t0 = pl.multiple_of((n_blocks - 1 - k if reverse else k) * tb, tb)
            if reverse:
                xc = xc_ref[pl.ds(t0, tb), :]
            else:
                xc = _conv_rows(x_ref, t0, tb, seq, cw, cb)
                xc_ref[pl.ds(t0, tb), :] = xc
            pre = [_bdot(xc[:, n * LANES:(n + 1) * LANES], w_ref[d, n]) for n in range(tc // LANES)]
            r = jax.nn.sigmoid(jnp.concatenate([p[:, :LANES] for p in pre], axis=1) + bias_a)
            ig = jax.nn.sigmoid(jnp.concatenate([p[:, LANES:] for p in pre], axis=1) + bias_x)
            log_a = -RG_C * r * sp
            a = jnp.exp(log_a)
            bv = jnp.sqrt(-jnp.tanh(log_a) * (a * a + 1.0)) * (ig * xc)
            a_ref[...], b_ref[...] = _group_scan(a, bv, reverse)

            def group(i, c):
                r0 = pl.multiple_of((n_groups - 1 - i if reverse else i) * SUBLANES, SUBLANES)
                h = b_ref[pl.ds(r0, SUBLANES), :] + a_ref[pl.ds(r0, SUBLANES), :] * c
                rows = pl.ds(pl.multiple_of(t0 + r0, SUBLANES), SUBLANES)
                if reverse:
                    y_ref[rows, :] += h
                    return h[0:1, :]
                y_ref[rows, :] = h
                return h[SUBLANES - 1:SUBLANES, :]

            carry = lax.fori_loop(0, n_groups, group, carry)
            if reverse:
                o_ref[pl.ds(t0, tb), :] = (y_ref[pl.ds(t0, tb), :] * gate_ref[pl.ds(t0, tb), :]).astype(o_ref.dtype)
            return carry

        fin_ref[d:d + 1, :] = lax.fori_loop(0, n_blocks, block, h0_ref[d:d + 1, :])


def rg_core(xin, gate, conv_w, conv_b, w_gates, bias, lam, h0, stream, *, tc=256):
    n, c = xin.shape
    seq, batch = stream.seq, stream.batch
    tb = min(SCAN_BLOCK, seq)
    nb = tc // LANES
    return pl.pallas_call(
        functools.partial(_rg_core_kernel, seq=seq),
        out_shape=(jax.ShapeDtypeStruct((n, c), BF16), jax.ShapeDtypeStruct((batch, 2, c), F32)),
        grid=(batch, c // tc),
        in_specs=[pl.BlockSpec((seq, tc), lambda b, j: (b, j)),
                  pl.BlockSpec((seq, tc), lambda b, j: (b, j)),
                  pl.BlockSpec((4, tc), lambda b, j: (0, j)),
                  pl.BlockSpec((1, tc), lambda b, j: (0, j)),
                  pl.BlockSpec((2, nb, LANES, 2 * LANES), lambda b, j: (0, j, 0, 0)),
                  pl.BlockSpec((2, 2, tc), lambda b, j: (0, 0, j)),
                  pl.BlockSpec((2, tc), lambda b, j: (0, j)),
                  pl.BlockSpec((None, 2, tc), lambda b, j: (b, 0, j))],
        out_specs=(pl.BlockSpec((seq, tc), lambda b, j: (b, j)),
                   pl.BlockSpec((None, 2, tc), lambda b, j: (b, 0, j))),
        scratch_shapes=[pltpu.VMEM((seq, tc), F32), pltpu.VMEM((seq, tc), F32),
                        pltpu.VMEM((tb, tc), F32), pltpu.VMEM((tb, tc), F32)],
        compiler_params=_cparams("parallel", "parallel"),
        name="rg_core",
    )(xin, gate, conv_w, conv_b.reshape(1, c), w_gates, bias, lam, h0)


def _gelu_epilogue(acc):
    return jax.nn.gelu(acc)


def rg_mixer(h, x, mods, layer, stream, state, w_in, w_gate, conv_w, conv_b, w_gates, bias, lam, w_out):
    xin = matmul(h, w_in, name="rg_in")
    gate = matmul(h, w_gate, epilogue=_gelu_epilogue, name="rg_gate")
    yg, fin = rg_core(xin, gate, conv_w, conv_b, w_gates, bias, lam, state, stream)
    return out_proj_residual(yg, w_out, x, mods, layer, stream, name="rg_out"), fin


GRID_W = 64
ROPE_BASE = 10000.0
ATTN_TQ = 512
ATTN_TK = 512


def rope_tables(seq, dk):
    half = dk // 2
    quarter = half // 2
    pos = jnp.arange(seq)
    inv = ROPE_BASE ** (-jnp.arange(0, half, 2, dtype=F32) / half)
    ang_r = (pos // GRID_W).astype(F32)[:, None] * inv
    ang_c = (pos % GRID_W).astype(F32)[:, None] * inv
    cos = jnp.concatenate([jnp.cos(ang_r)] * 2 + [jnp.cos(ang_c)] * 2, axis=1)
    sin = jnp.concatenate([-jnp.sin(ang_r), jnp.sin(ang_r), -jnp.sin(ang_c), jnp.sin(ang_c)], axis=1)
    assert cos.shape == (seq, 4 * quarter)
    return cos, sin


def _rope_epilogue(acc, cos_ref, sin_ref):
    cos = cos_ref[...]
    sin = sin_ref[...]
    dk = cos.shape[1]
    lane = lax.broadcasted_iota(jnp.int32, cos.shape, 1)
    first = (lane % (dk // 2)) < dk // 4
    outs = []
    for g in range(acc.shape[1] // dk):
        x = acc[:, g * dk:(g + 1) * dk]
        rot = jnp.where(first, pltpu.roll(x, dk - dk // 4, axis=1), pltpu.roll(x, dk // 4, axis=1))
        outs.append(x * cos + rot * sin)
    return jnp.concatenate(outs, axis=1)


def _attn_kernel(*refs, n_tiles, tk, has_cache, lam_init, scale):
    if has_cache:
        q_ref, k_ref, v_ref, ck_ref, cv_ref, lam_ref, g_ref, o_ref = refs[:8]
    else:
        q_ref, k_ref, v_ref, lam_ref, g_ref, o_ref = refs[:6]
    s_ref = refs[-2]
    sc_ref = refs[-1]
    q = q_ref[...]
    tq = q.shape[0]
    dk = q.shape[1] // 2
    dv = v_ref.shape[1]
    maps = range(2)

    def scores(kt):
        return [lax.dot_general(q[:, c * dk:(c + 1) * dk], kt[:, c * dk:(c + 1) * dk],
                                (((1,), (1,)), ((), ())), preferred_element_type=F32) for c in maps]

    m = [jnp.full((tq, 1), -jnp.inf, F32) for _ in maps]
    if has_cache:
        s = scores(ck_ref[...])
        for c in maps:
            sc_ref[c] = s[c]
            m[c] = jnp.maximum(m[c], jnp.max(s[c], axis=-1, keepdims=True))

    def max_pass(i, m):
        s = scores(k_ref[pl.ds(pl.multiple_of(i * tk, tk), tk), :])
        for c in maps:
            s_ref[c, i] = s[c]
        return tuple(jnp.maximum(m[c], jnp.max(s[c], axis=-1, keepdims=True)) for c in maps)

    m = lax.fori_loop(0, n_tiles, max_pass, tuple(m), unroll=2 if n_tiles % 2 == 0 else 1)

    c2 = scale * math.log2(math.e)

    def accumulate(s, vt, st):
        new = []
        for c in maps:
            l, acc = st[c]
            p = jnp.exp2((s[c] - m[c]) * c2)
            new.append((l + jnp.sum(p, axis=-1, keepdims=True),
                        acc + jnp.dot(p.astype(BF16), vt, preferred_element_type=F32)))
        return tuple(new)

    st = tuple((jnp.zeros((tq, 1), F32), jnp.zeros((tq, dv), F32)) for _ in maps)
    if has_cache:
        st = accumulate([sc_ref[c] for c in maps], cv_ref[...], st)

    def sum_pass(i, st):
        return accumulate([s_ref[c, i] for c in maps], v_ref[pl.ds(pl.multiple_of(i * tk, tk), tk), :], st)

    st = lax.fori_loop(0, n_tiles, sum_pass, st)
    lp = lam_ref[...]
    lam = (jnp.exp(jnp.sum(lp[0:1, :] * lp[1:2, :], axis=-1, keepdims=True))
           - jnp.exp(jnp.sum(lp[2:3, :] * lp[3:4, :], axis=-1, keepdims=True)) + lam_init)
    diff = st[0][1] / st[0][0] - lam * (st[1][1] / st[1][0])
    diff = diff * lax.rsqrt(jnp.mean(diff * diff, axis=-1, keepdims=True) + NORM_EPS)
    o_ref[...] = (diff * g_ref[...] * (1.0 - lam_init)).astype(o_ref.dtype)


def diff_attention(q, k, v, *rest, stream, lam_init):
    has_cache = len(rest) == 4
    ck, cv = (rest[0], rest[1]) if has_cache else (None, None)
    da_lam, subln_g = rest[-2], rest[-1]
    n, width = q.shape
    dv = subln_g.shape[0]
    heads = width // dv
    dk = dv // 2
    batch, seq = stream.batch, stream.seq
    tq = min(ATTN_TQ, seq)
    tk = min(ATTN_TK, seq)
    nq = seq // tq
    in_specs = [pl.BlockSpec((tq, dv), lambda b, h, i: (b * nq + i, h)),
                pl.BlockSpec((seq, dv), lambda b, h, i: (b, h)),
                pl.BlockSpec((seq, dv), lambda b, h, i: (b, h))]
    args = [q, k, v]
    if has_cache:
        past = ck.shape[0] // batch
        in_specs += [pl.BlockSpec((past, dv), lambda b, h, i: (b, h))] * 2
        args += [ck, cv]
    in_specs += [pl.BlockSpec(da_lam.shape, lambda b, h, i: (0, 0)), pl.BlockSpec((1, dv), lambda b, h, i: (0, 0))]
    args += [da_lam.astype(F32), subln_g.reshape(1, dv).astype(F32)]
    return pl.pallas_call(
        functools.partial(_attn_kernel, n_tiles=seq // tk, tk=tk, has_cache=has_cache, lam_init=lam_init,
                          scale=1.0 / math.sqrt(dk)),
        out_shape=jax.ShapeDtypeStruct((n, width), BF16),
        grid=(batch, heads, nq),
        in_specs=in_specs,
        out_specs=pl.BlockSpec((tq, dv), lambda b, h, i: (b * nq + i, h)),
        scratch_shapes=[pltpu.VMEM((2, seq // tk, tq, tk), F32),
                        pltpu.VMEM((2, tq, past if has_cache else SUBLANES), F32)],
        compiler_params=_cparams("parallel", "parallel", "parallel"),
        name="diff_attention",
    )(*args)


def _attn_t_kernel(*refs, n_tiles, has_cache, lam_init, scale):
    if has_cache:
        qt_ref, k_ref, vt_ref, ck_ref, cvt_ref, lam_ref, g_ref, o_ref, s_ref, sc_ref = refs
    else:
        qt_ref, k_ref, vt_ref, lam_ref, g_ref, o_ref, s_ref = refs
    tq = qt_ref.shape[1]
    dk = qt_ref.shape[0] // 2
    dv = vt_ref.shape[1]
    tk = vt_ref.shape[2]
    maps = range(2)
    qt = [qt_ref[c * dk:(c + 1) * dk, :] for c in maps]

    def scores(kt):
        return [jnp.dot(kt[:, c * dk:(c + 1) * dk], qt[c], preferred_element_type=F32) for c in maps]

    m = [jnp.full((1, tq), -jnp.inf, F32) for _ in maps]
    if has_cache:
        s = scores(ck_ref[...])
        for c in maps:
            sc_ref[c] = s[c]
            m[c] = jnp.maximum(m[c], jnp.max(s[c], axis=0, keepdims=True))

    def max_pass(i, m):
        s = scores(k_ref[pl.ds(pl.multiple_of(i * tk, tk), tk), :])
        for c in maps:
            s_ref[c, i] = s[c]
        return tuple(jnp.maximum(m[c], jnp.max(s[c], axis=0, keepdims=True)) for c in maps)

    m = lax.fori_loop(0, n_tiles, max_pass, tuple(m), unroll=True)

    c2 = scale * math.log2(math.e)

    def accumulate(s, vt, st):
        new = []
        for c in maps:
            l, acc = st[c]
            p = jnp.exp2((s[c] - m[c]) * c2)
            new.append((l + jnp.sum(p, axis=0, keepdims=True),
                        acc + jnp.dot(vt, p.astype(BF16), preferred_element_type=F32)))
        return tuple(new)

    st = tuple((jnp.zeros((1, tq), F32), jnp.zeros((dv, tq), F32)) for _ in maps)
    if has_cache:
        st = accumulate([sc_ref[c] for c in maps], cvt_ref[...], st)

    def sum_pass(i, st):
        return accumulate([s_ref[c, i] for c in maps], vt_ref[i], st)

    st = lax.fori_loop(0, n_tiles, sum_pass, st, unroll=True)
    lp = lam_ref[...]
    lam = (jnp.exp(jnp.sum(lp[0:1, :] * lp[1:2, :], axis=-1, keepdims=True))
           - jnp.exp(jnp.sum(lp[2:3, :] * lp[3:4, :], axis=-1, keepdims=True)) + lam_init)
    diff = st[0][1] / st[0][0] - lam * (st[1][1] / st[1][0])
    diff = diff * lax.rsqrt(jnp.mean(diff * diff, axis=0, keepdims=True) + NORM_EPS)
    o_ref[...] = (diff.T * g_ref[...] * (1.0 - lam_init)).astype(o_ref.dtype)


def diff_attention_t(q_t, k, v_t, *rest, stream, lam_init):
    has_cache = len(rest) == 4
    da_lam, subln_g = rest[-2], rest[-1]
    n = k.shape[0]
    batch, heads, n_tiles, dv, tk = v_t.shape
    seq = stream.seq
    tq = min(ATTN_TQ, seq)
    nq = seq // tq
    in_specs = [pl.BlockSpec((dv, tq), lambda b, h, i: (h, b * nq + i)),
                pl.BlockSpec((seq, dv), lambda b, h, i: (b, h)),
                pl.BlockSpec((None, None, n_tiles, dv, tk), lambda b, h, i: (b, h, 0, 0, 0))]
    args = [q_t, k, v_t]
    scratch = [pltpu.VMEM((2, n_tiles, tk, tq), F32)]
    if has_cache:
        ck, cv_t = rest[0], rest[1]
        past = cv_t.shape[3]
        in_specs += [pl.BlockSpec((past, dv), lambda b, h, i: (b, h)),
                     pl.BlockSpec((None, None, dv, past), lambda b, h, i: (b, h, 0, 0))]
        args += [ck, cv_t]
        scratch += [pltpu.VMEM((2, past, tq), F32)]
    in_specs += [pl.BlockSpec(da_lam.shape, lambda b, h, i: (0, 0)), pl.BlockSpec((1, dv), lambda b, h, i: (0, 0))]
    args += [da_lam.astype(F32), subln_g.reshape(1, dv).astype(F32)]
    return pl.pallas_call(
        functools.partial(_attn_t_kernel, n_tiles=n_tiles, has_cache=has_cache, lam_init=lam_init,
                          scale=1.0 / math.sqrt(dv // 2)),
        out_shape=jax.ShapeDtypeStruct((n, heads * dv), BF16),
        grid=(batch, heads, nq),
        in_specs=in_specs,
        out_specs=pl.BlockSpec((tq, dv), lambda b, h, i: (b * nq + i, h)),
        scratch_shapes=scratch,
        compiler_params=_cparams("parallel", "parallel", "parallel"),
        name="diff_attention",
    )(*args)


def _key_tiles_transposed(v, stream, heads):
    tk = min(ATTN_TK, stream.seq)
    dv = v.shape[1] // heads
    return v.reshape(stream.batch, stream.seq // tk, tk, heads, dv).transpose(0, 3, 1, 4, 2)


def _dup_epilogue(acc):
    return acc, acc


def dattn_mixer(h, x, mods, layer, stream, cache, wq, wk, wv, wo, da_lam, subln_g, lam_init):
    dk = subln_g.shape[0] // 2
    heads = wv.shape[1] // subln_g.shape[0]
    if cache is None:
        q = matmul(h, wq, out_dtype=BF16, name="da_q")
        k32, k = matmul(h, wk, out_dtype=(F32, BF16), epilogue=_dup_epilogue, name="da_k")
        v32, v = matmul(h, wv, out_dtype=(F32, BF16), epilogue=_dup_epilogue, name="da_v")
        o = diff_attention_t(q.T, k, _key_tiles_transposed(v, stream, heads), da_lam, subln_g,
                             stream=stream, lam_init=lam_init)
        new = (k32, v32)
    else:
        cos, sin = rope_tables(stream.seq, dk)
        tm = min(MM_TM, stream.seq)
        per = stream.seq // tm
        rope = [(cos, (tm, dk), lambda i, j: (i % per, 0)), (sin, (tm, dk), lambda i, j: (i % per, 0))]
        q = matmul(h, wq, tm=tm, out_dtype=BF16, epilogue=_rope_epilogue, extras=rope, name="da_q_rope")
        k = matmul(h, wk, tm=tm, out_dtype=BF16, epilogue=_rope_epilogue, extras=rope, name="da_k_rope")
        v = matmul(h, wv, out_dtype=BF16, name="da_v")
        o = diff_attention_t(q.T, k, _key_tiles_transposed(v, stream, heads), cache[0], cache[1], da_lam, subln_g,
                             stream=stream, lam_init=lam_init)
        new = None
    return out_proj_residual(o, wo, x, mods, layer, stream, name="da_out"), new


ML_CHUNK = 256


def _ml_conv_kernel(x_ref, cw_ref, cb_ref, o32_ref, o16_ref, *, seq):
    tb = min(SCAN_BLOCK, seq)
    cw = cw_ref[...]
    cb = cb_ref[...]

    def body(i, carry):
        t0 = pl.multiple_of(i * tb, tb)
        y = _conv_rows(x_ref, t0, tb, seq, cw, cb)
        y = y * jax.nn.sigmoid(y)
        o32_ref[pl.ds(t0, tb), :] = y
        o16_ref[pl.ds(t0, tb), :] = y.astype(BF16)
        return carry

    lax.fori_loop(0, seq // tb, body, 0)


def ml_conv(xi, conv_w, conv_b, stream, *, tc=256):
    n, w = xi.shape
    seq = stream.seq
    return pl.pallas_call(
        functools.partial(_ml_conv_kernel, seq=seq),
        out_shape=(jax.ShapeDtypeStruct((n, w), F32), jax.ShapeDtypeStruct((n, w), BF16)),
        grid=(stream.batch, w // tc),
        in_specs=[pl.BlockSpec((seq, tc), lambda b, j: (b, j)),
                  pl.BlockSpec((conv_w.shape[0], tc), lambda b, j: (0, j)),
                  pl.BlockSpec((1, tc), lambda b, j: (0, j))],
        out_specs=(pl.BlockSpec((seq, tc), lambda b, j: (b, j)), pl.BlockSpec((seq, tc), lambda b, j: (b, j))),
        compiler_params=_cparams("parallel", "parallel"),
        name="ml_conv",
    )(xi, conv_w.astype(F32), conv_b.reshape(1, w).astype(F32))


def _ml_gates_kernel(q_ref, k_ref, v_ref, w_ref, b_ref, o_ref, *, n_heads):
    acc = jnp.dot(q_ref[...], w_ref[0], preferred_element_type=F32)
    acc = acc + jnp.dot(k_ref[...], w_ref[1], preferred_element_type=F32)
    acc = acc + jnp.dot(v_ref[...], w_ref[2], preferred_element_type=F32) + b_ref[...]
    col = lax.broadcasted_iota(jnp.int32, acc.shape, 1)
    is_forget = (col // n_heads) % 2 == 1
    o_ref[...] = jnp.where(is_forget, -_softplus(-acc), acc)


def ml_gates(q, k, v, w_gate, b_gate, n_heads, *, tm=512):
    n, w = q.shape
    ng = w_gate.shape[2]
    row = pl.BlockSpec((tm, w), lambda i: (i, 0))
    return pl.pallas_call(
        functools.partial(_ml_gates_kernel, n_heads=n_heads),
        out_shape=jax.ShapeDtypeStruct((n, ng), F32),
        grid=(n // tm,),
        in_specs=[row, row, row, pl.BlockSpec((3, w, ng), lambda i: (0, 0, 0)), pl.BlockSpec((1, ng), lambda i: (0, 0))],
        out_specs=pl.BlockSpec((tm, ng), lambda i: (i, 0)),
        compiler_params=_cparams("parallel"),
        name="ml_gates",
    )(q, k, v, w_gate, b_gate.reshape(1, ng).astype(F32))


ML_HEADS_PER_STEP = 2


def _mlstm_cell_kernel(*refs, n_heads, has_init, want_final):
    q_ref, k_ref, kt_ref, v_ref, g_ref, gt_ref = refs[:6]
    init_refs = refs[6:9] if has_init else None
    h_ref = refs[9 if has_init else 6]
    ct_s, n_s, m_s = refs[-3:]
    final_refs = refs[-6:-3] if want_final else None
    hb = ct_s.shape[0]
    dh = ct_s.shape[1]
    h0 = pl.program_id(1) * hb
    d = pl.program_id(2)
    ci = pl.program_id(3)
    reverse = d == 1
    t = q_ref.shape[0]

    @pl.when(ci == 0)
    def _():
        for j in range(hb):
            if has_init:
                ct_s[j] = init_refs[0][j].T
                n_s[j] = init_refs[1][j]
                m_s[j] = init_refs[2][j]
            else:
                ct_s[j] = jnp.zeros((dh, dh), F32)
                n_s[j] = jnp.zeros((1, dh), F32)
                m_s[j] = jnp.zeros((1, 1), F32)

    g = g_ref[...]
    gt = gt_ref[...]
    lane = lax.broadcasted_iota(jnp.int32, g.shape, 1)
    sub = lax.broadcasted_iota(jnp.int32, gt.shape, 0)
    r_idx = lax.broadcasted_iota(jnp.int32, (t, t), 0)
    s_idx = lax.broadcasted_iota(jnp.int32, (t, t), 1)
    ahead = (s_idx - r_idx) * jnp.where(reverse, -1, 1)
    mask = ahead <= 0
    mask_t = ahead >= 0
    row = lax.broadcasted_iota(jnp.int32, (t, 1), 0)
    last = row == jnp.where(reverse, 0, t - 1)

    for j in range(hb):
        cols = slice(j * dh, (j + 1) * dh)
        q = q_ref[:, cols]
        k = k_ref[:, cols]
        v = v_ref[:, cols]
        kt = kt_ref[cols, :]
        col_i = d * (2 * n_heads) + h0 + j
        col_f = col_i + n_heads
        ig_col = jnp.sum(jnp.where(lane == col_i, g, 0.0), axis=1, keepdims=True)
        lf_col = jnp.sum(jnp.where(lane == col_f, g, 0.0), axis=1, keepdims=True)
        ig_row = jnp.sum(jnp.where(sub == col_i, gt, 0.0), axis=0, keepdims=True)
        lf_row = jnp.sum(jnp.where(sub == col_f, gt, 0.0), axis=0, keepdims=True)
        b_col = jnp.sum(jnp.where(mask, lf_row, 0.0), axis=1, keepdims=True)
        b_row = jnp.sum(jnp.where(mask_t, lf_col, 0.0), axis=0, keepdims=True)
        m_prev = m_s[j]
        dlog = jnp.where(mask, b_col - b_row + ig_row, -jnp.inf)
        inter = b_col + m_prev
        m_t = jnp.maximum(inter, jnp.max(dlog, axis=1, keepdims=True))
        w_intra = jnp.exp(dlog - m_t)
        w_inter = jnp.exp(inter - m_t)
        s_mat = lax.dot_general(q, k, (((1,), (1,)), ((), ())), preferred_element_type=F32) * w_intra
        ct_old = ct_s[j]
        n_old = n_s[j]
        qc = jnp.dot(q, ct_old.astype(BF16), preferred_element_type=F32)
        num = jnp.dot(s_mat.astype(BF16), v, preferred_element_type=F32) + w_inter * qc
        n_rows = jnp.broadcast_to(n_old.astype(BF16), (SUBLANES, dh))
        qn = lax.dot_general(q, n_rows, (((1,), (1,)), ((), ())), preferred_element_type=F32)[:, 0:1]
        den = jnp.sum(s_mat, axis=1, keepdims=True) + w_inter * qn
        h_ref[:, cols] = num / jnp.maximum(jnp.abs(den), jnp.exp(-m_t))
        m_new = jnp.sum(jnp.where(last, m_t, 0.0), axis=0, keepdims=True)
        b_last = jnp.sum(jnp.where(last, b_col, 0.0), axis=0, keepdims=True)
        w_old = jnp.exp(b_last + m_prev - m_new)
        w_in = jnp.exp(b_last - b_col + ig_col - m_new)
        vw = (v.astype(F32) * w_in).astype(BF16)
        ct_s[j] = w_old * ct_old + jnp.dot(kt, vw, preferred_element_type=F32)
        n_s[j] = w_old * n_old + jnp.sum(w_in * k.astype(F32), axis=0, keepdims=True)
        m_s[j] = m_new

    if want_final:
        @pl.when(ci == pl.num_programs(3) - 1)
        def _():
            for j in range(hb):
                final_refs[0][j] = ct_s[j].T
                final_refs[1][j] = n_s[j]
                final_refs[2][j] = m_s[j]


def mlstm_cell(q, k, k_t, v, g, g_t, init, stream, heads, want_final):
    n, w = q.shape
    batch, seq = stream.batch, stream.seq
    dh = w // heads
    hb = ML_HEADS_PER_STEP
    t = min(ML_CHUNK, seq)
    nc = seq // t

    def blk(b, d, ci):
        return b * nc + ci + d * (nc - 1 - 2 * ci)

    tok = pl.BlockSpec((t, hb * dh), lambda b, h, d, ci: (blk(b, d, ci), h))
    st5 = lambda r, c: pl.BlockSpec((None, None, hb, r, c), lambda b, h, d, ci: (b, d, h, 0, 0))
    state_dims = ((dh, dh), (1, dh), (1, 1))
    out_shape = [jax.ShapeDtypeStruct((2, n, w), F32)]
    out_specs = [pl.BlockSpec((None, t, hb * dh), lambda b, h, d, ci: (d, blk(b, d, ci), h))]
    if want_final:
        out_shape += [jax.ShapeDtypeStruct((batch, 2, heads) + rc, F32) for rc in state_dims]
        out_specs += [st5(*rc) for rc in state_dims]
    res = pl.pallas_call(
        functools.partial(_mlstm_cell_kernel, n_heads=heads, has_init=init is not None, want_final=want_final),
        out_shape=tuple(out_shape),
        grid=(batch, heads // hb, 2, nc),
        in_specs=[tok, tok,
                  pl.BlockSpec((hb * dh, t), lambda b, h, d, ci: (h, blk(b, d, ci))),
                  tok,
                  pl.BlockSpec((t, g.shape[1]), lambda b, h, d, ci: (blk(b, d, ci), 0)),
                  pl.BlockSpec((g.shape[1], t), lambda b, h, d, ci: (0, blk(b, d, ci)))]
        + ([st5(*rc) for rc in state_dims] if init is not None else []),
        out_specs=tuple(out_specs),
        scratch_shapes=[pltpu.VMEM((hb,) + rc, F32) for rc in state_dims],
        compiler_params=_cparams("parallel", "parallel", "arbitrary", "arbitrary"),
        name="mlstm_cell",
    )(q, k, k_t, v, g, g_t, *(init or ()))
    return res[0], (tuple(res[1:]) if want_final else None)


def _scale_epilogue(acc, *, scale):
    return acc * scale


def _ml_out_gate_epilogue(acc, b_ref, h_ref, xc_ref, gn_ref, skip_ref):
    hc = jax.nn.sigmoid(acc + b_ref[...]) * (h_ref[0] + h_ref[1])
    hn = hc * lax.rsqrt(jnp.mean(hc * hc, axis=-1, keepdims=True) + NORM_EPS)
    return hn * gn_ref[...] + skip_ref[...] * xc_ref[...]


def mlstm_mixer(h, x, mods, layer, stream, init, want_final, w_up, conv_w, conv_b, wq, wk, wv, w_gate, b_gate,
                w_o, b_o, gn, skip, w_down):
    dh = wq.shape[1]
    xi = matmul(h, w_up, name="ml_up")
    xc, xc16 = ml_conv(xi, conv_w, conv_b, stream)
    q = matmul(xc16, wq, tm=MM_TM_DIAG, tn=dh, out_dtype=BF16, k_block=dh, name="ml_q")
    k = matmul(xc16, wk, tm=MM_TM_DIAG, tn=dh, out_dtype=BF16, k_block=dh, name="ml_k",
               epilogue=functools.partial(_scale_epilogue, scale=1.0 / math.sqrt(dh)))
    v = matmul(xi, wv, tm=MM_TM_DIAG, tn=dh, out_dtype=BF16, k_block=dh, name="ml_v")
    w = xi.shape[1]
    g = ml_gates(q, k, v, w_gate, b_gate, w // dh)
    hdir, final = mlstm_cell(q, k, k.T, v, g, g.T, init, stream, w // dh, want_final)
    tm = min(MM_TM, h.shape[0])
    vec = lambda a: (a.reshape(1, w).astype(F32), (1, dh), lambda i, j: (0, j))
    y = matmul(h, w_o, tm=tm, tn=dh, out_dtype=BF16, name="ml_o", epilogue=_ml_out_gate_epilogue,
               extras=[vec(b_o), (hdir, (2, tm, dh), lambda i, j: (0, i, j)), (xc, (tm, dh), lambda i, j: (i, j)),
                       vec(gn), vec(skip)])
    return out_proj_residual(y, w_down, x, mods, layer, stream, name="ml_down"), final


def kernel(x_prompt, x_sample, state_s5, state_rglru, cache_dattn_k, cache_dattn_v, state_mlstm_C, state_mlstm_n, state_mlstm_m, c, c_ctx, ada_w, ada_b, norm_g, ffn1_w1, ffn1_w3, ffn1_w2, ffn2_w1, ffn2_w3, ffn2_w2, final_norm_g, s5_w_in, s5_a_re, s5_a_im, s5_log_dt, s5_b_re, s5_b_im, s5_c_re, s5_c_im, s5_d, s5_w_glu, s5_b_glu, s5_w_out, rg_w_in, rg_w_gate, rg_conv_w, rg_conv_b, rg_wa, rg_ba, rg_wx, rg_bx, rg_lam, rg_w_out, da_wq, da_wk, da_wv, da_wo, da_lam, da_subln_g, ml_w_up, ml_conv_w, ml_conv_b, ml_wq, ml_wk, ml_wv, ml_w_gate, ml_b_gate, ml_w_o, ml_b_o, ml_gn, ml_skip, ml_w_down):
    bc, lc, d = x_prompt.shape
    bl, ll, _ = x_sample.shape
    depth = ada_w.shape[0]
    cs = Stream(bc, lc, 0, False)
    ls = Stream(bl, ll, 1, True)
    ctx = x_prompt.reshape(cs.n, d)
    lat = x_sample.reshape(ls.n, d)
    c_all = jnp.concatenate([c_ctx[None, :], c, jnp.zeros((MOD_ROWS - 1 - bl, d), F32)], axis=0)
    mods = adaln(c_all, ada_w, ada_b)
    bf = lambda a: a.astype(BF16)
    ffn1 = (bf(ffn1_w1), bf(ffn1_w3), bf(ffn1_w2))
    ffn2 = (bf(ffn2_w1), bf(ffn2_w3), bf(ffn2_w2))

    new_s5 = new_rg = None
    for i in range(depth):
        ctx = ffn_half(ctx, norm_g[i, 0], mods, i, 0, *ffn1, cs)
        lat = ffn_half(lat, norm_g[i, 0], mods, i, 0, *ffn1, ls)
        hc = modnorm(ctx, norm_g[i, 1], mods, i, 3, cs)
        hl = modnorm(lat, norm_g[i, 1], mods, i, 3, ls)
        kind = i % 4
        if kind == 0:
            ops = _s5_operators(s5_a_re, s5_a_im, s5_log_dt, s5_b_re, s5_b_im, s5_c_re, s5_c_im)
            ng, p = s5_a_re.shape[1], s5_a_re.shape[2]
            h0 = jnp.concatenate([state_s5[:, :, 0], state_s5[:, :, 1]], axis=-1).transpose(1, 2, 0, 3)
            args = (bf(s5_w_in), ops, s5_d, bf(s5_w_glu), s5_b_glu, bf(s5_w_out))
            ctx, fin = s5_mixer(hc, ctx, mods, i, cs, jnp.zeros((2, ng, bc, 2 * p), F32), *args)
            lat, _ = s5_mixer(hl, lat, mods, i, ls, h0.astype(F32), *args)
            fin = fin.transpose(2, 0, 1, 3)
            new_s5 = jnp.stack([fin[..., :p], fin[..., p:]], axis=2)
        elif kind == 1:
            w_gates = bf(jnp.concatenate([rg_wa, rg_wx], axis=-1))
            bias = jnp.stack([rg_ba, rg_bx], axis=1).astype(F32)
            args = (bf(rg_w_in), bf(rg_w_gate), rg_conv_w, rg_conv_b, w_gates, bias, rg_lam, bf(rg_w_out))
            ctx, new_rg = rg_mixer(hc, ctx, mods, i, cs, jnp.zeros((bc, 2, rg_lam.shape[1]), F32), *args)
            lat, _ = rg_mixer(hl, lat, mods, i, ls, state_rglru, *args)
        elif kind == 2:
            lam_init = 0.8 - 0.6 * math.exp(-0.3 * i)
            heads, dv = cache_dattn_v.shape[2], cache_dattn_v.shape[3]
            args = (bf(da_wq), bf(da_wk), bf(da_wv), bf(da_wo), da_lam, da_subln_g, lam_init)
            ctx, (k32, v32) = dattn_mixer(hc, ctx, mods, i, cs, None, *args)
            new_k = k32.reshape(bc, lc, heads, 2, dv // 2)
            new_v = v32.reshape(bc, lc, heads, dv)
            cache = (bf(cache_dattn_k).reshape(-1, heads * dv), bf(cache_dattn_v).transpose(0, 2, 3, 1))
            lat, _ = dattn_mixer(hl, lat, mods, i, ls, cache, *args)
        else:
            heads, dh = ml_wq.shape[0], ml_wq.shape[1]
            args = (bf(ml_w_up), ml_conv_w, ml_conv_b, bf(ml_wq).reshape(heads * dh, dh),
                    bf(ml_wk).reshape(heads * dh, dh), bf(ml_wv).reshape(heads * dh, dh), bf(ml_w_gate), ml_b_gate,
                    bf(ml_w_o), ml_b_o, ml_gn, ml_skip, bf(ml_w_down))
            ctx, (cf, nf, mf) = mlstm_mixer(hc, ctx, mods, i, cs, None, True, *args)
            init = (state_mlstm_C.astype(F32), state_mlstm_n.astype(F32).reshape(bl, 2, heads, 1, dh),
                    state_mlstm_m.astype(F32).reshape(bl, 2, heads, 1, 1))
            lat, _ = mlstm_mixer(hl, lat, mods, i, ls, init, False, *args)
            new_c, new_n, new_m = cf, nf.reshape(bc, 2, heads, dh), mf.reshape(bc, 2, heads)
        ctx = ffn_half(ctx, norm_g[i, 2], mods, i, 6, *ffn2, cs)
        lat = ffn_half(lat, norm_g[i, 2], mods, i, 6, *ffn2, ls)
    y_prompt = rmsnorm(ctx, final_norm_g).reshape(bc, lc, d)
    y_sample = rmsnorm(lat, final_norm_g).reshape(bl, ll, d)
    return (y_prompt, y_sample, new_s5, new_rg, new_k, new_v, new_c, new_n, new_m)
```

```python
import functools
import math

import jax
import jax.numpy as jnp
from jax import lax
from jax.experimental import pallas as pl
from jax.experimental.pallas import tpu as pltpu

F32 = jnp.float32
BF16 = jnp.bfloat16

NORM_EPS = 1e-6
VMEM_LIMIT_BYTES = 56 * 1024 * 1024
MOD_ROWS = 8


def _cparams(*sem):
    return pltpu.CompilerParams(dimension_semantics=sem, vmem_limit_bytes=VMEM_LIMIT_BYTES)


def _bdot(a, b):
    return jnp.dot(a.astype(BF16), b.astype(BF16), preferred_element_type=F32)


def _mod_row(ref, row):
    return ref[pl.ds(row, 1), :]


class Stream:
    def __init__(self, batch, seq, row0, per_batch_mod):
        self.batch, self.seq, self.row0 = batch, seq, row0
        self.n = batch * seq
        self.rows_per_mod = seq if per_batch_mod else self.n

    def mod_row(self, first_token):
        return self.row0 + first_token // self.rows_per_mod


def _adaln_kernel(c_ref, w_ref, b_ref, o_ref):
    c = c_ref[...]
    s = c * jax.nn.sigmoid(c)
    o_ref[...] = _bdot(s, w_ref[...]) + b_ref[...]


def adaln(c_all, ada_w, ada_b, *, tn=1024):
    nl, d, m = ada_w.shape
    return pl.pallas_call(
        _adaln_kernel,
        out_shape=jax.ShapeDtypeStruct((nl, MOD_ROWS, m), F32),
        grid=(nl, m // tn),
        in_specs=[pl.BlockSpec((MOD_ROWS, d), lambda l, j: (0, 0)),
                  pl.BlockSpec((None, d, tn), lambda l, j: (l, 0, j)),
                  pl.BlockSpec((None, 1, tn), lambda l, j: (l, 0, j))],
        out_specs=pl.BlockSpec((None, MOD_ROWS, tn), lambda l, j: (l, 0, j)),
        compiler_params=_cparams("parallel", "parallel"),
        name="adaln",
    )(c_all, ada_w, ada_b.reshape(nl, 1, m))


def _mod_spec(d, layer, k):
    return pl.BlockSpec((None, MOD_ROWS, d), lambda *_: (layer, 0, k))


def _rms(x, g):
    return x * lax.rsqrt(jnp.mean(x * x, axis=-1, keepdims=True) + NORM_EPS) * g


def _modnorm_kernel(x_ref, g_ref, sh_ref, sc_ref, o_ref, *, stream, tm):
    row = stream.mod_row(pl.program_id(0) * tm)
    y = _rms(x_ref[...], g_ref[...])
    o_ref[...] = (y * (1.0 + _mod_row(sc_ref, row)) + _mod_row(sh_ref, row)).astype(o_ref.dtype)


def modnorm(x, g, mods, layer, k_shift, stream, *, tm=512):
    n, d = x.shape
    return pl.pallas_call(
        functools.partial(_modnorm_kernel, stream=stream, tm=tm),
        out_shape=jax.ShapeDtypeStruct((n, d), BF16),
        grid=(n // tm,),
        in_specs=[pl.BlockSpec((tm, d), lambda i: (i, 0)),
                  pl.BlockSpec((1, d), lambda i: (0, 0)),
                  _mod_spec(d, layer, k_shift),
                  _mod_spec(d, layer, k_shift + 1)],
        out_specs=pl.BlockSpec((tm, d), lambda i: (i, 0)),
        compiler_params=_cparams("parallel"),
        name="modnorm",
    )(x, g.reshape(1, d), mods, mods)


def _rmsnorm_kernel(x_ref, g_ref, o_ref):
    o_ref[...] = _rms(x_ref[...], g_ref[...])


def rmsnorm(x, g, *, tm=512):
    n, d = x.shape
    return pl.pallas_call(
        _rmsnorm_kernel,
        out_shape=jax.ShapeDtypeStruct((n, d), F32),
        grid=(n // tm,),
        in_specs=[pl.BlockSpec((tm, d), lambda i: (i, 0)), pl.BlockSpec((1, d), lambda i: (0, 0))],
        out_specs=pl.BlockSpec((tm, d), lambda i: (i, 0)),
        compiler_params=_cparams("parallel"),
        name="rmsnorm",
    )(x, g.reshape(1, d))


def _ffn_kernel(x_ref, g_ref, sh_ref, sc_ref, gt_ref, w1_ref, w3_ref, w2_ref, o_ref, h_ref, acc_ref,
                *, stream, tm):
    f = pl.program_id(1)
    row = stream.mod_row(pl.program_id(0) * tm)

    @pl.when(f == 0)
    def _():
        y = _rms(x_ref[...], g_ref[...])
        h_ref[...] = (y * (1.0 + _mod_row(sc_ref, row)) + _mod_row(sh_ref, row)).astype(BF16)
        acc_ref[...] = jnp.zeros_like(acc_ref)

    h = h_ref[...]
    a = jnp.dot(h, w1_ref[...], preferred_element_type=F32)
    b = jnp.dot(h, w3_ref[...], preferred_element_type=F32)
    u = (a * jax.nn.sigmoid(a)) * b
    acc_ref[...] += jnp.dot(u.astype(BF16), w2_ref[...], preferred_element_type=F32)

    @pl.when(f == pl.num_programs(1) - 1)
    def _():
        o_ref[...] = x_ref[...] + (0.5 * _mod_row(gt_ref, row)) * acc_ref[...]


def ffn_half(x, g, mods, layer, k_shift, w1, w3, w2, stream, *, tm=512, tf=512):
    n, d = x.shape
    dff = w1.shape[2]
    return pl.pallas_call(
        functools.partial(_ffn_kernel, stream=stream, tm=tm),
        out_shape=jax.ShapeDtypeStruct((n, d), F32),
        grid=(n // tm, dff // tf),
        in_specs=[pl.BlockSpec((tm, d), lambda i, f: (i, 0)),
                  pl.BlockSpec((1, d), lambda i, f: (0, 0)),
                  _mod_spec(d, layer, k_shift),
                  _mod_spec(d, layer, k_shift + 1),
                  _mod_spec(d, layer, k_shift + 2),
                  pl.BlockSpec((None, d, tf), lambda i, f: (layer, 0, f)),
                  pl.BlockSpec((None, d, tf), lambda i, f: (layer, 0, f)),
                  pl.BlockSpec((None, tf, d), lambda i, f: (layer, f, 0))],
        out_specs=pl.BlockSpec((tm, d), lambda i, f: (i, 0)),
        scratch_shapes=[pltpu.VMEM((tm, d), BF16), pltpu.VMEM((tm, d), F32)],
        compiler_params=_cparams("parallel", "arbitrary"),
        name="ffn_half",
    )(x, g.reshape(1, d), mods, mods, mods, w1, w3, w2)


def _mm_kernel(x_ref, w_ref, *rest, epilogue, n_extra):
    extra, o_refs = rest[:n_extra], rest[n_extra:]
    acc = _bdot(x_ref[...], w_ref[...])
    outs = epilogue(acc, *extra) if epilogue is not None else acc
    if not isinstance(outs, tuple):
        outs = (outs,)
    for o_ref, o in zip(o_refs, outs):
        o_ref[...] = o.astype(o_ref.dtype)


MM_TM = 1024
MM_TN = 512
MM_TM_DIAG = 2048


def matmul(x, w, *, tm=MM_TM, tn=MM_TN, out_dtype=F32, epilogue=None, extras=(), k_block=None, slab_out=False,
           name="matmul"):
    n, kx = x.shape
    tm = min(tm, n)
    if k_block is None:
        kb, m = kx, w.shape[1]
        x_map = lambda i, j: (i, 0)
        w_map = lambda i, j: (0, j)
    else:
        kb = k_block
        per = w.shape[1] // tn
        m = (kx // kb) * w.shape[1]
        x_map = lambda i, j: (i, j // per)
        w_map = lambda i, j: (j // per, j % per)
    dts = out_dtype if isinstance(out_dtype, tuple) else (out_dtype,)
    if slab_out:
        out_shape = tuple(jax.ShapeDtypeStruct((m // 128, n, 128), dt) for dt in dts)
        out_specs = tuple(pl.BlockSpec((tn // 128, tm, 128), lambda i, j: (j, i, 0)) for _ in dts)
    else:
        out_shape = tuple(jax.ShapeDtypeStruct((n, m), dt) for dt in dts)
        out_specs = tuple(pl.BlockSpec((tm, tn), lambda i, j: (i, j)) for _ in dts)
    res = pl.pallas_call(
        functools.partial(_mm_kernel, epilogue=epilogue, n_extra=len(extras)),
        out_shape=out_shape,
        grid=(n // tm, m // tn),
        in_specs=[pl.BlockSpec((tm, kb), x_map), pl.BlockSpec((kb, tn), w_map)]
        + [pl.BlockSpec(bs, im) for _, bs, im in extras],
        out_specs=out_specs,
        compiler_params=_cparams("parallel", "parallel"),
        name=name,
    )(x, w, *[a for a, _, _ in extras])
    return res if isinstance(out_dtype, tuple) else res[0]


def _mm_t_kernel(x_ref, w_ref, *rest, epilogue, n_extra):
    extra, o_ref = rest[:n_extra], rest[n_extra]
    acc = lax.dot_general(w_ref[...], x_ref[...].astype(BF16), (((1,), (1,)), ((), ())), preferred_element_type=F32)
    o_ref[...] = (epilogue(acc, *extra) if epilogue is not None else acc).astype(o_ref.dtype)


def matmul_t(x, w_t, *, tm=MM_TM, tn=MM_TN, out_dtype=BF16, epilogue=None, extras=(), k_block=None, name="matmul_t"):
    n, kx = x.shape
    m = w_t.shape[0]
    tm = min(tm, n)
    if k_block is None:
        kb = kx
        x_map = lambda i, j: (i, 0)
    else:
        kb = k_block
        per = (m // (kx // kb)) // tn
        x_map = lambda i, j: (i, j // per)
    return pl.pallas_call(
        functools.partial(_mm_t_kernel, epilogue=epilogue, n_extra=len(extras)),
        out_shape=jax.ShapeDtypeStruct((m, n), out_dtype),
        grid=(n // tm, m // tn),
        in_specs=[pl.BlockSpec((tm, kb), x_map), pl.BlockSpec((tn, kb), lambda i, j: (j, 0))]
        + [pl.BlockSpec(bs, im) for _, bs, im in extras],
        out_specs=pl.BlockSpec((tn, tm), lambda i, j: (j, i)),
        compiler_params=_cparams("parallel", "parallel"),
        name=name,
    )(x, w_t, *[a for a, _, _ in extras])


S5_CHUNK = 16
S5_GROUPS_PER_STEP = 8


def _s5_operators(a_re, a_im, log_dt, b_re, b_im, c_re, c_im):
    hp = lax.Precision.HIGHEST
    t = S5_CHUNK
    a_re, a_im, b_re, b_im, c_re, c_im = (z.astype(F32) for z in (a_re, a_im, b_re, b_im, c_re, c_im))
    dt = jnp.exp(log_dt.astype(F32))[..., None]
    mag = jnp.exp(a_re * dt)
    lr = mag * jnp.cos(a_im * dt)
    li = mag * jnp.sin(a_im * dt)
    den = a_re * a_re + a_im * a_im
    cr = ((lr - 1.0) * a_re + li * a_im) / den
    ci = (li * a_re - (lr - 1.0) * a_im) / den
    bbr = cr[..., None] * b_re - ci[..., None] * b_im
    bbi = cr[..., None] * b_im + ci[..., None] * b_re
    k = jnp.arange(t + 1, dtype=F32)[None, None, :, None]
    pmag = jnp.exp(k * (a_re * dt)[:, :, None, :])
    pr = pmag * jnp.cos(k * (a_im * dt)[:, :, None, :])
    pi = pmag * jnp.sin(k * (a_im * dt)[:, :, None, :])
    pbr = pr[..., None] * bbr[:, :, None] - pi[..., None] * bbi[:, :, None]
    pbi = pr[..., None] * bbi[:, :, None] + pi[..., None] * bbr[:, :, None]
    m = (jnp.einsum('dgop,dgkpi->dgkoi', c_re, pbr, precision=hp)
         - jnp.einsum('dgop,dgkpi->dgkoi', c_im, pbi, precision=hp))
    m = m.at[:, :, t].set(0.0)
    s_idx = jnp.arange(t)[:, None]
    t_idx = jnp.arange(t)[None, :]
    lag_f = jnp.where(t_idx >= s_idx, t_idx - s_idx, t)
    lag_b = jnp.where(s_idx >= t_idx, s_idx - t_idx, t)

    def toeplitz(md, lag):
        x = md[:, lag]
        return x.transpose(0, 1, 4, 2, 3).reshape(md.shape[0], t * x.shape[-1], t * x.shape[-2])

    toep = toeplitz(m[0], lag_f) + toeplitz(m[1], lag_b)
    pow_s = jnp.stack([t - 1 - jnp.arange(t), jnp.arange(t)])

    def smat(d):
        r = pbr[d][:, pow_s[d]]
        i = pbi[d][:, pow_s[d]]
        x = jnp.concatenate([r, i], axis=2)
        return x.transpose(0, 1, 3, 2).reshape(x.shape[0], t * x.shape[3], x.shape[2])

    s_op = jnp.stack([smat(0), smat(1)])
    pow_w = jnp.stack([jnp.arange(t) + 1, t - jnp.arange(t)])

    def wmat(d):
        zr = (c_re[d][:, None] * pr[d][:, pow_w[d]][:, :, None, :]
              - c_im[d][:, None] * pi[d][:, pow_w[d]][:, :, None, :])
        zi = (c_re[d][:, None] * pi[d][:, pow_w[d]][:, :, None, :]
              + c_im[d][:, None] * pr[d][:, pow_w[d]][:, :, None, :])
        x = jnp.concatenate([zr, -zi], axis=3)
        return x.transpose(0, 3, 1, 2).reshape(x.shape[0], x.shape[3], t * x.shape[2])

    w_op = jnp.stack([wmat(0), wmat(1)])
    mul_a = jnp.concatenate([pr[:, :, t], pr[:, :, t]], axis=-1)
    mul_b = jnp.concatenate([-pi[:, :, t], pi[:, :, t]], axis=-1)
    return toep.astype(BF16), s_op.astype(BF16), w_op.astype(BF16), mul_a, mul_b


def _s5_intra_kernel(u_ref, toep_ref, s_ref, y_ref, st_ref):
    for g in range(u_ref.shape[0]):
        u = u_ref[g]
        y_ref[g] = jnp.dot(u, toep_ref[g], preferred_element_type=F32)
        for d in range(2):
            st_ref[d, g] = jnp.dot(u, s_ref[d, g], preferred_element_type=F32)


def _s5_recur_kernel(st_ref, a_ref, b_ref, h0_ref, hin_ref, fin_ref, *, n_chunks):
    rev = pl.program_id(0) == 1
    half = a_ref.shape[-1] // 2
    a = a_ref[...]
    b = b_ref[...]
    b_sw = pltpu.roll(b, half, axis=2)

    def body(i, carry):
        h, h_sw = carry
        j = jnp.where(rev, n_chunks - 1 - i, i)
        hin_ref[j] = h
        s = st_ref[j]
        return h * a + h_sw * b + s, h_sw * a + h * b_sw + pltpu.roll(s, half, axis=2)

    h0 = h0_ref[...]
    fin_ref[...] = lax.fori_loop(0, n_chunks, body, (h0, pltpu.roll(h0, half, axis=2)))[0]


def _s5_inter_kernel(y_ref, hin_ref, w_ref, o_ref):
    for g in range(y_ref.shape[0]):
        acc = y_ref[g]
        for d in range(2):
            acc = acc + jnp.dot(hin_ref[d, g].astype(BF16), w_ref[d, g], preferred_element_type=F32)
        o_ref[g] = acc


def s5_scan(u, h0, ops, batch, seq):
    toep, s_op, w_op, mul_a, mul_b = ops
    ng, tc, _ = toep.shape
    t = S5_CHUNK
    c = tc // t
    p2 = s_op.shape[-1]
    nj = seq // t
    rows = nj * batch
    gs = S5_GROUPS_PER_STEP
    ug = u.astype(BF16).reshape(batch, nj, t, ng, c).transpose(3, 1, 0, 2, 4).reshape(ng, rows, tc)
    y_intra, st = pl.pallas_call(
        _s5_intra_kernel,
        out_shape=(jax.ShapeDtypeStruct((ng, rows, tc), F32), jax.ShapeDtypeStruct((2, ng, rows, p2), F32)),
        grid=(ng // gs,),
        in_specs=[pl.BlockSpec((gs, rows, tc), lambda g: (g, 0, 0)),
                  pl.BlockSpec((gs, tc, tc), lambda g: (g, 0, 0)),
                  pl.BlockSpec((2, gs, tc, p2), lambda g: (0, g, 0, 0))],
        out_specs=(pl.BlockSpec((gs, rows, tc), lambda g: (g, 0, 0)),
                   pl.BlockSpec((2, gs, rows, p2), lambda g: (0, g, 0, 0))),
        compiler_params=_cparams("parallel"),
        name="s5_intra",
    )(ug, toep, s_op)
    st5 = st.reshape(2, ng, nj, batch, p2).transpose(0, 2, 1, 3, 4)
    hin, fin = pl.pallas_call(
        functools.partial(_s5_recur_kernel, n_chunks=nj),
        out_shape=(jax.ShapeDtypeStruct((2, nj, ng, batch, p2), F32),
                   jax.ShapeDtypeStruct((2, ng, batch, p2), F32)),
        grid=(2, ng // gs),
        in_specs=[pl.BlockSpec((None, nj, gs, batch, p2), lambda d, g: (d, 0, g, 0, 0)),
                  pl.BlockSpec((None, gs, 1, p2), lambda d, g: (d, g, 0, 0)),
                  pl.BlockSpec((None, gs, 1, p2), lambda d, g: (d, g, 0, 0)),
                  pl.BlockSpec((None, gs, batch, p2), lambda d, g: (d, g, 0, 0))],
        out_specs=(pl.BlockSpec((None, nj, gs, batch, p2), lambda d, g: (d, 0, g, 0, 0)),
                   pl.BlockSpec((None, gs, batch, p2), lambda d, g: (d, g, 0, 0))),
        compiler_params=_cparams("parallel", "parallel"),
        name="s5_recur",
    )(st5, mul_a.reshape(2, ng, 1, p2), mul_b.reshape(2, ng, 1, p2), h0)
    hin = hin.transpose(0, 2, 1, 3, 4).reshape(2, ng, rows, p2)
    yg = pl.pallas_call(
        _s5_inter_kernel,
        out_shape=jax.ShapeDtypeStruct((ng, rows, tc), F32),
        grid=(ng // gs,),
        in_specs=[pl.BlockSpec((gs, rows, tc), lambda g: (g, 0, 0)),
                  pl.BlockSpec((2, gs, rows, p2), lambda g: (0, g, 0, 0)),
                  pl.BlockSpec((2, gs, p2, tc), lambda g: (0, g, 0, 0))],
        out_specs=pl.BlockSpec((gs, rows, tc), lambda g: (g, 0, 0)),
        compiler_params=_cparams("parallel"),
        name="s5_inter",
    )(y_intra, hin, w_op)
    y = yg.reshape(ng, nj, batch, t, c).transpose(2, 1, 3, 0, 4).reshape(batch * seq, ng * c)
    return y, fin


def _s5_post_kernel(u_ref, y_ref, d_ref, o_ref):
    y = d_ref[...] * u_ref[...] + y_ref[...]
    o_ref[...] = jax.nn.gelu(y)


def s5_post(u, y, d_skip, *, tm=512):
    n, w = u.shape
    return pl.pallas_call(
        _s5_post_kernel,
        out_shape=jax.ShapeDtypeStruct((n, w), F32),
        grid=(n // tm,),
        in_specs=[pl.BlockSpec((tm, w), lambda i: (i, 0)), pl.BlockSpec((tm, w), lambda i: (i, 0)),
                  pl.BlockSpec((1, w), lambda i: (0, 0))],
        out_specs=pl.BlockSpec((tm, w), lambda i: (i, 0)),
        compiler_params=_cparams("parallel"),
        name="s5_post",
    )(u, y, d_skip.reshape(1, w).astype(F32))


def _residual_epilogue(acc, x_ref, gt_ref, *, stream, tm):
    row = stream.mod_row(pl.program_id(0) * tm)
    return x_ref[...] + _mod_row(gt_ref, row) * acc


def out_proj_residual(z, w, x, mods, layer, stream, *, name="out_proj"):
    d = x.shape[1]
    tm, tn = min(MM_TM, x.shape[0]), MM_TN
    return matmul(z, w, tm=tm, tn=tn, name=name,
                  epilogue=functools.partial(_residual_epilogue, stream=stream, tm=tm),
                  extras=[(x, (tm, tn), lambda i, j: (i, j)),
                          (mods, (None, MOD_ROWS, tn), lambda i, j: (layer, 0, 5 * (d // tn) + j))])


def _glu_epilogue(acc, z_ref, b_ref):
    return z_ref[...] * jax.nn.sigmoid(acc + b_ref[...])


def s5_mixer(h, x, mods, layer, stream, state, w_in, ops, d_skip, w_glu, b_glu, w_out):
    u = matmul(h, w_in, name="s5_in")
    y, fin = s5_scan(u, state, ops, stream.batch, stream.seq)
    z = s5_post(u, y, d_skip)
    w = z.shape[1]
    zz = matmul(z, w_glu, out_dtype=BF16, name="s5_glu", epilogue=_glu_epilogue,
                extras=[(z, (min(MM_TM, z.shape[0]), MM_TN), lambda i, j: (i, j)),
                        (b_glu.reshape(1, w).astype(F32), (1, MM_TN), lambda i, j: (0, j))])
    return out_proj_residual(zz, w_out, x, mods, layer, stream, name="s5_out"), fin


SUBLANES = 8
LANES = 128
SCAN_BLOCK = 256


def _conv_rows(x_ref, t0, tb, seq, cw, cb):
    x = x_ref[pl.ds(t0, tb), :]
    prev = x_ref[pl.ds(pl.multiple_of(jnp.maximum(t0 - SUBLANES, 0), SUBLANES), SUBLANES), :]
    nxt = x_ref[pl.ds(pl.multiple_of(jnp.minimum(t0 + tb, seq - SUBLANES), SUBLANES), SUBLANES), :]
    has_prev = t0 > 0
    has_next = t0 + tb < seq
    p1 = jnp.where(has_prev, prev[SUBLANES - 1:SUBLANES, :], 0.0)
    n0 = jnp.where(has_next, nxt[0:1, :], 0.0)
    n1 = jnp.where(has_next, nxt[1:2, :], 0.0)
    rows = lax.broadcasted_iota(jnp.int32, x.shape, 0)
    xm1 = jnp.where(rows == 0, p1, pltpu.roll(x, 1, axis=0))
    xp1 = jnp.where(rows == tb - 1, n0, pltpu.roll(x, tb - 1, axis=0))
    xp2 = jnp.where(rows == tb - 1, n1, jnp.where(rows == tb - 2, n0, pltpu.roll(x, tb - 2, axis=0)))
    return cb + xm1 * cw[0:1, :] + x * cw[1:2, :] + xp1 * cw[2:3, :] + xp2 * cw[3:4, :]


def _group_scan(a, b, reverse):
    tb, tc = a.shape
    a = a.reshape(tb // SUBLANES, SUBLANES, tc)
    b = b.reshape(tb // SUBLANES, SUBLANES, tc)
    pos = lax.broadcasted_iota(jnp.int32, a.shape, 1)
    s = 1
    while s < SUBLANES:
        ok = pos < SUBLANES - s if reverse else pos >= s
        shift = SUBLANES - s if reverse else s
        a_s, b_s = pltpu.roll(a, shift, axis=1), pltpu.roll(b, shift, axis=1)
        b = b + jnp.where(ok, a, 0.0) * b_s
        a = jnp.where(ok, a * a_s, a)
        s *= 2
    return a.reshape(tb, tc), b.reshape(tb, tc)


def _softplus(x):
    return jnp.maximum(x, 0.0) + jnp.log1p(jnp.exp(-jnp.abs(x)))


SLAB_CHUNK = 8


def _s5_slab_operators(a_re, a_im, log_dt, b_re, b_im, c_re, c_im):
    hp = lax.Precision.HIGHEST
    t = SLAB_CHUNK
    a_re, a_im, b_re, b_im, c_re, c_im = (z.astype(F32) for z in (a_re, a_im, b_re, b_im, c_re, c_im))
    ng, p = a_re.shape[1], a_re.shape[2]
    c = b_re.shape[3]
    gs = LANES // c
    ns = ng // gs
    dt = jnp.exp(log_dt.astype(F32))[..., None]
    mag = jnp.exp(a_re * dt)
    lr = mag * jnp.cos(a_im * dt)
    li = mag * jnp.sin(a_im * dt)
    den = a_re * a_re + a_im * a_im
    cr = ((lr - 1.0) * a_re + li * a_im) / den
    ci = (li * a_re - (lr - 1.0) * a_im) / den
    bbr = cr[..., None] * b_re - ci[..., None] * b_im
    bbi = cr[..., None] * b_im + ci[..., None] * b_re

    def powers(k):
        k = k.astype(F32)[None, None, :, None]
        pm = jnp.exp(k * (a_re * dt)[:, :, None, :])
        return pm * jnp.cos(k * (a_im * dt)[:, :, None, :]), pm * jnp.sin(k * (a_im * dt)[:, :, None, :])

    pr, pi = powers(jnp.arange(t + 1))
    pbr = (pr[:, :, :, :, None] * bbr[:, :, None] - pi[:, :, :, :, None] * bbi[:, :, None])
    pbi = (pr[:, :, :, :, None] * bbi[:, :, None] + pi[:, :, :, :, None] * bbr[:, :, None])
    flat = lambda z: z.transpose(0, 1, 3, 2, 4).reshape(2, ng, p, (t + 1) * c)
    m = (jnp.einsum('dgop,dgpx->dgox', c_re, flat(pbr), precision=hp)
         - jnp.einsum('dgop,dgpx->dgox', c_im, flat(pbi), precision=hp)).reshape(2, ng, c, t + 1, c)
    eye = jnp.eye(gs, dtype=F32)
    s_idx = jnp.arange(t)[:, None]
    t_idx = jnp.arange(t)[None, :]
    lag = jnp.stack([jnp.clip(t_idx - s_idx, 0, t), jnp.clip(s_idx - t_idx, 0, t)])
    live = jnp.stack([t_idx >= s_idx, s_idx >= t_idx]).astype(F32)
    msel = sum((m[d][:, :, lag[d], :] * live[d][None, None, :, :, None]).transpose(0, 2, 3, 4, 1) for d in range(2))
    msel = msel.reshape(ns, gs, t, t, c, c)
    toep = jnp.einsum('agstio,gh->asgitho', msel, eye).reshape(ns, t * LANES, t * LANES)
    pow_s = jnp.stack([t - 1 - jnp.arange(t), jnp.arange(t)])
    pb = jnp.stack([pbr, pbi], axis=2)
    s_sel = jnp.stack([pb[d][:, :, pow_s[d]] for d in range(2)])
    s_sel = s_sel.transpose(0, 1, 3, 5, 2, 4).reshape(2, ns, gs, t, c, 2, p)
    s_op = jnp.einsum('dagsirp,gh->dasgirhp', s_sel, eye).reshape(2, ns, t * LANES, 2 * gs * p)
    pow_w = jnp.stack([jnp.arange(t) + 1, t - jnp.arange(t)])
    zr = jnp.stack([c_re[d][:, None] * pr[d][:, pow_w[d]][:, :, None, :]
                    - c_im[d][:, None] * pi[d][:, pow_w[d]][:, :, None, :] for d in range(2)])
    zi = jnp.stack([c_re[d][:, None] * pi[d][:, pow_w[d]][:, :, None, :]
                    + c_im[d][:, None] * pr[d][:, pow_w[d]][:, :, None, :] for d in range(2)])
    w_sel = jnp.stack([zr, -zi], axis=2)
    w_sel = w_sel.transpose(0, 1, 2, 5, 3, 4).reshape(2, ns, gs, 2, p, t, c)
    w_op = jnp.einsum('dagrpto,gh->dargptho', w_sel, eye).reshape(2, ns, 2 * gs * p, t * LANES)
    qr, qi = powers(t * (jnp.arange(SUBLANES) + 1))
    lanes = lambda z: z.reshape(2, ns, gs, SUBLANES, p).transpose(0, 1, 3, 2, 4).reshape(2, ns, SUBLANES, gs * p)
    pow_a = jnp.concatenate([lanes(qr), lanes(qr)], axis=-1)
    pow_b = jnp.concatenate([-lanes(qi), lanes(qi)], axis=-1)
    return toep.astype(BF16), s_op.astype(BF16), w_op.astype(BF16), pow_a, pow_b


def _s5_state_kernel(u_ref, s_ref, o_ref):
    o_ref[...] = jnp.dot(u_ref[...], s_ref[...], preferred_element_type=F32)


def _s5_recur_slab_kernel(st_ref, pa_ref, pb_ref, h0_ref, hin_ref, fin_ref, loc_ref):
    nj, w = loc_ref.shape
    half = w // 2
    ng = nj // SUBLANES
    swap = lambda z: jnp.concatenate([z[..., half:], z[..., :half]], axis=-1)
    for d in range(2):
        reverse = d == 1
        pa = pa_ref[d]
        pb = pb_ref[d]
        z = st_ref[d].reshape(ng, SUBLANES, w)
        pos = lax.broadcasted_iota(jnp.int32, z.shape, 1)
        s = 1
        while s < SUBLANES:
            ok = pos < SUBLANES - s if reverse else pos >= s
            zs = pltpu.roll(z, SUBLANES - s if reverse else s, axis=1)
            z = z + jnp.where(ok, zs * pa[s - 1:s, :] + swap(zs) * pb[s - 1:s, :], 0.0)
            s *= 2
        loc_ref[...] = z.reshape(nj, w)
        row = lax.broadcasted_iota(jnp.int32, (SUBLANES, w), 0)
        if reverse:
            ca = jnp.concatenate([pa[SUBLANES - 1 - r:SUBLANES - r, :] for r in range(SUBLANES)], axis=0)
            cb = jnp.concatenate([pb[SUBLANES - 1 - r:SUBLANES - r, :] for r in range(SUBLANES)], axis=0)
        else:
            ca, cb = pa, pb

        def group(i, carry, d=d, reverse=reverse, ca=ca, cb=cb):
            c, c_sw = carry
            r0 = pl.multiple_of((ng - 1 - i if reverse else i) * SUBLANES, SUBLANES)
            h = loc_ref[pl.ds(r0, SUBLANES), :] + c * ca + c_sw * cb
            shifted = pltpu.roll(h, SUBLANES - 1 if reverse else 1, axis=0)
            hin_ref[d, pl.ds(r0, SUBLANES), :] = jnp.where(row == (SUBLANES - 1 if reverse else 0), c, shifted)
            last = h[0:1, :] if reverse else h[SUBLANES - 1:SUBLANES, :]
            return last, swap(last)

        h0 = h0_ref[d]
        fin_ref[d] = lax.fori_loop(0, ng, group, (h0, swap(h0)))[0]


def _s5_out_kernel(u_ref, toep_ref, hin_ref, w_ref, o_ref):
    acc = jnp.dot(u_ref[...], toep_ref[...], preferred_element_type=F32)
    for d in range(2):
        acc = acc + jnp.dot(hin_ref[d].astype(BF16), w_ref[d], preferred_element_type=F32)
    o_ref[...] = acc


def s5_slab_scan(u16, h0, ops, batch, seq):
    toep, s_op, w_op, pow_a, pow_b = ops
    ns, n, _ = u16.shape
    t = SLAB_CHUNK
    kc = t * LANES
    w = s_op.shape[-1]
    nj = seq // t
    rows = n // t
    tr = min(rows, 1024)
    uc = u16.reshape(ns, rows, kc)
    st = pl.pallas_call(
        _s5_state_kernel,
        out_shape=jax.ShapeDtypeStruct((2, ns, rows, w), F32),
        grid=(ns, 2, rows // tr),
        in_specs=[pl.BlockSpec((None, tr, kc), lambda a, d, r: (a, r, 0)),
                  pl.BlockSpec((None, None, kc, w), lambda a, d, r: (d, a, 0, 0))],
        out_specs=pl.BlockSpec((None, None, tr, w), lambda a, d, r: (d, a, r, 0)),
        compiler_params=_cparams("parallel", "parallel", "parallel"),
        name="s5_state",
    )(uc, s_op)
    hin, fin = pl.pallas_call(
        _s5_recur_slab_kernel,
        out_shape=(jax.ShapeDtypeStruct((2, ns, rows, w), F32), jax.ShapeDtypeStruct((2, ns, batch, 1, w), F32)),
        grid=(ns, batch),
        in_specs=[pl.BlockSpec((2, None, nj, w), lambda a, b: (0, a, b, 0)),
                  pl.BlockSpec((2, None, SUBLANES, w), lambda a, b: (0, a, 0, 0)),
                  pl.BlockSpec((2, None, SUBLANES, w), lambda a, b: (0, a, 0, 0)),
                  pl.BlockSpec((2, None, None, 1, w), lambda a, b: (0, a, b, 0, 0))],
        out_specs=(pl.BlockSpec((2, None, nj, w), lambda a, b: (0, a, b, 0)),
                   pl.BlockSpec((2, None, None, 1, w), lambda a, b: (0, a, b, 0, 0))),
        scratch_shapes=[pltpu.VMEM((nj, w), F32)],
        compiler_params=_cparams("parallel", "parallel"),
        name="s5_recur",
    )(st, pow_a, pow_b, h0)
    y = pl.pallas_call(
        _s5_out_kernel,
        out_shape=jax.ShapeDtypeStruct((ns, rows, kc), F32),
        grid=(ns, rows // tr),
        in_specs=[pl.BlockSpec((None, tr, kc), lambda a, r: (a, r, 0)),
                  pl.BlockSpec((None, kc, kc), lambda a, r: (a, 0, 0)),
                  pl.BlockSpec((2, None, tr, w), lambda a, r: (0, a, r, 0)),
                  pl.BlockSpec((2, None, w, kc), lambda a, r: (0, a, 0, 0))],
        out_specs=pl.BlockSpec((None, tr, kc), lambda a, r: (a, r, 0)),
        compiler_params=_cparams("parallel", "parallel"),
        name="s5_out_chunks",
    )(uc, toep, hin, w_op)
    return y.reshape(ns, n, LANES), fin


def _slab_epilogue(acc):
    parts = jnp.stack([acc[:, s * LANES:(s + 1) * LANES] for s in range(acc.shape[1] // LANES)])
    return parts, parts


def _s5_post_slab_kernel(u_ref, y_ref, d_ref, o_ref):
    for s in range(u_ref.shape[0]):
        cols = slice(s * LANES, (s + 1) * LANES)
        o_ref[:, cols] = jax.nn.gelu(d_ref[:, cols] * u_ref[s] + y_ref[s])


def s5_post_slab(u, y, d_skip, *, tm=512):
    ns, n, _ = u.shape
    w = ns * LANES
    slab = pl.BlockSpec((ns, tm, LANES), lambda i: (0, i, 0))
    return pl.pallas_call(
        _s5_post_slab_kernel,
        out_shape=jax.ShapeDtypeStruct((n, w), F32),
        grid=(n // tm,),
        in_specs=[slab, slab, pl.BlockSpec((1, w), lambda i: (0, 0))],
        out_specs=pl.BlockSpec((tm, w), lambda i: (i, 0)),
        compiler_params=_cparams("parallel"),
        name="s5_post",
    )(u, y, d_skip.reshape(1, w).astype(F32))


def s5_slab_mixer(h, x, mods, layer, stream, state, w_in, ops, d_skip, w_glu, b_glu, w_out):
    u, u16 = matmul(h, w_in, out_dtype=(F32, BF16), epilogue=_slab_epilogue, slab_out=True, name="s5_in")
    y, fin = s5_slab_scan(u16, state, ops, stream.batch, stream.seq)
    z = s5_post_slab(u, y, d_skip)
    w = z.shape[1]
    zz = matmul(z, w_glu, out_dtype=BF16, name="s5_glu", epilogue=_glu_epilogue,
                extras=[(z, (min(MM_TM, z.shape[0]), MM_TN), lambda i, j: (i, j)),
                        (b_glu.reshape(1, w).astype(F32), (1, MM_TN), lambda i, j: (0, j))])
    return out_proj_residual(zz, w_out, x, mods, layer, stream, name="s5_out"), fin


def s5_pack_state(state, slab_groups):
    b, _, _, g, p = state.shape
    s = state.astype(F32).reshape(b, 2, 2, g // slab_groups, slab_groups * p)
    return s.transpose(1, 3, 0, 2, 4).reshape(2, g // slab_groups, b, 1, 2 * slab_groups * p)


def s5_unpack_state(fin, slab_groups, p):
    _, ns, b, _, w = fin.shape
    s = fin.reshape(2, ns, b, 2, slab_groups * p).transpose(2, 0, 3, 1, 4)
    return s.reshape(b, 2, 2, ns * slab_groups, p)


RG_C = 8.0


def _rg_core_kernel(x_ref, gate_ref, cw_ref, cb_ref, w_ref, bias_ref, lam_ref, h0_ref, o_ref, fin_ref,
                    y_ref, xc_ref, a_ref, b_ref, *, seq):
    tb = a_ref.shape[0]
    tc = a_ref.shape[1]
    n_blocks = seq // tb
    n_groups = tb // SUBLANES
    cw = cw_ref[...]
    cb = cb_ref[...]
    for d in range(2):
        reverse = d == 1
        sp = _softplus(-lam_ref[d:d + 1, :])
        bias_a = bias_ref[d, 0:1, :]
        bias_x = bias_ref[d, 1:2, :]

        def block(k, carry, d=d, reverse=reverse, sp=sp, bias_a=bias_a, bias_x=bias_x):
            t0 = pl.multiple_of((n_blocks - 1 - k if reverse else k) * tb, tb)
            if reverse:
                xc = xc_ref[pl.ds(t0, tb), :]
            else:
                xc = _conv_rows(x_ref, t0, tb, seq, cw, cb)
                xc_ref[pl.ds(t0, tb), :] = xc
            pre = [_bdot(xc[:, n * LANES:(n + 1) * LANES], w_ref[d, n]) for n in range(tc // LANES)]
            r = jax.nn.sigmoid(jnp.concatenate([p[:, :LANES] for p in pre], axis=1) + bias_a)
            ig = jax.nn.sigmoid(jnp.concatenate([p[:, LANES:] for p in pre], axis=1) + bias_x)
            log_a = -RG_C * r * sp
            a = jnp.exp(log_a)
            bv = jnp.sqrt(-jnp.tanh(log_a) * (a * a + 1.0)) * (ig * xc)
            a_ref[...], b_ref[...] = _group_scan(a, bv, reverse)

            def group(i, c):
                r0 = pl.multiple_of((n_groups - 1 - i if reverse else i) * SUBLANES, SUBLANES)
                h = b_ref[pl.ds(r0, SUBLANES), :] + a_ref[pl.ds(r0, SUBLANES), :] * c
                rows = pl.ds(pl.multiple_of(t0 + r0, SUBLANES), SUBLANES)
                if reverse:
                    y_ref[rows, :] += h
                    return h[0:1, :]
                y_ref[rows, :] = h
                return h[SUBLANES - 1:SUBLANES, :]

            carry = lax.fori_loop(0, n_groups, group, carry)
            if reverse:
                o_ref[pl.ds(t0, tb), :] = (y_ref[pl.ds(t0, tb), :] * gate_ref[pl.ds(t0, tb), :]).astype(o_ref.dtype)
            return carry

        fin_ref[d:d + 1, :] = lax.fori_loop(0, n_blocks, block, h0_ref[d:d + 1, :])


def rg_core(xin, gate, conv_w, conv_b, w_gates, bias, lam, h0, stream, *, tc=256):
    n, c = xin.shape
    seq, batch = stream.seq, stream.batch
    tb = min(SCAN_BLOCK, seq)
    nb = tc // LANES
    return pl.pallas_call(
        functools.partial(_rg_core_kernel, seq=seq),
        out_shape=(jax.ShapeDtypeStruct((n, c), BF16), jax.ShapeDtypeStruct((batch, 2, c), F32)),
        grid=(batch, c // tc),
        in_specs=[pl.BlockSpec((seq, tc), lambda b, j: (b, j)),
                  pl.BlockSpec((seq, tc), lambda b, j: (b, j)),
                  pl.BlockSpec((4, tc), lambda b, j: (0, j)),
                  pl.BlockSpec((1, tc), lambda b, j: (0, j)),
                  pl.BlockSpec((2, nb, LANES, 2 * LANES), lambda b, j: (0, j, 0, 0)),
                  pl.BlockSpec((2, 2, tc), lambda b, j: (0, 0, j)),
                  pl.BlockSpec((2, tc), lambda b, j: (0, j)),
                  pl.BlockSpec((None, 2, tc), lambda b, j: (b, 0, j))],
        out_specs=(pl.BlockSpec((seq, tc), lambda b, j: (b, j)),
                   pl.BlockSpec((None, 2, tc), lambda b, j: (b, 0, j))),
        scratch_shapes=[pltpu.VMEM((seq, tc), F32), pltpu.VMEM((seq, tc), F32),
                        pltpu.VMEM((tb, tc), F32), pltpu.VMEM((tb, tc), F32)],
        compiler_params=_cparams("parallel", "parallel"),
        name="rg_core",
    )(xin, gate, conv_w, conv_b.reshape(1, c), w_gates, bias, lam, h0)


def _gelu_epilogue(acc):
    return jax.nn.gelu(acc)


def rg_mixer(h, x, mods, layer, stream, state, w_in, w_gate, conv_w, conv_b, w_gates, bias, lam, w_out):
    xin = matmul(h, w_in, name="rg_in")
    gate = matmul(h, w_gate, epilogue=_gelu_epilogue, name="rg_gate")
    yg, fin = rg_core(xin, gate, conv_w, conv_b, w_gates, bias, lam, state, stream)
    return out_proj_residual(yg, w_out, x, mods, layer, stream, name="rg_out"), fin


GRID_W = 64
ROPE_BASE = 10000.0
ATTN_TQ = 512
ATTN_TK = 512


def rope_tables(seq, dk):
    half = dk // 2
    quarter = half // 2
    pos = jnp.arange(seq)
    inv = ROPE_BASE ** (-jnp.arange(0, half, 2, dtype=F32) / half)
    ang_r = (pos // GRID_W).astype(F32)[:, None] * inv
    ang_c = (pos % GRID_W).astype(F32)[:, None] * inv
    cos = jnp.concatenate([jnp.cos(ang_r)] * 2 + [jnp.cos(ang_c)] * 2, axis=1)
    sin = jnp.concatenate([-jnp.sin(ang_r), jnp.sin(ang_r), -jnp.sin(ang_c), jnp.sin(ang_c)], axis=1)
    assert cos.shape == (seq, 4 * quarter)
    return cos, sin


def _rope_epilogue(acc, cos_ref, sin_ref):
    cos = cos_ref[...]
    sin = sin_ref[...]
    dk = cos.shape[1]
    lane = lax.broadcasted_iota(jnp.int32, cos.shape, 1)
    first = (lane % (dk // 2)) < dk // 4
    outs = []
    for g in range(acc.shape[1] // dk):
        x = acc[:, g * dk:(g + 1) * dk]
        rot = jnp.where(first, pltpu.roll(x, dk - dk // 4, axis=1), pltpu.roll(x, dk // 4, axis=1))
        outs.append(x * cos + rot * sin)
    return jnp.concatenate(outs, axis=1)


def _attn_kernel(*refs, n_tiles, tk, has_cache, lam_init, scale):
    if has_cache:
        q_ref, k_ref, v_ref, ck_ref, cv_ref, lam_ref, g_ref, o_ref = refs[:8]
    else:
        q_ref, k_ref, v_ref, lam_ref, g_ref, o_ref = refs[:6]
    s_ref = refs[-2]
    sc_ref = refs[-1]
    q = q_ref[...]
    tq = q.shape[0]
    dk = q.shape[1] // 2
    dv = v_ref.shape[1]
    maps = range(2)

    def scores(kt):
        return [lax.dot_general(q[:, c * dk:(c + 1) * dk], kt[:, c * dk:(c + 1) * dk],
                                (((1,), (1,)), ((), ())), preferred_element_type=F32) for c in maps]

    m = [jnp.full((tq, 1), -jnp.inf, F32) for _ in maps]
    if has_cache:
        s = scores(ck_ref[...])
        for c in maps:
            sc_ref[c] = s[c]
            m[c] = jnp.maximum(m[c], jnp.max(s[c], axis=-1, keepdims=True))

    def max_pass(i, m):
        s = scores(k_ref[pl.ds(pl.multiple_of(i * tk, tk), tk), :])
        for c in maps:
            s_ref[c, i] = s[c]
        return tuple(jnp.maximum(m[c], jnp.max(s[c], axis=-1, keepdims=True)) for c in maps)

    m = lax.fori_loop(0, n_tiles, max_pass, tuple(m), unroll=2 if n_tiles % 2 == 0 else 1)

    c2 = scale * math.log2(math.e)

    def accumulate(s, vt, st):
        new = []
        for c in maps:
            l, acc = st[c]
            p = jnp.exp2((s[c] - m[c]) * c2)
            new.append((l + jnp.sum(p, axis=-1, keepdims=True),
                        acc + jnp.dot(p.astype(BF16), vt, preferred_element_type=F32)))
        return tuple(new)

    st = tuple((jnp.zeros((tq, 1), F32), jnp.zeros((tq, dv), F32)) for _ in maps)
    if has_cache:
        st = accumulate([sc_ref[c] for c in maps], cv_ref[...], st)

    def sum_pass(i, st):
        return accumulate([s_ref[c, i] for c in maps], v_ref[pl.ds(pl.multiple_of(i * tk, tk), tk), :], st)

    st = lax.fori_loop(0, n_tiles, sum_pass, st)
    lp = lam_ref[...]
    lam = (jnp.exp(jnp.sum(lp[0:1, :] * lp[1:2, :], axis=-1, keepdims=True))
           - jnp.exp(jnp.sum(lp[2:3, :] * lp[3:4, :], axis=-1, keepdims=True)) + lam_init)
    diff = st[0][1] / st[0][0] - lam * (st[1][1] / st[1][0])
    diff = diff * lax.rsqrt(jnp.mean(diff * diff, axis=-1, keepdims=True) + NORM_EPS)
    o_ref[...] = (diff * g_ref[...] * (1.0 - lam_init)).astype(o_ref.dtype)


def diff_attention(q, k, v, *rest, stream, lam_init):
    has_cache = len(rest) == 4
    ck, cv = (rest[0], rest[1]) if has_cache else (None, None)
    da_lam, subln_g = rest[-2], rest[-1]
    n, width = q.shape
    dv = subln_g.shape[0]
    heads = width // dv
    dk = dv // 2
    batch, seq = stream.batch, stream.seq
    tq = min(ATTN_TQ, seq)
    tk = min(ATTN_TK, seq)
    nq = seq // tq
    in_specs = [pl.BlockSpec((tq, dv), lambda b, h, i: (b * nq + i, h)),
                pl.BlockSpec((seq, dv), lambda b, h, i: (b, h)),
                pl.BlockSpec((seq, dv), lambda b, h, i: (b, h))]
    args = [q, k, v]
    if has_cache:
        past = ck.shape[0] // batch
        in_specs += [pl.BlockSpec((past, dv), lambda b, h, i: (b, h))] * 2
        args += [ck, cv]
    in_specs += [pl.BlockSpec(da_lam.shape, lambda b, h, i: (0, 0)), pl.BlockSpec((1, dv), lambda b, h, i: (0, 0))]
    args += [da_lam.astype(F32), subln_g.reshape(1, dv).astype(F32)]
    return pl.pallas_call(
        functools.partial(_attn_kernel, n_tiles=seq // tk, tk=tk, has_cache=has_cache, lam_init=lam_init,
                          scale=1.0 / math.sqrt(dk)),
        out_shape=jax.ShapeDtypeStruct((n, width), BF16),
        grid=(batch, heads, nq),
        in_specs=in_specs,
        out_specs=pl.BlockSpec((tq, dv), lambda b, h, i: (b * nq + i, h)),
        scratch_shapes=[pltpu.VMEM((2, seq // tk, tq, tk), F32),
                        pltpu.VMEM((2, tq, past if has_cache else SUBLANES), F32)],
        compiler_params=_cparams("parallel", "parallel", "parallel"),
        name="diff_attention",
    )(*args)


def _attn_t_kernel(*refs, n_tiles, has_cache, lam_init, scale):
    if has_cache:
        qt_ref, k_ref, vt_ref, ck_ref, cvt_ref, lam_ref, g_ref, o_ref, s_ref, sc_ref = refs
    else:
        qt_ref, k_ref, vt_ref, lam_ref, g_ref, o_ref, s_ref = refs
    tq = qt_ref.shape[1]
    dk = qt_ref.shape[0] // 2
    dv = vt_ref.shape[0]
    tk = s_ref.shape[2]
    maps = range(2)
    qt = [qt_ref[c * dk:(c + 1) * dk, :] for c in maps]

    def scores(kt):
        return [jnp.dot(kt[:, c * dk:(c + 1) * dk], qt[c], preferred_element_type=F32) for c in maps]

    m = [jnp.full((1, tq), -jnp.inf, F32) for _ in maps]
    if has_cache:
        s = scores(ck_ref[...])
        for c in maps:
            sc_ref[c] = s[c]
            m[c] = jnp.maximum(m[c], jnp.max(s[c], axis=0, keepdims=True))

    for i in range(n_tiles):
        s = scores(k_ref[i * tk:(i + 1) * tk, :])
        for c in maps:
            s_ref[c, i] = s[c]
            m[c] = jnp.maximum(m[c], jnp.max(s[c], axis=0, keepdims=True))

    c2 = scale * math.log2(math.e)

    def accumulate(s, vt, st):
        new = []
        for c in maps:
            l, acc = st[c]
            p = jnp.exp2((s[c] - m[c]) * c2)
            new.append((l + jnp.sum(p, axis=0, keepdims=True),
                        acc + jnp.dot(vt, p.astype(BF16), preferred_element_type=F32)))
        return tuple(new)

    st = tuple((jnp.zeros((1, tq), F32), jnp.zeros((dv, tq), F32)) for _ in maps)
    if has_cache:
        st = accumulate([sc_ref[c] for c in maps], cvt_ref[...], st)

    for i in range(n_tiles):
        st = accumulate([s_ref[c, i] for c in maps], vt_ref[:, i * tk:(i + 1) * tk], st)
    lp = lam_ref[...]
    lam = (jnp.exp(jnp.sum(lp[0:1, :] * lp[1:2, :], axis=-1, keepdims=True))
           - jnp.exp(jnp.sum(lp[2:3, :] * lp[3:4, :], axis=-1, keepdims=True)) + lam_init)
    diff = st[0][1] / st[0][0] - lam * (st[1][1] / st[1][0])
    diff = diff * lax.rsqrt(jnp.mean(diff * diff, axis=0, keepdims=True) + NORM_EPS)
    o_ref[...] = (diff.T * g_ref[...] * (1.0 - lam_init)).astype(o_ref.dtype)


def diff_attention_t(q_t, k, v_t, *rest, stream, lam_init):
    has_cache = len(rest) == 4
    da_lam, subln_g = rest[-2], rest[-1]
    n = k.shape[0]
    dv = subln_g.shape[0]
    heads = v_t.shape[0] // dv
    batch, seq = stream.batch, stream.seq
    tq = min(ATTN_TQ, seq)
    tk = min(ATTN_TK, seq)
    n_tiles = seq // tk
    nq = seq // tq
    in_specs = [pl.BlockSpec((dv, tq), lambda b, h, i: (h, b * nq + i)),
                pl.BlockSpec((seq, dv), lambda b, h, i: (b, h)),
                pl.BlockSpec((dv, seq), lambda b, h, i: (h, b))]
    args = [q_t, k, v_t]
    scratch = [pltpu.VMEM((2, n_tiles, tk, tq), F32)]
    if has_cache:
        ck, cv_t = rest[0], rest[1]
        past = cv_t.shape[3]
        in_specs += [pl.BlockSpec((past, dv), lambda b, h, i: (b, h)),
                     pl.BlockSpec((None, None, dv, past), lambda b, h, i: (b, h, 0, 0))]
        args += [ck, cv_t]
        scratch += [pltpu.VMEM((2, past, tq), F32)]
    in_specs += [pl.BlockSpec(da_lam.shape, lambda b, h, i: (0, 0)), pl.BlockSpec((1, dv), lambda b, h, i: (0, 0))]
    args += [da_lam.astype(F32), subln_g.reshape(1, dv).astype(F32)]
    return pl.pallas_call(
        functools.partial(_attn_t_kernel, n_tiles=n_tiles, has_cache=has_cache, lam_init=lam_init,
                          scale=1.0 / math.sqrt(dv // 2)),
        out_shape=jax.ShapeDtypeStruct((n, heads * dv), BF16),
        grid=(batch, heads, nq),
        in_specs=in_specs,
        out_specs=pl.BlockSpec((tq, dv), lambda b, h, i: (b * nq + i, h)),
        scratch_shapes=scratch,
        compiler_params=_cparams("parallel", "parallel", "parallel"),
        name="diff_attention",
    )(*args)


def _rope_t_epilogue(acc, cos_ref, sin_ref):
    cos = cos_ref[...]
    sin = sin_ref[...]
    dk = cos.shape[0]
    row = lax.broadcasted_iota(jnp.int32, cos.shape, 0)
    first = (row % (dk // 2)) < dk // 4
    outs = []
    for g in range(acc.shape[0] // dk):
        x = acc[g * dk:(g + 1) * dk, :]
        rot = jnp.where(first, pltpu.roll(x, dk - dk // 4, axis=0), pltpu.roll(x, dk // 4, axis=0))
        outs.append(x * cos + rot * sin)
    return jnp.concatenate(outs, axis=0)


def _dup_epilogue(acc):
    return acc, acc


def dattn_mixer(h, x, mods, layer, stream, cache, wq_t, wk, wv, wv_t, wo, da_lam, subln_g, lam_init):
    dk = subln_g.shape[0] // 2
    v_t = matmul_t(h, wv_t, name="da_v_t")
    if cache is None:
        q_t = matmul_t(h, wq_t, name="da_q_t")
        k32, k = matmul(h, wk, out_dtype=(F32, BF16), epilogue=_dup_epilogue, name="da_k")
        v32 = matmul(h, wv, name="da_v")
        o = diff_attention_t(q_t, k, v_t, da_lam, subln_g, stream=stream, lam_init=lam_init)
        new = (k32, v32)
    else:
        cos, sin = rope_tables(stream.seq, dk)
        tm = min(MM_TM, stream.seq)
        per = stream.seq // tm
        rope = [(cos, (tm, dk), lambda i, j: (i % per, 0)), (sin, (tm, dk), lambda i, j: (i % per, 0))]
        rope_t = [(cos.T, (dk, tm), lambda i, j: (0, i % per)), (sin.T, (dk, tm), lambda i, j: (0, i % per))]
        q_t = matmul_t(h, wq_t, tm=tm, epilogue=_rope_t_epilogue, extras=rope_t, name="da_q_t_rope")
        k = matmul(h, wk, tm=tm, out_dtype=BF16, epilogue=_rope_epilogue, extras=rope, name="da_k_rope")
        o = diff_attention_t(q_t, k, v_t, cache[0], cache[1], da_lam, subln_g, stream=stream, lam_init=lam_init)
        new = None
    return out_proj_residual(o, wo, x, mods, layer, stream, name="da_out"), new


ML_CHUNK = 256


def _ml_conv_kernel(x_ref, cw_ref, cb_ref, o32_ref, o16_ref, *, seq):
    tb = min(SCAN_BLOCK, seq)
    cw = cw_ref[...]
    cb = cb_ref[...]

    def body(i, carry):
        t0 = pl.multiple_of(i * tb, tb)
        y = _conv_rows(x_ref, t0, tb, seq, cw, cb)
        y = y * jax.nn.sigmoid(y)
        o32_ref[pl.ds(t0, tb), :] = y
        o16_ref[pl.ds(t0, tb), :] = y.astype(BF16)
        return carry

    lax.fori_loop(0, seq // tb, body, 0)


def ml_conv(xi, conv_w, conv_b, stream, *, tc=256):
    n, w = xi.shape
    seq = stream.seq
    return pl.pallas_call(
        functools.partial(_ml_conv_kernel, seq=seq),
        out_shape=(jax.ShapeDtypeStruct((n, w), F32), jax.ShapeDtypeStruct((n, w), BF16)),
        grid=(stream.batch, w // tc),
        in_specs=[pl.BlockSpec((seq, tc), lambda b, j: (b, j)),
                  pl.BlockSpec((conv_w.shape[0], tc), lambda b, j: (0, j)),
                  pl.BlockSpec((1, tc), lambda b, j: (0, j))],
        out_specs=(pl.BlockSpec((seq, tc), lambda b, j: (b, j)), pl.BlockSpec((seq, tc), lambda b, j: (b, j))),
        compiler_params=_cparams("parallel", "parallel"),
        name="ml_conv",
    )(xi, conv_w.astype(F32), conv_b.reshape(1, w).astype(F32))


def _ml_gates_kernel(q_ref, k_ref, v_ref, w_ref, b_ref, o_ref, *, n_heads):
    acc = jnp.dot(q_ref[...], w_ref[0], preferred_element_type=F32)
    acc = acc + jnp.dot(k_ref[...], w_ref[1], preferred_element_type=F32)
    acc = acc + jnp.dot(v_ref[...], w_ref[2], preferred_element_type=F32) + b_ref[...]
    col = lax.broadcasted_iota(jnp.int32, acc.shape, 1)
    is_forget = (col // n_heads) % 2 == 1
    o_ref[...] = jnp.where(is_forget, -_softplus(-acc), acc)


def ml_gates(q, k, v, w_gate, b_gate, n_heads, *, tm=512):
    n, w = q.shape
    ng = w_gate.shape[2]
    row = pl.BlockSpec((tm, w), lambda i: (i, 0))
    return pl.pallas_call(
        functools.partial(_ml_gates_kernel, n_heads=n_heads),
        out_shape=jax.ShapeDtypeStruct((n, ng), F32),
        grid=(n // tm,),
        in_specs=[row, row, row, pl.BlockSpec((3, w, ng), lambda i: (0, 0, 0)), pl.BlockSpec((1, ng), lambda i: (0, 0))],
        out_specs=pl.BlockSpec((tm, ng), lambda i: (i, 0)),
        compiler_params=_cparams("parallel"),
        name="ml_gates",
    )(q, k, v, w_gate, b_gate.reshape(1, ng).astype(F32))


ML_HEADS_PER_STEP = 2


def _mlstm_cell_kernel(*refs, n_heads, has_init, want_final):
    q_ref, k_ref, kt_ref, v_ref, g_ref, gt_ref = refs[:6]
    init_refs = refs[6:9] if has_init else None
    h_ref = refs[9 if has_init else 6]
    ct_s, n_s, m_s = refs[-3:]
    final_refs = refs[-6:-3] if want_final else None
    hb = ct_s.shape[0]
    dh = ct_s.shape[1]
    h0 = pl.program_id(1) * hb
    d = pl.program_id(2)
    ci = pl.program_id(3)
    reverse = d == 1
    t = q_ref.shape[0]

    @pl.when(ci == 0)
    def _():
        for j in range(hb):
            if has_init:
                ct_s[j] = init_refs[0][j].T
                n_s[j] = init_refs[1][j]
                m_s[j] = init_refs[2][j]
            else:
                ct_s[j] = jnp.zeros((dh, dh), F32)
                n_s[j] = jnp.zeros((1, dh), F32)
                m_s[j] = jnp.zeros((1, 1), F32)

    g = g_ref[...]
    gt = gt_ref[...]
    lane = lax.broadcasted_iota(jnp.int32, g.shape, 1)
    sub = lax.broadcasted_iota(jnp.int32, gt.shape, 0)
    r_idx = lax.broadcasted_iota(jnp.int32, (t, t), 0)
    s_idx = lax.broadcasted_iota(jnp.int32, (t, t), 1)
    ahead = (s_idx - r_idx) * jnp.where(reverse, -1, 1)
    mask = ahead <= 0
    mask_t = ahead >= 0
    row = lax.broadcasted_iota(jnp.int32, (t, 1), 0)
    last = row == jnp.where(reverse, 0, t - 1)

    for j in range(hb):
        cols = slice(j * dh, (j + 1) * dh)
        q = q_ref[:, cols]
        k = k_ref[:, cols]
        v = v_ref[:, cols]
        kt = kt_ref[cols, :]
        col_i = d * (2 * n_heads) + h0 + j
        col_f = col_i + n_heads
        ig_col = jnp.sum(jnp.where(lane == col_i, g, 0.0), axis=1, keepdims=True)
        lf_col = jnp.sum(jnp.where(lane == col_f, g, 0.0), axis=1, keepdims=True)
        ig_row = jnp.sum(jnp.where(sub == col_i, gt, 0.0), axis=0, keepdims=True)
        lf_row = jnp.sum(jnp.where(sub == col_f, gt, 0.0), axis=0, keepdims=True)
        b_col = jnp.sum(jnp.where(mask, lf_row, 0.0), axis=1, keepdims=True)
        b_row = jnp.sum(jnp.where(mask_t, lf_col, 0.0), axis=0, keepdims=True)
        m_prev = m_s[j]
        dlog = jnp.where(mask, b_col - b_row + ig_row, -jnp.inf)
        inter = b_col + m_prev
        m_t = jnp.maximum(inter, jnp.max(dlog, axis=1, keepdims=True))
        w_intra = jnp.exp(dlog - m_t)
        w_inter = jnp.exp(inter - m_t)
        s_mat = lax.dot_general(q, k, (((1,), (1,)), ((), ())), preferred_element_type=F32) * w_intra
        ct_old = ct_s[j]
        n_old = n_s[j]
        qc = jnp.dot(q, ct_old.astype(BF16), preferred_element_type=F32)
        num = jnp.dot(s_mat.astype(BF16), v, preferred_element_type=F32) + w_inter * qc
        n_rows = jnp.broadcast_to(n_old.astype(BF16), (SUBLANES, dh))
        qn = lax.dot_general(q, n_rows, (((1,), (1,)), ((), ())), preferred_element_type=F32)[:, 0:1]
        den = jnp.sum(s_mat, axis=1, keepdims=True) + w_inter * qn
        h_ref[:, cols] = num / jnp.maximum(jnp.abs(den), jnp.exp(-m_t))
        m_new = jnp.sum(jnp.where(last, m_t, 0.0), axis=0, keepdims=True)
        b_last = jnp.sum(jnp.where(last, b_col, 0.0), axis=0, keepdims=True)
        w_old = jnp.exp(b_last + m_prev - m_new)
        w_in = jnp.exp(b_last - b_col + ig_col - m_new)
        vw = (v.astype(F32) * w_in).astype(BF16)
        ct_s[j] = w_old * ct_old + jnp.dot(kt, vw, preferred_element_type=F32)
        n_s[j] = w_old * n_old + jnp.sum(w_in * k.astype(F32), axis=0, keepdims=True)
        m_s[j] = m_new

    if want_final:
        @pl.when(ci == pl.num_programs(3) - 1)
        def _():
            for j in range(hb):
                final_refs[0][j] = ct_s[j].T
                final_refs[1][j] = n_s[j]
                final_refs[2][j] = m_s[j]


def mlstm_cell(q, k, k_t, v, g, g_t, init, stream, heads, want_final):
    n, w = q.shape
    batch, seq = stream.batch, stream.seq
    dh = w // heads
    hb = ML_HEADS_PER_STEP
    t = min(ML_CHUNK, seq)
    nc = seq // t

    def blk(b, d, ci):
        return b * nc + ci + d * (nc - 1 - 2 * ci)

    tok = pl.BlockSpec((t, hb * dh), lambda b, h, d, ci: (blk(b, d, ci), h))
    st5 = lambda r, c: pl.BlockSpec((None, None, hb, r, c), lambda b, h, d, ci: (b, d, h, 0, 0))
    state_dims = ((dh, dh), (1, dh), (1, 1))
    out_shape = [jax.ShapeDtypeStruct((2, n, w), F32)]
    out_specs = [pl.BlockSpec((None, t, hb * dh), lambda b, h, d, ci: (d, blk(b, d, ci), h))]
    if want_final:
        out_shape += [jax.ShapeDtypeStruct((batch, 2, heads) + rc, F32) for rc in state_dims]
        out_specs += [st5(*rc) for rc in state_dims]
    res = pl.pallas_call(
        functools.partial(_mlstm_cell_kernel, n_heads=heads, has_init=init is not None, want_final=want_final),
        out_shape=tuple(out_shape),
        grid=(batch, heads // hb, 2, nc),
        in_specs=[tok, tok,
                  pl.BlockSpec((hb * dh, t), lambda b, h, d, ci: (h, blk(b, d, ci))),
                  tok,
                  pl.BlockSpec((t, g.shape[1]), lambda b, h, d, ci: (blk(b, d, ci), 0)),
                  pl.BlockSpec((g.shape[1], t), lambda b, h, d, ci: (0, blk(b, d, ci)))]
        + ([st5(*rc) for rc in state_dims] if init is not None else []),
        out_specs=tuple(out_specs),
        scratch_shapes=[pltpu.VMEM((hb,) + rc, F32) for rc in state_dims],
        compiler_params=_cparams("parallel", "parallel", "arbitrary", "arbitrary"),
        name="mlstm_cell",
    )(q, k, k_t, v, g, g_t, *(init or ()))
    return res[0], (tuple(res[1:]) if want_final else None)


def _scale_epilogue(acc, *, scale):
    return acc * scale


def _ml_out_gate_epilogue(acc, b_ref, h_ref, xc_ref, gn_ref, skip_ref):
    hc = jax.nn.sigmoid(acc + b_ref[...]) * (h_ref[0] + h_ref[1])
    hn = hc * lax.rsqrt(jnp.mean(hc * hc, axis=-1, keepdims=True) + NORM_EPS)
    return hn * gn_ref[...] + skip_ref[...] * xc_ref[...]


def mlstm_mixer(h, x, mods, layer, stream, init, want_final, w_up, conv_w, conv_b, wq, wk, wk_t, wv, w_gate, b_gate,
                w_o, b_o, gn, skip, w_down):
    dh = wq.shape[1]
    xi = matmul(h, w_up, name="ml_up")
    xc, xc16 = ml_conv(xi, conv_w, conv_b, stream)
    q = matmul(xc16, wq, tm=MM_TM_DIAG, tn=dh, out_dtype=BF16, k_block=dh, name="ml_q")
    k = matmul(xc16, wk, tm=MM_TM_DIAG, tn=dh, out_dtype=BF16, k_block=dh, name="ml_k",
               epilogue=functools.partial(_scale_epilogue, scale=1.0 / math.sqrt(dh)))
    v = matmul(xi, wv, tm=MM_TM_DIAG, tn=dh, out_dtype=BF16, k_block=dh, name="ml_v")
    w = xi.shape[1]
    g = ml_gates(q, k, v, w_gate, b_gate, w // dh)
    k_t = matmul_t(xc16, wk_t, tm=MM_TM_DIAG, tn=dh, k_block=dh, name="ml_k_t",
                   epilogue=functools.partial(_scale_epilogue, scale=1.0 / math.sqrt(dh)))
    hdir, final = mlstm_cell(q, k, k_t, v, g, g.T, init, stream, w // dh, want_final)
    tm = min(MM_TM, h.shape[0])
    vec = lambda a: (a.reshape(1, w).astype(F32), (1, dh), lambda i, j: (0, j))
    y = matmul(h, w_o, tm=tm, tn=dh, out_dtype=BF16, name="ml_o", epilogue=_ml_out_gate_epilogue,
               extras=[vec(b_o), (hdir, (2, tm, dh), lambda i, j: (0, i, j)), (xc, (tm, dh), lambda i, j: (i, j)),
                       vec(gn), vec(skip)])
    return out_proj_residual(y, w_down, x, mods, layer, stream, name="ml_down"), final


def kernel(x_prompt, x_sample, state_s5, state_rglru, cache_dattn_k, cache_dattn_v, state_mlstm_C, state_mlstm_n, state_mlstm_m, c, c_ctx, ada_w, ada_b, norm_g, ffn1_w1, ffn1_w3, ffn1_w2, ffn2_w1, ffn2_w3, ffn2_w2, final_norm_g, s5_w_in, s5_a_re, s5_a_im, s5_log_dt, s5_b_re, s5_b_im, s5_c_re, s5_c_im, s5_d, s5_w_glu, s5_b_glu, s5_w_out, rg_w_in, rg_w_gate, rg_conv_w, rg_conv_b, rg_wa, rg_ba, rg_wx, rg_bx, rg_lam, rg_w_out, da_wq, da_wk, da_wv, da_wo, da_lam, da_subln_g, ml_w_up, ml_conv_w, ml_conv_b, ml_wq, ml_wk, ml_wv, ml_w_gate, ml_b_gate, ml_w_o, ml_b_o, ml_gn, ml_skip, ml_w_down):
    bc, lc, d = x_prompt.shape
    bl, ll, _ = x_sample.shape
    depth = ada_w.shape[0]
    cs = Stream(bc, lc, 0, False)
    ls = Stream(bl, ll, 1, True)
    ctx = x_prompt.reshape(cs.n, d)
    lat = x_sample.reshape(ls.n, d)
    c_all = jnp.concatenate([c_ctx[None, :], c, jnp.zeros((MOD_ROWS - 1 - bl, d), F32)], axis=0)
    mods = adaln(c_all, ada_w, ada_b)
    bf = lambda a: a.astype(BF16)
    ffn1 = (bf(ffn1_w1), bf(ffn1_w3), bf(ffn1_w2))
    ffn2 = (bf(ffn2_w1), bf(ffn2_w3), bf(ffn2_w2))

    new_s5 = new_rg = None
    for i in range(depth):
        ctx = ffn_half(ctx, norm_g[i, 0], mods, i, 0, *ffn1, cs)
        lat = ffn_half(lat, norm_g[i, 0], mods, i, 0, *ffn1, ls)
        hc = modnorm(ctx, norm_g[i, 1], mods, i, 3, cs)
        hl = modnorm(lat, norm_g[i, 1], mods, i, 3, ls)
        kind = i % 4
        if kind == 0:
            ops = _s5_slab_operators(s5_a_re, s5_a_im, s5_log_dt, s5_b_re, s5_b_im, s5_c_re, s5_c_im)
            p = s5_a_re.shape[2]
            slab_groups = LANES // s5_b_re.shape[3]
            args = (bf(s5_w_in), ops, s5_d, bf(s5_w_glu), s5_b_glu, bf(s5_w_out))
            zero_state = jnp.zeros((bc,) + state_s5.shape[1:], F32)
            ctx, fin = s5_slab_mixer(hc, ctx, mods, i, cs, s5_pack_state(zero_state, slab_groups), *args)
            lat, _ = s5_slab_mixer(hl, lat, mods, i, ls, s5_pack_state(state_s5, slab_groups), *args)
            new_s5 = s5_unpack_state(fin, slab_groups, p)
        elif kind == 1:
            w_gates = bf(jnp.concatenate([rg_wa, rg_wx], axis=-1))
            bias = jnp.stack([rg_ba, rg_bx], axis=1).astype(F32)
            args = (bf(rg_w_in), bf(rg_w_gate), rg_conv_w, rg_conv_b, w_gates, bias, rg_lam, bf(rg_w_out))
            ctx, new_rg = rg_mixer(hc, ctx, mods, i, cs, jnp.zeros((bc, 2, rg_lam.shape[1]), F32), *args)
            lat, _ = rg_mixer(hl, lat, mods, i, ls, state_rglru, *args)
        elif kind == 2:
            lam_init = 0.8 - 0.6 * math.exp(-0.3 * i)
            heads, dv = cache_dattn_v.shape[2], cache_dattn_v.shape[3]
            args = (bf(da_wq).T, bf(da_wk), bf(da_wv), bf(da_wv).T, bf(da_wo), da_lam, da_subln_g, lam_init)
            ctx, (k32, v32) = dattn_mixer(hc, ctx, mods, i, cs, None, *args)
            new_k = k32.reshape(bc, lc, heads, 2, dv // 2)
            new_v = v32.reshape(bc, lc, heads, dv)
            cache = (bf(cache_dattn_k).reshape(-1, heads * dv), bf(cache_dattn_v).transpose(0, 2, 3, 1))
            lat, _ = dattn_mixer(hl, lat, mods, i, ls, cache, *args)
        else:
            heads, dh = ml_wq.shape[0], ml_wq.shape[1]
            args = (bf(ml_w_up), ml_conv_w, ml_conv_b, bf(ml_wq).reshape(heads * dh, dh),
                    bf(ml_wk).reshape(heads * dh, dh), bf(ml_wk).transpose(0, 2, 1).reshape(heads * dh, dh),
                    bf(ml_wv).reshape(heads * dh, dh), bf(ml_w_gate), ml_b_gate,
                    bf(ml_w_o), ml_b_o, ml_gn, ml_skip, bf(ml_w_down))
            ctx, (cf, nf, mf) = mlstm_mixer(hc, ctx, mods, i, cs, None, True, *args)
            init = (state_mlstm_C.astype(F32), state_mlstm_n.astype(F32).reshape(bl, 2, heads, 1, dh),
                    state_mlstm_m.astype(F32).reshape(bl, 2, heads, 1, 1))
            lat, _ = mlstm_mixer(hl, lat, mods, i, ls, init, False, *args)
            new_c, new_n, new_m = cf, nf.reshape(bc, 2, heads, dh), mf.reshape(bc, 2, heads)
        ctx = ffn_half(ctx, norm_g[i, 2], mods, i, 6, *ffn2, cs)
        lat = ffn_half(lat, norm_g[i, 2], mods, i, 6, *ffn2, ls)
    y_prompt = rmsnorm(ctx, final_norm_g).reshape(bc, lc, d)
    y_sample = rmsnorm(lat, final_norm_g).reshape(bl, ll, d)
    return (y_prompt, y_sample, new_s5, new_rg, new_k, new_v, new_c, new_n, new_m)
```

```python
import functools
import math

import jax
import jax.numpy as jnp
from jax import lax
from jax.experimental import pallas as pl
from jax.experimental.pallas import tpu as pltpu

F32 = jnp.float32
BF16 = jnp.bfloat16

NORM_EPS = 1e-6
VMEM_LIMIT_BYTES = 56 * 1024 * 1024
MOD_ROWS = 8


def _cparams(*sem):
    return pltpu.CompilerParams(dimension_semantics=sem, vmem_limit_bytes=VMEM_LIMIT_BYTES)


def _bdot(a, b):
    return jnp.dot(a.astype(BF16), b.astype(BF16), preferred_element_type=F32)


def _mod_row(ref, row):
    return ref[pl.ds(row, 1), :]


class Stream:
    def __init__(self, batch, seq, row0, per_batch_mod):
        self.batch, self.seq, self.row0 = batch, seq, row0
        self.n = batch * seq
        self.rows_per_mod = seq if per_batch_mod else self.n

    def mod_row(self, first_token):
        return self.row0 + first_token // self.rows_per_mod


def _adaln_kernel(c_ref, w_ref, b_ref, o_ref):
    c = c_ref[...]
    s = c * jax.nn.sigmoid(c)
    o_ref[...] = _bdot(s, w_ref[...]) + b_ref[...]


def adaln(c_all, ada_w, ada_b, *, tn=1024):
    nl, d, m = ada_w.shape
    return pl.pallas_call(
        _adaln_kernel,
        out_shape=jax.ShapeDtypeStruct((nl, MOD_ROWS, m), F32),
        grid=(nl, m // tn),
        in_specs=[pl.BlockSpec((MOD_ROWS, d), lambda l, j: (0, 0)),
                  pl.BlockSpec((None, d, tn), lambda l, j: (l, 0, j)),
                  pl.BlockSpec((None, 1, tn), lambda l, j: (l, 0, j))],
        out_specs=pl.BlockSpec((None, MOD_ROWS, tn), lambda l, j: (l, 0, j)),
        compiler_params=_cparams("parallel", "parallel"),
        name="adaln",
    )(c_all, ada_w, ada_b.reshape(nl, 1, m))


def _mod_spec(d, layer, k):
    return pl.BlockSpec((None, MOD_ROWS, d), lambda *_: (layer, 0, k))


def _rms(x, g):
    return x * lax.rsqrt(jnp.mean(x * x, axis=-1, keepdims=True) + NORM_EPS) * g


def _modnorm_kernel(x_ref, g_ref, sh_ref, sc_ref, o_ref, *, stream, tm):
    row = stream.mod_row(pl.program_id(0) * tm)
    y = _rms(x_ref[...], g_ref[...])
    o_ref[...] = (y * (1.0 + _mod_row(sc_ref, row)) + _mod_row(sh_ref, row)).astype(o_ref.dtype)


def modnorm(x, g, mods, layer, k_shift, stream, *, tm=512):
    n, d = x.shape
    return pl.pallas_call(
        functools.partial(_modnorm_kernel, stream=stream, tm=tm),
        out_shape=jax.ShapeDtypeStruct((n, d), BF16),
        grid=(n // tm,),
        in_specs=[pl.BlockSpec((tm, d), lambda i: (i, 0)),
                  pl.BlockSpec((1, d), lambda i: (0, 0)),
                  _mod_spec(d, layer, k_shift),
                  _mod_spec(d, layer, k_shift + 1)],
        out_specs=pl.BlockSpec((tm, d), lambda i: (i, 0)),
        compiler_params=_cparams("parallel"),
        name="modnorm",
    )(x, g.reshape(1, d), mods, mods)


def _rmsnorm_kernel(x_ref, g_ref, o_ref):
    o_ref[...] = _rms(x_ref[...], g_ref[...])


def rmsnorm(x, g, *, tm=512):
    n, d = x.shape
    return pl.pallas_call(
        _rmsnorm_kernel,
        out_shape=jax.ShapeDtypeStruct((n, d), F32),
        grid=(n // tm,),
        in_specs=[pl.BlockSpec((tm, d), lambda i: (i, 0)), pl.BlockSpec((1, d), lambda i: (0, 0))],
        out_specs=pl.BlockSpec((tm, d), lambda i: (i, 0)),
        compiler_params=_cparams("parallel"),
        name="rmsnorm",
    )(x, g.reshape(1, d))


def _ffn_kernel(x_ref, g_ref, sh_ref, sc_ref, gt_ref, w1_ref, w3_ref, w2_ref, o_ref, h_ref, acc_ref,
                *, stream, tm):
    f = pl.program_id(1)
    row = stream.mod_row(pl.program_id(0) * tm)

    @pl.when(f == 0)
    def _():
        y = _rms(x_ref[...], g_ref[...])
        h_ref[...] = (y * (1.0 + _mod_row(sc_ref, row)) + _mod_row(sh_ref, row)).astype(BF16)
        acc_ref[...] = jnp.zeros_like(acc_ref)

    h = h_ref[...]
    a = jnp.dot(h, w1_ref[...], preferred_element_type=F32)
    b = jnp.dot(h, w3_ref[...], preferred_element_type=F32)
    u = (a * jax.nn.sigmoid(a)) * b
    acc_ref[...] += jnp.dot(u.astype(BF16), w2_ref[...], preferred_element_type=F32)

    @pl.when(f == pl.num_programs(1) - 1)
    def _():
        o_ref[...] = x_ref[...] + (0.5 * _mod_row(gt_ref, row)) * acc_ref[...]


def ffn_half(x, g, mods, layer, k_shift, w1, w3, w2, stream, *, tm=512, tf=512):
    n, d = x.shape
    dff = w1.shape[2]
    return pl.pallas_call(
        functools.partial(_ffn_kernel, stream=stream, tm=tm),
        out_shape=jax.ShapeDtypeStruct((n, d), F32),
        grid=(n // tm, dff // tf),
        in_specs=[pl.BlockSpec((tm, d), lambda i, f: (i, 0)),
                  pl.BlockSpec((1, d), lambda i, f: (0, 0)),
                  _mod_spec(d, layer, k_shift),
                  _mod_spec(d, layer, k_shift + 1),
                  _mod_spec(d, layer, k_shift + 2),
                  pl.BlockSpec((None, d, tf), lambda i, f: (layer, 0, f)),
                  pl.BlockSpec((None, d, tf), lambda i, f: (layer, 0, f)),
                  pl.BlockSpec((None, tf, d), lambda i, f: (layer, f, 0))],
        out_specs=pl.BlockSpec((tm, d), lambda i, f: (i, 0)),
        scratch_shapes=[pltpu.VMEM((tm, d), BF16), pltpu.VMEM((tm, d), F32)],
        compiler_params=_cparams("parallel", "arbitrary"),
        name="ffn_half",
    )(x, g.reshape(1, d), mods, mods, mods, w1, w3, w2)


def _mm_kernel(x_ref, w_ref, *rest, epilogue, n_extra):
    extra, o_refs = rest[:n_extra], rest[n_extra:]
    acc = _bdot(x_ref[...], w_ref[...])
    outs = epilogue(acc, *extra) if epilogue is not None else acc
    if not isinstance(outs, tuple):
        outs = (outs,)
    for o_ref, o in zip(o_refs, outs):
        o_ref[...] = o.astype(o_ref.dtype)


MM_TM = 1024
MM_TN = 512
MM_TM_DIAG = 2048


def matmul(x, w, *, tm=MM_TM, tn=MM_TN, out_dtype=F32, epilogue=None, extras=(), k_block=None, slab_out=False,
           name="matmul"):
    n, kx = x.shape
    tm = min(tm, n)
    if k_block is None:
        kb, m = kx, w.shape[1]
        x_map = lambda i, j: (i, 0)
        w_map = lambda i, j: (0, j)
    else:
        kb = k_block
        per = w.shape[1] // tn
        m = (kx // kb) * w.shape[1]
        x_map = lambda i, j: (i, j // per)
        w_map = lambda i, j: (j // per, j % per)
    dts = out_dtype if isinstance(out_dtype, tuple) else (out_dtype,)
    if slab_out:
        out_shape = tuple(jax.ShapeDtypeStruct((m // 128, n, 128), dt) for dt in dts)
        out_specs = tuple(pl.BlockSpec((tn // 128, tm, 128), lambda i, j: (j, i, 0)) for _ in dts)
    else:
        out_shape = tuple(jax.ShapeDtypeStruct((n, m), dt) for dt in dts)
        out_specs = tuple(pl.BlockSpec((tm, tn), lambda i, j: (i, j)) for _ in dts)
    res = pl.pallas_call(
        functools.partial(_mm_kernel, epilogue=epilogue, n_extra=len(extras)),
        out_shape=out_shape,
        grid=(n // tm, m // tn),
        in_specs=[pl.BlockSpec((tm, kb), x_map), pl.BlockSpec((kb, tn), w_map)]
        + [pl.BlockSpec(bs, im) for _, bs, im in extras],
        out_specs=out_specs,
        compiler_params=_cparams("parallel", "parallel"),
        name=name,
    )(x, w, *[a for a, _, _ in extras])
    return res if isinstance(out_dtype, tuple) else res[0]


def _mm_t_kernel(x_ref, w_ref, *rest, epilogue, n_extra):
    extra, o_ref = rest[:n_extra], rest[n_extra]
    acc = lax.dot_general(w_ref[...], x_ref[...].astype(BF16), (((1,), (1,)), ((), ())), preferred_element_type=F32)
    o_ref[...] = (epilogue(acc, *extra) if epilogue is not None else acc).astype(o_ref.dtype)


def matmul_t(x, w_t, *, tm=MM_TM, tn=MM_TN, out_dtype=BF16, epilogue=None, extras=(), k_block=None, name="matmul_t"):
    n, kx = x.shape
    m = w_t.shape[0]
    tm = min(tm, n)
    if k_block is None:
        kb = kx
        x_map = lambda i, j: (i, 0)
    else:
        kb = k_block
        per = (m // (kx // kb)) // tn
        x_map = lambda i, j: (i, j // per)
    return pl.pallas_call(
        functools.partial(_mm_t_kernel, epilogue=epilogue, n_extra=len(extras)),
        out_shape=jax.ShapeDtypeStruct((m, n), out_dtype),
        grid=(n // tm, m // tn),
        in_specs=[pl.BlockSpec((tm, kb), x_map), pl.BlockSpec((tn, kb), lambda i, j: (j, 0))]
        + [pl.BlockSpec(bs, im) for _, bs, im in extras],
        out_specs=pl.BlockSpec((tn, tm), lambda i, j: (j, i)),
        compiler_params=_cparams("parallel", "parallel"),
        name=name,
    )(x, w_t, *[a for a, _, _ in extras])


S5_CHUNK = 16
S5_GROUPS_PER_STEP = 8


def _s5_operators(a_re, a_im, log_dt, b_re, b_im, c_re, c_im):
    hp = lax.Precision.HIGHEST
    t = S5_CHUNK
    a_re, a_im, b_re, b_im, c_re, c_im = (z.astype(F32) for z in (a_re, a_im, b_re, b_im, c_re, c_im))
    dt = jnp.exp(log_dt.astype(F32))[..., None]
    mag = jnp.exp(a_re * dt)
    lr = mag * jnp.cos(a_im * dt)
    li = mag * jnp.sin(a_im * dt)
    den = a_re * a_re + a_im * a_im
    cr = ((lr - 1.0) * a_re + li * a_im) / den
    ci = (li * a_re - (lr - 1.0) * a_im) / den
    bbr = cr[..., None] * b_re - ci[..., None] * b_im
    bbi = cr[..., None] * b_im + ci[..., None] * b_re
    k = jnp.arange(t + 1, dtype=F32)[None, None, :, None]
    pmag = jnp.exp(k * (a_re * dt)[:, :, None, :])
    pr = pmag * jnp.cos(k * (a_im * dt)[:, :, None, :])
    pi = pmag * jnp.sin(k * (a_im * dt)[:, :, None, :])
    pbr = pr[..., None] * bbr[:, :, None] - pi[..., None] * bbi[:, :, None]
    pbi = pr[..., None] * bbi[:, :, None] + pi[..., None] * bbr[:, :, None]
    m = (jnp.einsum('dgop,dgkpi->dgkoi', c_re, pbr, precision=hp)
         - jnp.einsum('dgop,dgkpi->dgkoi', c_im, pbi, precision=hp))
    m = m.at[:, :, t].set(0.0)
    s_idx = jnp.arange(t)[:, None]
    t_idx = jnp.arange(t)[None, :]
    lag_f = jnp.where(t_idx >= s_idx, t_idx - s_idx, t)
    lag_b = jnp.where(s_idx >= t_idx, s_idx - t_idx, t)

    def toeplitz(md, lag):
        x = md[:, lag]
        return x.transpose(0, 1, 4, 2, 3).reshape(md.shape[0], t * x.shape[-1], t * x.shape[-2])

    toep = toeplitz(m[0], lag_f) + toeplitz(m[1], lag_b)
    pow_s = jnp.stack([t - 1 - jnp.arange(t), jnp.arange(t)])

    def smat(d):
        r = pbr[d][:, pow_s[d]]
        i = pbi[d][:, pow_s[d]]
        x = jnp.concatenate([r, i], axis=2)
        return x.transpose(0, 1, 3, 2).reshape(x.shape[0], t * x.shape[3], x.shape[2])

    s_op = jnp.stack([smat(0), smat(1)])
    pow_w = jnp.stack([jnp.arange(t) + 1, t - jnp.arange(t)])

    def wmat(d):
        zr = (c_re[d][:, None] * pr[d][:, pow_w[d]][:, :, None, :]
              - c_im[d][:, None] * pi[d][:, pow_w[d]][:, :, None, :])
        zi = (c_re[d][:, None] * pi[d][:, pow_w[d]][:, :, None, :]
              + c_im[d][:, None] * pr[d][:, pow_w[d]][:, :, None, :])
        x = jnp.concatenate([zr, -zi], axis=3)
        return x.transpose(0, 3, 1, 2).reshape(x.shape[0], x.shape[3], t * x.shape[2])

    w_op = jnp.stack([wmat(0), wmat(1)])
    mul_a = jnp.concatenate([pr[:, :, t], pr[:, :, t]], axis=-1)
    mul_b = jnp.concatenate([-pi[:, :, t], pi[:, :, t]], axis=-1)
    return toep.astype(BF16), s_op.astype(BF16), w_op.astype(BF16), mul_a, mul_b


def _s5_intra_kernel(u_ref, toep_ref, s_ref, y_ref, st_ref):
    for g in range(u_ref.shape[0]):
        u = u_ref[g]
        y_ref[g] = jnp.dot(u, toep_ref[g], preferred_element_type=F32)
        for d in range(2):
            st_ref[d, g] = jnp.dot(u, s_ref[d, g], preferred_element_type=F32)


def _s5_recur_kernel(st_ref, a_ref, b_ref, h0_ref, hin_ref, fin_ref, *, n_chunks):
    rev = pl.program_id(0) == 1
    half = a_ref.shape[-1] // 2
    a = a_ref[...]
    b = b_ref[...]
    b_sw = pltpu.roll(b, half, axis=2)

    def body(i, carry):
        h, h_sw = carry
        j = jnp.where(rev, n_chunks - 1 - i, i)
        hin_ref[j] = h
        s = st_ref[j]
        return h * a + h_sw * b + s, h_sw * a + h * b_sw + pltpu.roll(s, half, axis=2)

    h0 = h0_ref[...]
    fin_ref[...] = lax.fori_loop(0, n_chunks, body, (h0, pltpu.roll(h0, half, axis=2)))[0]


def _s5_inter_kernel(y_ref, hin_ref, w_ref, o_ref):
    for g in range(y_ref.shape[0]):
        acc = y_ref[g]
        for d in range(2):
            acc = acc + jnp.dot(hin_ref[d, g].astype(BF16), w_ref[d, g], preferred_element_type=F32)
        o_ref[g] = acc


def s5_scan(u, h0, ops, batch, seq):
    toep, s_op, w_op, mul_a, mul_b = ops
    ng, tc, _ = toep.shape
    t = S5_CHUNK
    c = tc // t
    p2 = s_op.shape[-1]
    nj = seq // t
    rows = nj * batch
    gs = S5_GROUPS_PER_STEP
    ug = u.astype(BF16).reshape(batch, nj, t, ng, c).transpose(3, 1, 0, 2, 4).reshape(ng, rows, tc)
    y_intra, st = pl.pallas_call(
        _s5_intra_kernel,
        out_shape=(jax.ShapeDtypeStruct((ng, rows, tc), F32), jax.ShapeDtypeStruct((2, ng, rows, p2), F32)),
        grid=(ng // gs,),
        in_specs=[pl.BlockSpec((gs, rows, tc), lambda g: (g, 0, 0)),
                  pl.BlockSpec((gs, tc, tc), lambda g: (g, 0, 0)),
                  pl.BlockSpec((2, gs, tc, p2), lambda g: (0, g, 0, 0))],
        out_specs=(pl.BlockSpec((gs, rows, tc), lambda g: (g, 0, 0)),
                   pl.BlockSpec((2, gs, rows, p2), lambda g: (0, g, 0, 0))),
        compiler_params=_cparams("parallel"),
        name="s5_intra",
    )(ug, toep, s_op)
    st5 = st.reshape(2, ng, nj, batch, p2).transpose(0, 2, 1, 3, 4)
    hin, fin = pl.pallas_call(
        functools.partial(_s5_recur_kernel, n_chunks=nj),
        out_shape=(jax.ShapeDtypeStruct((2, nj, ng, batch, p2), F32),
                   jax.ShapeDtypeStruct((2, ng, batch, p2), F32)),
        grid=(2, ng // gs),
        in_specs=[pl.BlockSpec((None, nj, gs, batch, p2), lambda d, g: (d, 0, g, 0, 0)),
                  pl.BlockSpec((None, gs, 1, p2), lambda d, g: (d, g, 0, 0)),
                  pl.BlockSpec((None, gs, 1, p2), lambda d, g: (d, g, 0, 0)),
                  pl.BlockSpec((None, gs, batch, p2), lambda d, g: (d, g, 0, 0))],
        out_specs=(pl.BlockSpec((None, nj, gs, batch, p2), lambda d, g: (d, 0, g, 0, 0)),
                   pl.BlockSpec((None, gs, batch, p2), lambda d, g: (d, g, 0, 0))),
        compiler_params=_cparams("parallel", "parallel"),
        name="s5_recur",
    )(st5, mul_a.reshape(2, ng, 1, p2), mul_b.reshape(2, ng, 1, p2), h0)
    hin = hin.transpose(0, 2, 1, 3, 4).reshape(2, ng, rows, p2)
    yg = pl.pallas_call(
        _s5_inter_kernel,
        out_shape=jax.ShapeDtypeStruct((ng, rows, tc), F32),
        grid=(ng // gs,),
        in_specs=[pl.BlockSpec((gs, rows, tc), lambda g: (g, 0, 0)),
                  pl.BlockSpec((2, gs, rows, p2), lambda g: (0, g, 0, 0)),
                  pl.BlockSpec((2, gs, p2, tc), lambda g: (0, g, 0, 0))],
        out_specs=pl.BlockSpec((gs, rows, tc), lambda g: (g, 0, 0)),
        compiler_params=_cparams("parallel"),
        name="s5_inter",
    )(y_intra, hin, w_op)
    y = yg.reshape(ng, nj, batch, t, c).transpose(2, 1, 3, 0, 4).reshape(batch * seq, ng * c)
    return y, fin


def _s5_post_kernel(u_ref, y_ref, d_ref, o_ref):
    y = d_ref[...] * u_ref[...] + y_ref[...]
    o_ref[...] = jax.nn.gelu(y)


def s5_post(u, y, d_skip, *, tm=512):
    n, w = u.shape
    return pl.pallas_call(
        _s5_post_kernel,
        out_shape=jax.ShapeDtypeStruct((n, w), F32),
        grid=(n // tm,),
        in_specs=[pl.BlockSpec((tm, w), lambda i: (i, 0)), pl.BlockSpec((tm, w), lambda i: (i, 0)),
                  pl.BlockSpec((1, w), lambda i: (0, 0))],
        out_specs=pl.BlockSpec((tm, w), lambda i: (i, 0)),
        compiler_params=_cparams("parallel"),
        name="s5_post",
    )(u, y, d_skip.reshape(1, w).astype(F32))


def _residual_epilogue(acc, x_ref, gt_ref, *, stream, tm):
    row = stream.mod_row(pl.program_id(0) * tm)
    return x_ref[...] + _mod_row(gt_ref, row) * acc


def out_proj_residual(z, w, x, mods, layer, stream, *, name="out_proj"):
    d = x.shape[1]
    tm, tn = min(MM_TM, x.shape[0]), MM_TN
    return matmul(z, w, tm=tm, tn=tn, name=name,
                  epilogue=functools.partial(_residual_epilogue, stream=stream, tm=tm),
                  extras=[(x, (tm, tn), lambda i, j: (i, j)),
                          (mods, (None, MOD_ROWS, tn), lambda i, j: (layer, 0, 5 * (d // tn) + j))])


def _glu_epilogue(acc, z_ref, b_ref):
    return z_ref[...] * jax.nn.sigmoid(acc + b_ref[...])


def s5_mixer(h, x, mods, layer, stream, state, w_in, ops, d_skip, w_glu, b_glu, w_out):
    u = matmul(h, w_in, name="s5_in")
    y, fin = s5_scan(u, state, ops, stream.batch, stream.seq)
    z = s5_post(u, y, d_skip)
    w = z.shape[1]
    zz = matmul(z, w_glu, out_dtype=BF16, name="s5_glu", epilogue=_glu_epilogue,
                extras=[(z, (min(MM_TM, z.shape[0]), MM_TN), lambda i, j: (i, j)),
                        (b_glu.reshape(1, w).astype(F32), (1, MM_TN), lambda i, j: (0, j))])
    return out_proj_residual(zz, w_out, x, mods, layer, stream, name="s5_out"), fin


SUBLANES = 8
LANES = 128
SCAN_BLOCK = 256
SEQ_BLOCK_ELEMS = 1 << 20


def _conv_rows(x_ref, t0, tb, seq, cw, cb):
    x = x_ref[pl.ds(t0, tb), :]
    prev = x_ref[pl.ds(pl.multiple_of(jnp.maximum(t0 - SUBLANES, 0), SUBLANES), SUBLANES), :]
    nxt = x_ref[pl.ds(pl.multiple_of(jnp.minimum(t0 + tb, seq - SUBLANES), SUBLANES), SUBLANES), :]
    has_prev = t0 > 0
    has_next = t0 + tb < seq
    p1 = jnp.where(has_prev, prev[SUBLANES - 1:SUBLANES, :], 0.0)
    n0 = jnp.where(has_next, nxt[0:1, :], 0.0)
    n1 = jnp.where(has_next, nxt[1:2, :], 0.0)
    rows = lax.broadcasted_iota(jnp.int32, x.shape, 0)
    xm1 = jnp.where(rows == 0, p1, pltpu.roll(x, 1, axis=0))
    xp1 = jnp.where(rows == tb - 1, n0, pltpu.roll(x, tb - 1, axis=0))
    xp2 = jnp.where(rows == tb - 1, n1, jnp.where(rows == tb - 2, n0, pltpu.roll(x, tb - 2, axis=0)))
    return cb + xm1 * cw[0:1, :] + x * cw[1:2, :] + xp1 * cw[2:3, :] + xp2 * cw[3:4, :]


def _group_scan(a, b, reverse):
    tb, tc = a.shape
    a = a.reshape(tb // SUBLANES, SUBLANES, tc)
    b = b.reshape(tb // SUBLANES, SUBLANES, tc)
    pos = lax.broadcasted_iota(jnp.int32, a.shape, 1)
    s = 1
    while s < SUBLANES:
        ok = pos < SUBLANES - s if reverse else pos >= s
        shift = SUBLANES - s if reverse else s
        a_s, b_s = pltpu.roll(a, shift, axis=1), pltpu.roll(b, shift, axis=1)
        b = b + jnp.where(ok, a, 0.0) * b_s
        a = jnp.where(ok, a * a_s, a)
        s *= 2
    return a.reshape(tb, tc), b.reshape(tb, tc)


def _softplus(x):
    return jnp.maximum(x, 0.0) + jnp.log1p(jnp.exp(-jnp.abs(x)))


SLAB_CHUNK = 8


def _s5_slab_operators(a_re, a_im, log_dt, b_re, b_im, c_re, c_im):
    hp = lax.Precision.HIGHEST
    t = SLAB_CHUNK
    a_re, a_im, b_re, b_im, c_re, c_im = (z.astype(F32) for z in (a_re, a_im, b_re, b_im, c_re, c_im))
    ng, p = a_re.shape[1], a_re.shape[2]
    c = b_re.shape[3]
    gs = LANES // c
    ns = ng // gs
    dt = jnp.exp(log_dt.astype(F32))[..., None]
    mag = jnp.exp(a_re * dt)
    lr = mag * jnp.cos(a_im * dt)
    li = mag * jnp.sin(a_im * dt)
    den = a_re * a_re + a_im * a_im
    cr = ((lr - 1.0) * a_re + li * a_im) / den
    ci = (li * a_re - (lr - 1.0) * a_im) / den
    bbr = cr[..., None] * b_re - ci[..., None] * b_im
    bbi = cr[..., None] * b_im + ci[..., None] * b_re

    def powers(k):
        k = k.astype(F32)[None, None, :, None]
        pm = jnp.exp(k * (a_re * dt)[:, :, None, :])
        return pm * jnp.cos(k * (a_im * dt)[:, :, None, :]), pm * jnp.sin(k * (a_im * dt)[:, :, None, :])

    pr, pi = powers(jnp.arange(t + 1))
    pbr = (pr[:, :, :, :, None] * bbr[:, :, None] - pi[:, :, :, :, None] * bbi[:, :, None])
    pbi = (pr[:, :, :, :, None] * bbi[:, :, None] + pi[:, :, :, :, None] * bbr[:, :, None])
    flat = lambda z: z.transpose(0, 1, 3, 2, 4).reshape(2, ng, p, (t + 1) * c)
    m = (jnp.einsum('dgop,dgpx->dgox', c_re, flat(pbr), precision=hp)
         - jnp.einsum('dgop,dgpx->dgox', c_im, flat(pbi), precision=hp)).reshape(2, ng, c, t + 1, c)

    def block_diag(small, row_group, col_group, col_inner):
        rows, q = small.shape[-2], small.shape[-1]
        cols = jnp.arange(gs * q)
        expand = (jnp.arange(q)[:, None] == col_inner(cols)[None, :]).astype(BF16)
        wide = jnp.matmul(small.astype(BF16), expand, preferred_element_type=BF16)
        keep = row_group(jnp.arange(rows))[:, None] == col_group(cols)[None, :]
        return jnp.where(keep, wide, jnp.zeros((), BF16))

    s_idx = jnp.arange(t)[:, None]
    t_idx = jnp.arange(t)[None, :]
    lag = jnp.stack([jnp.clip(t_idx - s_idx, 0, t), jnp.clip(s_idx - t_idx, 0, t)])
    live = jnp.stack([t_idx >= s_idx, s_idx >= t_idx]).astype(F32)
    msel = sum((m[d][:, :, lag[d], :] * live[d][None, None, :, :, None]).transpose(0, 2, 3, 4, 1) for d in range(2))
    msel = msel.reshape(ns, gs, t, t, c, c).transpose(0, 2, 1, 4, 3, 5)
    in_group = lambda r: (r % LANES) // c
    out_col = lambda col: (col // LANES) * c + col % c
    state_group = lambda r: (r % (gs * p)) // p
    state_col = lambda col: (col // (gs * p)) * p + col % p
    toep = block_diag(msel.reshape(ns, t * LANES, t * c), in_group, in_group, out_col)
    pow_s = jnp.stack([t - 1 - jnp.arange(t), jnp.arange(t)])
    pb = jnp.stack([pbr, pbi], axis=2)
    s_sel = jnp.stack([pb[d][:, :, pow_s[d]] for d in range(2)])
    s_sel = s_sel.reshape(2, ns, gs, 2, t, p, c).transpose(0, 1, 4, 2, 6, 3, 5)
    s_op = block_diag(s_sel.reshape(2, ns, t * LANES, 2 * p), in_group, state_group, state_col)
    pow_w = jnp.stack([jnp.arange(t) + 1, t - jnp.arange(t)])
    zr = jnp.stack([c_re[d][:, None] * pr[d][:, pow_w[d]][:, :, None, :]
                    - c_im[d][:, None] * pi[d][:, pow_w[d]][:, :, None, :] for d in range(2)])
    zi = jnp.stack([c_re[d][:, None] * pi[d][:, pow_w[d]][:, :, None, :]
                    + c_im[d][:, None] * pr[d][:, pow_w[d]][:, :, None, :] for d in range(2)])
    w_sel = jnp.stack([zr, -zi], axis=2)
    w_sel = w_sel.reshape(2, ns, gs, 2, t, c, p).transpose(0, 1, 3, 2, 6, 4, 5)
    w_op = block_diag(w_sel.reshape(2, ns, 2 * gs * p, t * c), state_group, in_group, out_col)
    qr, qi = powers(t * (jnp.arange(SUBLANES) + 1))
    lanes = lambda z: z.reshape(2, ns, gs, SUBLANES, p).transpose(0, 1, 3, 2, 4).reshape(2, ns, SUBLANES, gs * p)
    pow_a = jnp.concatenate([lanes(qr), lanes(qr)], axis=-1)
    pow_b = jnp.concatenate([-lanes(qi), lanes(qi)], axis=-1)
    return toep, s_op, w_op, pow_a, pow_b


def _s5_state_kernel(u_ref, s_ref, o_ref):
    o_ref[...] = jnp.dot(u_ref[...], s_ref[...], preferred_element_type=F32)


def _s5_recur_slab_kernel(st_ref, pa_ref, pb_ref, h0_ref, hin_ref, fin_ref, loc_ref):
    nj, w = loc_ref.shape
    half = w // 2
    ng = nj // SUBLANES
    swap = lambda z: jnp.concatenate([z[..., half:], z[..., :half]], axis=-1)
    for d in range(2):
        reverse = d == 1
        pa = pa_ref[d]
        pb = pb_ref[d]
        z = st_ref[d].reshape(ng, SUBLANES, w)
        pos = lax.broadcasted_iota(jnp.int32, z.shape, 1)
        s = 1
        while s < SUBLANES:
            ok = pos < SUBLANES - s if reverse else pos >= s
            zs = pltpu.roll(z, SUBLANES - s if reverse else s, axis=1)
            z = z + jnp.where(ok, zs * pa[s - 1:s, :] + swap(zs) * pb[s - 1:s, :], 0.0)
            s *= 2
        loc_ref[...] = z.reshape(nj, w)
        row = lax.broadcasted_iota(jnp.int32, (SUBLANES, w), 0)
        if reverse:
            ca = jnp.concatenate([pa[SUBLANES - 1 - r:SUBLANES - r, :] for r in range(SUBLANES)], axis=0)
            cb = jnp.concatenate([pb[SUBLANES - 1 - r:SUBLANES - r, :] for r in range(SUBLANES)], axis=0)
        else:
            ca, cb = pa, pb

        def group(i, carry, d=d, reverse=reverse, ca=ca, cb=cb):
            c, c_sw = carry
            r0 = pl.multiple_of((ng - 1 - i if reverse else i) * SUBLANES, SUBLANES)
            h = loc_ref[pl.ds(r0, SUBLANES), :] + c * ca + c_sw * cb
            shifted = pltpu.roll(h, SUBLANES - 1 if reverse else 1, axis=0)
            hin_ref[d, pl.ds(r0, SUBLANES), :] = jnp.where(row == (SUBLANES - 1 if reverse else 0), c, shifted)
            last = h[0:1, :] if reverse else h[SUBLANES - 1:SUBLANES, :]
            return last, swap(last)

        h0 = h0_ref[d]
        fin_ref[d] = lax.fori_loop(0, ng, group, (h0, swap(h0)))[0]


def _s5_out_kernel(u_ref, toep_ref, hin_ref, w_ref, o_ref):
    acc = jnp.dot(u_ref[...], toep_ref[...], preferred_element_type=F32)
    for d in range(2):
        acc = acc + jnp.dot(hin_ref[d].astype(BF16), w_ref[d], preferred_element_type=F32)
    o_ref[...] = acc


def s5_slab_scan(u16, h0, ops, batch, seq):
    toep, s_op, w_op, pow_a, pow_b = ops
    ns, n, _ = u16.shape
    t = SLAB_CHUNK
    kc = t * LANES
    w = s_op.shape[-1]
    nj = seq // t
    rows = n // t
    tr = min(rows, 1024)
    uc = u16.reshape(ns, rows, kc)
    st = pl.pallas_call(
        _s5_state_kernel,
        out_shape=jax.ShapeDtypeStruct((2, ns, rows, w), F32),
        grid=(ns, 2, rows // tr),
        in_specs=[pl.BlockSpec((None, tr, kc), lambda a, d, r: (a, r, 0)),
                  pl.BlockSpec((None, None, kc, w), lambda a, d, r: (d, a, 0, 0))],
        out_specs=pl.BlockSpec((None, None, tr, w), lambda a, d, r: (d, a, r, 0)),
        compiler_params=_cparams("parallel", "parallel", "parallel"),
        name="s5_state",
    )(uc, s_op)
    hin, fin = pl.pallas_call(
        _s5_recur_slab_kernel,
        out_shape=(jax.ShapeDtypeStruct((2, ns, rows, w), F32), jax.ShapeDtypeStruct((2, ns, batch, 1, w), F32)),
        grid=(ns, batch),
        in_specs=[pl.BlockSpec((2, None, nj, w), lambda a, b: (0, a, b, 0)),
                  pl.BlockSpec((2, None, SUBLANES, w), lambda a, b: (0, a, 0, 0)),
                  pl.BlockSpec((2, None, SUBLANES, w), lambda a, b: (0, a, 0, 0)),
                  pl.BlockSpec((2, None, None, 1, w), lambda a, b: (0, a, b, 0, 0))],
        out_specs=(pl.BlockSpec((2, None, nj, w), lambda a, b: (0, a, b, 0)),
                   pl.BlockSpec((2, None, None, 1, w), lambda a, b: (0, a, b, 0, 0))),
        scratch_shapes=[pltpu.VMEM((nj, w), F32)],
        compiler_params=_cparams("parallel", "parallel"),
        name="s5_recur",
    )(st, pow_a, pow_b, h0)
    y = pl.pallas_call(
        _s5_out_kernel,
        out_shape=jax.ShapeDtypeStruct((ns, rows, kc), F32),
        grid=(ns, rows // tr),
        in_specs=[pl.BlockSpec((None, tr, kc), lambda a, r: (a, r, 0)),
                  pl.BlockSpec((None, kc, kc), lambda a, r: (a, 0, 0)),
                  pl.BlockSpec((2, None, tr, w), lambda a, r: (0, a, r, 0)),
                  pl.BlockSpec((2, None, w, kc), lambda a, r: (0, a, 0, 0))],
        out_specs=pl.BlockSpec((None, tr, kc), lambda a, r: (a, r, 0)),
        compiler_params=_cparams("parallel", "parallel"),
        name="s5_out_chunks",
    )(uc, toep, hin, w_op)
    return y.reshape(ns, n, LANES), fin


def _slab_epilogue(acc):
    parts = jnp.stack([acc[:, s * LANES:(s + 1) * LANES] for s in range(acc.shape[1] // LANES)])
    return parts, parts


def _s5_post_slab_kernel(u_ref, y_ref, d_ref, o_ref):
    for s in range(u_ref.shape[0]):
        cols = slice(s * LANES, (s + 1) * LANES)
        o_ref[:, cols] = jax.nn.gelu(d_ref[:, cols] * u_ref[s] + y_ref[s])


def s5_post_slab(u, y, d_skip, *, tm=512):
    ns, n, _ = u.shape
    w = ns * LANES
    slab = pl.BlockSpec((ns, tm, LANES), lambda i: (0, i, 0))
    return pl.pallas_call(
        _s5_post_slab_kernel,
        out_shape=jax.ShapeDtypeStruct((n, w), F32),
        grid=(n // tm,),
        in_specs=[slab, slab, pl.BlockSpec((1, w), lambda i: (0, 0))],
        out_specs=pl.BlockSpec((tm, w), lambda i: (i, 0)),
        compiler_params=_cparams("parallel"),
        name="s5_post",
    )(u, y, d_skip.reshape(1, w).astype(F32))


def s5_slab_mixer(h, x, mods, layer, stream, state, w_in, ops, d_skip, w_glu, b_glu, w_out):
    u, u16 = matmul(h, w_in, out_dtype=(F32, BF16), epilogue=_slab_epilogue, slab_out=True, name="s5_in")
    y, fin = s5_slab_scan(u16, state, ops, stream.batch, stream.seq)
    z = s5_post_slab(u, y, d_skip)
    w = z.shape[1]
    zz = matmul(z, w_glu, out_dtype=BF16, name="s5_glu", epilogue=_glu_epilogue,
                extras=[(z, (min(MM_TM, z.shape[0]), MM_TN), lambda i, j: (i, j)),
                        (b_glu.reshape(1, w).astype(F32), (1, MM_TN), lambda i, j: (0, j))])
    return out_proj_residual(zz, w_out, x, mods, layer, stream, name="s5_out"), fin


def s5_pack_state(state, slab_groups):
    b, _, _, g, p = state.shape
    s = state.astype(F32).reshape(b, 2, 2, g // slab_groups, slab_groups * p)
    return s.transpose(1, 3, 0, 2, 4).reshape(2, g // slab_groups, b, 1, 2 * slab_groups * p)


def s5_unpack_state(fin, slab_groups, p):
    _, ns, b, _, w = fin.shape
    s = fin.reshape(2, ns, b, 2, slab_groups * p).transpose(2, 0, 3, 1, 4)
    return s.reshape(b, 2, 2, ns * slab_groups, p)


RG_C = 8.0


def _rg_core_kernel(x_ref, gate_ref, cw_ref, cb_ref, w_ref, bias_ref, lam_ref, h0_ref, o_ref, fin_ref,
                    y_ref, xc_ref, a_ref, b_ref, *, seq):
    tb = a_ref.shape[0]
    tc = a_ref.shape[1]
    n_blocks = seq // tb
    n_groups = tb // SUBLANES
    cw = cw_ref[...]
    cb = cb_ref[...]
    for d in range(2):
        reverse = d == 1
        sp = _softplus(-lam_ref[d:d + 1, :])
        bias_a = bias_ref[d, 0:1, :]
        bias_x = bias_ref[d, 1:2, :]

        def block(k, carry, d=d, reverse=reverse, sp=sp, bias_a=bias_a, bias_x=bias_x):
            t0 = pl.multiple_of((n_blocks - 1 - k if reverse else k) * tb, tb)
            if reverse:
                xc = xc_ref[pl.ds(t0, tb), :]
            else:
                xc = _conv_rows(x_ref, t0, tb, seq, cw, cb)
                xc_ref[pl.ds(t0, tb), :] = xc
            pre = [_bdot(xc[:, n * LANES:(n + 1) * LANES], w_ref[d, n]) for n in range(tc // LANES)]
            r = jax.nn.sigmoid(jnp.concatenate([p[:, :LANES] for p in pre], axis=1) + bias_a)
            ig = jax.nn.sigmoid(jnp.concatenate([p[:, LANES:] for p in pre], axis=1) + bias_x)
            log_a = -RG_C * r * sp
            a = jnp.exp(log_a)
            bv = jnp.sqrt(-jnp.tanh(log_a) * (a * a + 1.0)) * (ig * xc)
            a_ref[...], b_ref[...] = _group_scan(a, bv, reverse)

            def group(i, c):
                r0 = pl.multiple_of((n_groups - 1 - i if reverse else i) * SUBLANES, SUBLANES)
                h = b_ref[pl.ds(r0, SUBLANES), :] + a_ref[pl.ds(r0, SUBLANES), :] * c
                rows = pl.ds(pl.multiple_of(t0 + r0, SUBLANES), SUBLANES)
                if reverse:
                    y_ref[rows, :] += h
                    return h[0:1, :]
                y_ref[rows, :] = h
                return h[SUBLANES - 1:SUBLANES, :]

            carry = lax.fori_loop(0, n_groups, group, carry)
            if reverse:
                o_ref[pl.ds(t0, tb), :] = (y_ref[pl.ds(t0, tb), :] * gate_ref[pl.ds(t0, tb), :]).astype(o_ref.dtype)
            return carry

        fin_ref[d:d + 1, :] = lax.fori_loop(0, n_blocks, block, h0_ref[d:d + 1, :])


def rg_core(xin, gate, conv_w, conv_b, w_gates, bias, lam, h0, stream, *, tc=256):
    n, c = xin.shape
    seq, batch = stream.seq, stream.batch
    tb = min(SCAN_BLOCK, seq)
    nb = tc // LANES
    return pl.pallas_call(
        functools.partial(_rg_core_kernel, seq=seq),
        out_shape=(jax.ShapeDtypeStruct((n, c), BF16), jax.ShapeDtypeStruct((batch, 2, c), F32)),
        grid=(batch, c // tc),
        in_specs=[pl.BlockSpec((seq, tc), lambda b, j: (b, j)),
                  pl.BlockSpec((seq, tc), lambda b, j: (b, j)),
                  pl.BlockSpec((4, tc), lambda b, j: (0, j)),
                  pl.BlockSpec((1, tc), lambda b, j: (0, j)),
                  pl.BlockSpec((2, nb, LANES, 2 * LANES), lambda b, j: (0, j, 0, 0)),
                  pl.BlockSpec((2, 2, tc), lambda b, j: (0, 0, j)),
                  pl.BlockSpec((2, tc), lambda b, j: (0, j)),
                  pl.BlockSpec((None, 2, tc), lambda b, j: (b, 0, j))],
        out_specs=(pl.BlockSpec((seq, tc), lambda b, j: (b, j)),
                   pl.BlockSpec((None, 2, tc), lambda b, j: (b, 0, j))),
        scratch_shapes=[pltpu.VMEM((seq, tc), F32), pltpu.VMEM((seq, tc), F32),
                        pltpu.VMEM((tb, tc), F32), pltpu.VMEM((tb, tc), F32)],
        compiler_params=_cparams("parallel", "parallel"),
        name="rg_core",
    )(xin, gate, conv_w, conv_b.reshape(1, c), w_gates, bias, lam, h0)


def _gelu_epilogue(acc):
    return jax.nn.gelu(acc)


def rg_mixer(h, x, mods, layer, stream, state, w_in, w_gate, conv_w, conv_b, w_gates, bias, lam, w_out):
    xin = matmul(h, w_in, name="rg_in")
    gate = matmul(h, w_gate, epilogue=_gelu_epilogue, name="rg_gate")
    yg, fin = rg_core(xin, gate, conv_w, conv_b, w_gates, bias, lam, state, stream)
    return out_proj_residual(yg, w_out, x, mods, layer, stream, name="rg_out"), fin


GRID_W = 64
ROPE_BASE = 10000.0
ATTN_TQ = 512
ATTN_TK = 512


def rope_tables(seq, dk):
    half = dk // 2
    quarter = half // 2
    pos = jnp.arange(seq)
    inv = ROPE_BASE ** (-jnp.arange(0, half, 2, dtype=F32) / half)
    ang_r = (pos // GRID_W).astype(F32)[:, None] * inv
    ang_c = (pos % GRID_W).astype(F32)[:, None] * inv
    cos = jnp.concatenate([jnp.cos(ang_r)] * 2 + [jnp.cos(ang_c)] * 2, axis=1)
    sin = jnp.concatenate([-jnp.sin(ang_r), jnp.sin(ang_r), -jnp.sin(ang_c), jnp.sin(ang_c)], axis=1)
    assert cos.shape == (seq, 4 * quarter)
    return cos, sin


def _rope_epilogue(acc, cos_ref, sin_ref):
    cos = cos_ref[...]
    sin = sin_ref[...]
    dk = cos.shape[1]
    lane = lax.broadcasted_iota(jnp.int32, cos.shape, 1)
    first = (lane % (dk // 2)) < dk // 4
    outs = []
    for g in range(acc.shape[1] // dk):
        x = acc[:, g * dk:(g + 1) * dk]
        rot = jnp.where(first, pltpu.roll(x, dk - dk // 4, axis=1), pltpu.roll(x, dk // 4, axis=1))
        outs.append(x * cos + rot * sin)
    return jnp.concatenate(outs, axis=1)


def _attn_kernel(*refs, n_tiles, tk, has_cache, lam_init, scale):
    if has_cache:
        q_ref, k_ref, v_ref, ck_ref, cv_ref, lam_ref, g_ref, o_ref = refs[:8]
    else:
        q_ref, k_ref, v_ref, lam_ref, g_ref, o_ref = refs[:6]
    s_ref = refs[-2]
    sc_ref = refs[-1]
    q = q_ref[...]
    tq = q.shape[0]
    dk = q.shape[1] // 2
    dv = v_ref.shape[1]
    maps = range(2)

    def scores(kt):
        return [lax.dot_general(q[:, c * dk:(c + 1) * dk], kt[:, c * dk:(c + 1) * dk],
                                (((1,), (1,)), ((), ())), preferred_element_type=F32) for c in maps]

    m = [jnp.full((tq, 1), -jnp.inf, F32) for _ in maps]
    if has_cache:
        s = scores(ck_ref[...])
        for c in maps:
            sc_ref[c] = s[c]
            m[c] = jnp.maximum(m[c], jnp.max(s[c], axis=-1, keepdims=True))

    def max_pass(i, m):
        s = scores(k_ref[pl.ds(pl.multiple_of(i * tk, tk), tk), :])
        for c in maps:
            s_ref[c, i] = s[c]
        return tuple(jnp.maximum(m[c], jnp.max(s[c], axis=-1, keepdims=True)) for c in maps)

    m = lax.fori_loop(0, n_tiles, max_pass, tuple(m), unroll=2 if n_tiles % 2 == 0 else 1)

    c2 = scale * math.log2(math.e)

    def accumulate(s, vt, st):
        new = []
        for c in maps:
            l, acc = st[c]
            p = jnp.exp2((s[c] - m[c]) * c2)
            new.append((l + jnp.sum(p, axis=-1, keepdims=True),
                        acc + jnp.dot(p.astype(BF16), vt, preferred_element_type=F32)))
        return tuple(new)

    st = tuple((jnp.zeros((tq, 1), F32), jnp.zeros((tq, dv), F32)) for _ in maps)
    if has_cache:
        st = accumulate([sc_ref[c] for c in maps], cv_ref[...], st)

    def sum_pass(i, st):
        return accumulate([s_ref[c, i] for c in maps], v_ref[pl.ds(pl.multiple_of(i * tk, tk), tk), :], st)

    st = lax.fori_loop(0, n_tiles, sum_pass, st)
    lp = lam_ref[...]
    lam = (jnp.exp(jnp.sum(lp[0:1, :] * lp[1:2, :], axis=-1, keepdims=True))
           - jnp.exp(jnp.sum(lp[2:3, :] * lp[3:4, :], axis=-1, keepdims=True)) + lam_init)
    diff = st[0][1] / st[0][0] - lam * (st[1][1] / st[1][0])
    diff = diff * lax.rsqrt(jnp.mean(diff * diff, axis=-1, keepdims=True) + NORM_EPS)
    o_ref[...] = (diff * g_ref[...] * (1.0 - lam_init)).astype(o_ref.dtype)


def diff_attention(q, k, v, *rest, stream, lam_init):
    has_cache = len(rest) == 4
    ck, cv = (rest[0], rest[1]) if has_cache else (None, None)
    da_lam, subln_g = rest[-2], rest[-1]
    n, width = q.shape
    dv = subln_g.shape[0]
    heads = width // dv
    dk = dv // 2
    batch, seq = stream.batch, stream.seq
    tq = min(ATTN_TQ, seq)
    tk = min(ATTN_TK, seq)
    nq = seq // tq
    in_specs = [pl.BlockSpec((tq, dv), lambda b, h, i: (b * nq + i, h)),
                pl.BlockSpec((seq, dv), lambda b, h, i: (b, h)),
                pl.BlockSpec((seq, dv), lambda b, h, i: (b, h))]
    args = [q, k, v]
    if has_cache:
        past = ck.shape[0] // batch
        in_specs += [pl.BlockSpec((past, dv), lambda b, h, i: (b, h))] * 2
        args += [ck, cv]
    in_specs += [pl.BlockSpec(da_lam.shape, lambda b, h, i: (0, 0)), pl.BlockSpec((1, dv), lambda b, h, i: (0, 0))]
    args += [da_lam.astype(F32), subln_g.reshape(1, dv).astype(F32)]
    return pl.pallas_call(
        functools.partial(_attn_kernel, n_tiles=seq // tk, tk=tk, has_cache=has_cache, lam_init=lam_init,
                          scale=1.0 / math.sqrt(dk)),
        out_shape=jax.ShapeDtypeStruct((n, width), BF16),
        grid=(batch, heads, nq),
        in_specs=in_specs,
        out_specs=pl.BlockSpec((tq, dv), lambda b, h, i: (b * nq + i, h)),
        scratch_shapes=[pltpu.VMEM((2, seq // tk, tq, tk), F32),
                        pltpu.VMEM((2, tq, past if has_cache else SUBLANES), F32)],
        compiler_params=_cparams("parallel", "parallel", "parallel"),
        name="diff_attention",
    )(*args)


def _attn_t_kernel(*refs, n_tiles, has_cache, lam_init, scale):
    if has_cache:
        qt_ref, k_ref, vt_ref, ck_ref, cvt_ref, lam_ref, g_ref, o_ref, s_ref, sc_ref = refs
    else:
        qt_ref, k_ref, vt_ref, lam_ref, g_ref, o_ref, s_ref = refs
    tq = qt_ref.shape[1]
    dk = qt_ref.shape[0] // 2
    dv = vt_ref.shape[0]
    tk = s_ref.shape[2]
    maps = range(2)
    qt = [qt_ref[c * dk:(c + 1) * dk, :] for c in maps]

    def scores(kt):
        return [jnp.dot(kt[:, c * dk:(c + 1) * dk], qt[c], preferred_element_type=F32) for c in maps]

    m = [jnp.full((1, tq), -jnp.inf, F32) for _ in maps]
    if has_cache:
        s = scores(ck_ref[...])
        for c in maps:
            sc_ref[c] = s[c]
            m[c] = jnp.maximum(m[c], jnp.max(s[c], axis=0, keepdims=True))

    for i in range(n_tiles):
        s = scores(k_ref[i * tk:(i + 1) * tk, :])
        for c in maps:
            s_ref[c, i] = s[c]
            m[c] = jnp.maximum(m[c], jnp.max(s[c], axis=0, keepdims=True))

    c2 = scale * math.log2(math.e)

    def accumulate(s, vt, st):
        new = []
        for c in maps:
            l, acc = st[c]
            p = jnp.exp2((s[c] - m[c]) * c2)
            new.append((l + jnp.sum(p, axis=0, keepdims=True),
                        acc + jnp.dot(vt, p.astype(BF16), preferred_element_type=F32)))
        return tuple(new)

    st = tuple((jnp.zeros((1, tq), F32), jnp.zeros((dv, tq), F32)) for _ in maps)
    if has_cache:
        st = accumulate([sc_ref[c] for c in maps], cvt_ref[...], st)

    for i in range(n_tiles):
        st = accumulate([s_ref[c, i] for c in maps], vt_ref[:, i * tk:(i + 1) * tk], st)
    lp = lam_ref[...]
    lam = (jnp.exp(jnp.sum(lp[0:1, :] * lp[1:2, :], axis=-1, keepdims=True))
           - jnp.exp(jnp.sum(lp[2:3, :] * lp[3:4, :], axis=-1, keepdims=True)) + lam_init)
    diff = st[0][1] / st[0][0] - lam * (st[1][1] / st[1][0])
    diff = diff * lax.rsqrt(jnp.mean(diff * diff, axis=0, keepdims=True) + NORM_EPS)
    o_ref[...] = (diff.T * g_ref[...] * (1.0 - lam_init)).astype(o_ref.dtype)


def diff_attention_t(q_t, k, v_t, *rest, stream, lam_init):
    has_cache = len(rest) == 4
    da_lam, subln_g = rest[-2], rest[-1]
    n = k.shape[0]
    dv = subln_g.shape[0]
    heads = v_t.shape[0] // dv
    batch, seq = stream.batch, stream.seq
    tq = min(ATTN_TQ, seq)
    tk = min(ATTN_TK, seq)
    n_tiles = seq // tk
    nq = seq // tq
    in_specs = [pl.BlockSpec((dv, tq), lambda b, h, i: (h, b * nq + i)),
                pl.BlockSpec((seq, dv), lambda b, h, i: (b, h)),
                pl.BlockSpec((dv, seq), lambda b, h, i: (h, b))]
    args = [q_t, k, v_t]
    scratch = [pltpu.VMEM((2, n_tiles, tk, tq), F32)]
    if has_cache:
        ck, cv_t = rest[0], rest[1]
        past = cv_t.shape[3]
        in_specs += [pl.BlockSpec((past, dv), lambda b, h, i: (b, h)),
                     pl.BlockSpec((None, None, dv, past), lambda b, h, i: (b, h, 0, 0))]
        args += [ck, cv_t]
        scratch += [pltpu.VMEM((2, past, tq), F32)]
    in_specs += [pl.BlockSpec(da_lam.shape, lambda b, h, i: (0, 0)), pl.BlockSpec((1, dv), lambda b, h, i: (0, 0))]
    args += [da_lam.astype(F32), subln_g.reshape(1, dv).astype(F32)]
    return pl.pallas_call(
        functools.partial(_attn_t_kernel, n_tiles=n_tiles, has_cache=has_cache, lam_init=lam_init,
                          scale=1.0 / math.sqrt(dv // 2)),
        out_shape=jax.ShapeDtypeStruct((n, heads * dv), BF16),
        grid=(batch, heads, nq),
        in_specs=in_specs,
        out_specs=pl.BlockSpec((tq, dv), lambda b, h, i: (b * nq + i, h)),
        scratch_shapes=scratch,
        compiler_params=_cparams("parallel", "parallel", "parallel"),
        name="diff_attention",
    )(*args)


def _rope_t_epilogue(acc, cos_ref, sin_ref):
    cos = cos_ref[...]
    sin = sin_ref[...]
    dk = cos.shape[0]
    row = lax.broadcasted_iota(jnp.int32, cos.shape, 0)
    first = (row % (dk // 2)) < dk // 4
    outs = []
    for g in range(acc.shape[0] // dk):
        x = acc[g * dk:(g + 1) * dk, :]
        rot = jnp.where(first, pltpu.roll(x, dk - dk // 4, axis=0), pltpu.roll(x, dk // 4, axis=0))
        outs.append(x * cos + rot * sin)
    return jnp.concatenate(outs, axis=0)


def _dup_epilogue(acc):
    return acc, acc


def dattn_mixer(h, x, mods, layer, stream, cache, wq_t, wk, wv, wv_t, wo, da_lam, subln_g, lam_init):
    dk = subln_g.shape[0] // 2
    v_t = matmul_t(h, wv_t, name="da_v_t")
    if cache is None:
        q_t = matmul_t(h, wq_t, name="da_q_t")
        k32, k = matmul(h, wk, out_dtype=(F32, BF16), epilogue=_dup_epilogue, name="da_k")
        v32 = matmul(h, wv, name="da_v")
        o = diff_attention_t(q_t, k, v_t, da_lam, subln_g, stream=stream, lam_init=lam_init)
        new = (k32, v32)
    else:
        cos, sin = rope_tables(stream.seq, dk)
        tm = min(MM_TM, stream.seq)
        per = stream.seq // tm
        rope = [(cos, (tm, dk), lambda i, j: (i % per, 0)), (sin, (tm, dk), lambda i, j: (i % per, 0))]
        rope_t = [(cos.T, (dk, tm), lambda i, j: (0, i % per)), (sin.T, (dk, tm), lambda i, j: (0, i % per))]
        q_t = matmul_t(h, wq_t, tm=tm, epilogue=_rope_t_epilogue, extras=rope_t, name="da_q_t_rope")
        k = matmul(h, wk, tm=tm, out_dtype=BF16, epilogue=_rope_epilogue, extras=rope, name="da_k_rope")
        o = diff_attention_t(q_t, k, v_t, cache[0], cache[1], da_lam, subln_g, stream=stream, lam_init=lam_init)
        new = None
    return out_proj_residual(o, wo, x, mods, layer, stream, name="da_out"), new


ML_CHUNK = 256


def _ml_conv_kernel(x_ref, cw_ref, cb_ref, o32_ref, o16_ref, *, seq):
    tb = min(SCAN_BLOCK, seq)
    cw = cw_ref[...]
    cb = cb_ref[...]

    def body(i, carry):
        t0 = pl.multiple_of(i * tb, tb)
        y = _conv_rows(x_ref, t0, tb, seq, cw, cb)
        y = y * jax.nn.sigmoid(y)
        o32_ref[pl.ds(t0, tb), :] = y
        o16_ref[pl.ds(t0, tb), :] = y.astype(BF16)
        return carry

    lax.fori_loop(0, seq // tb, body, 0)


def ml_conv(xi, conv_w, conv_b, stream):
    n, w = xi.shape
    seq = stream.seq
    tc = min(w, 1024, SEQ_BLOCK_ELEMS // seq)
    return pl.pallas_call(
        functools.partial(_ml_conv_kernel, seq=seq),
        out_shape=(jax.ShapeDtypeStruct((n, w), F32), jax.ShapeDtypeStruct((n, w), BF16)),
        grid=(stream.batch, w // tc),
        in_specs=[pl.BlockSpec((seq, tc), lambda b, j: (b, j)),
                  pl.BlockSpec((conv_w.shape[0], tc), lambda b, j: (0, j)),
                  pl.BlockSpec((1, tc), lambda b, j: (0, j))],
        out_specs=(pl.BlockSpec((seq, tc), lambda b, j: (b, j)), pl.BlockSpec((seq, tc), lambda b, j: (b, j))),
        compiler_params=_cparams("parallel", "parallel"),
        name="ml_conv",
    )(xi, conv_w.astype(F32), conv_b.reshape(1, w).astype(F32))


def _ml_gates_kernel(q_ref, k_ref, v_ref, w_ref, b_ref, o_ref, *, n_heads):
    acc = jnp.dot(q_ref[...], w_ref[0], preferred_element_type=F32)
    acc = acc + jnp.dot(k_ref[...], w_ref[1], preferred_element_type=F32)
    acc = acc + jnp.dot(v_ref[...], w_ref[2], preferred_element_type=F32) + b_ref[...]
    col = lax.broadcasted_iota(jnp.int32, acc.shape, 1)
    is_forget = (col // n_heads) % 2 == 1
    o_ref[...] = jnp.where(is_forget, -_softplus(-acc), acc)


def ml_gates(q, k, v, w_gate, b_gate, n_heads, *, tm=512):
    n, w = q.shape
    ng = w_gate.shape[2]
    row = pl.BlockSpec((tm, w), lambda i: (i, 0))
    return pl.pallas_call(
        functools.partial(_ml_gates_kernel, n_heads=n_heads),
        out_shape=jax.ShapeDtypeStruct((n, ng), F32),
        grid=(n // tm,),
        in_specs=[row, row, row, pl.BlockSpec((3, w, ng), lambda i: (0, 0, 0)), pl.BlockSpec((1, ng), lambda i: (0, 0))],
        out_specs=pl.BlockSpec((tm, ng), lambda i: (i, 0)),
        compiler_params=_cparams("parallel"),
        name="ml_gates",
    )(q, k, v, w_gate, b_gate.reshape(1, ng).astype(F32))


ML_HEADS_PER_STEP = 2


def _mlstm_cell_kernel(*refs, n_heads, has_init, want_final):
    q_ref, k_ref, kt_ref, v_ref, g_ref, gt_ref = refs[:6]
    init_refs = refs[6:9] if has_init else None
    h_ref = refs[9 if has_init else 6]
    ct_s, n_s, m_s = refs[-3:]
    final_refs = refs[-6:-3] if want_final else None
    hb = ct_s.shape[0]
    dh = ct_s.shape[1]
    h0 = pl.program_id(1) * hb
    d = pl.program_id(2)
    ci = pl.program_id(3)
    reverse = d == 1
    t = q_ref.shape[0]

    @pl.when(ci == 0)
    def _():
        for j in range(hb):
            if has_init:
                ct_s[j] = init_refs[0][j].T
                n_s[j] = init_refs[1][j]
                m_s[j] = init_refs[2][j]
            else:
                ct_s[j] = jnp.zeros((dh, dh), F32)
                n_s[j] = jnp.zeros((1, dh), F32)
                m_s[j] = jnp.zeros((1, 1), F32)

    g = g_ref[...]
    gt = gt_ref[...]
    lane = lax.broadcasted_iota(jnp.int32, g.shape, 1)
    sub = lax.broadcasted_iota(jnp.int32, gt.shape, 0)
    r_idx = lax.broadcasted_iota(jnp.int32, (t, t), 0)
    s_idx = lax.broadcasted_iota(jnp.int32, (t, t), 1)
    ahead = (s_idx - r_idx) * jnp.where(reverse, -1, 1)
    mask = ahead <= 0
    mask_t = ahead >= 0
    row = lax.broadcasted_iota(jnp.int32, (t, 1), 0)
    last = row == jnp.where(reverse, 0, t - 1)

    for j in range(hb):
        cols = slice(j * dh, (j + 1) * dh)
        q = q_ref[:, cols]
        k = k_ref[:, cols]
        v = v_ref[:, cols]
        kt = kt_ref[cols, :]
        col_i = d * (2 * n_heads) + h0 + j
        col_f = col_i + n_heads
        ig_col = jnp.sum(jnp.where(lane == col_i, g, 0.0), axis=1, keepdims=True)
        lf_col = jnp.sum(jnp.where(lane == col_f, g, 0.0), axis=1, keepdims=True)
        ig_row = jnp.sum(jnp.where(sub == col_i, gt, 0.0), axis=0, keepdims=True)
        lf_row = jnp.sum(jnp.where(sub == col_f, gt, 0.0), axis=0, keepdims=True)
        b_col = jnp.sum(jnp.where(mask, lf_row, 0.0), axis=1, keepdims=True)
        b_row = jnp.sum(jnp.where(mask_t, lf_col, 0.0), axis=0, keepdims=True)
        m_prev = m_s[j]
        dlog = jnp.where(mask, b_col - b_row + ig_row, -jnp.inf)
        inter = b_col + m_prev
        m_t = jnp.maximum(inter, jnp.max(dlog, axis=1, keepdims=True))
        w_intra = jnp.exp(dlog - m_t)
        w_inter = jnp.exp(inter - m_t)
        s_mat = lax.dot_general(q, k, (((1,), (1,)), ((), ())), preferred_element_type=F32) * w_intra
        ct_old = ct_s[j]
        n_old = n_s[j]
        qc = jnp.dot(q, ct_old.astype(BF16), preferred_element_type=F32)
        num = jnp.dot(s_mat.astype(BF16), v, preferred_element_type=F32) + w_inter * qc
        n_rows = jnp.broadcast_to(n_old.astype(BF16), (SUBLANES, dh))
        qn = lax.dot_general(q, n_rows, (((1,), (1,)), ((), ())), preferred_element_type=F32)[:, 0:1]
        den = jnp.sum(s_mat, axis=1, keepdims=True) + w_inter * qn
        h_ref[:, cols] = num / jnp.maximum(jnp.abs(den), jnp.exp(-m_t))
        m_new = jnp.sum(jnp.where(last, m_t, 0.0), axis=0, keepdims=True)
        b_last = jnp.sum(jnp.where(last, b_col, 0.0), axis=0, keepdims=True)
        w_old = jnp.exp(b_last + m_prev - m_new)
        w_in = jnp.exp(b_last - b_col + ig_col - m_new)
        vw = (v.astype(F32) * w_in).astype(BF16)
        ct_s[j] = w_old * ct_old + jnp.dot(kt, vw, preferred_element_type=F32)
        n_s[j] = w_old * n_old + jnp.sum(w_in * k.astype(F32), axis=0, keepdims=True)
        m_s[j] = m_new

    if want_final:
        @pl.when(ci == pl.num_programs(3) - 1)
        def _():
            for j in range(hb):
                final_refs[0][j] = ct_s[j].T
                final_refs[1][j] = n_s[j]
                final_refs[2][j] = m_s[j]


def mlstm_cell(q, k, k_t, v, g, g_t, init, stream, heads, want_final):
    n, w = q.shape
    batch, seq = stream.batch, stream.seq
    dh = w // heads
    hb = ML_HEADS_PER_STEP
    t = min(ML_CHUNK, seq)
    nc = seq // t

    def blk(b, d, ci):
        return b * nc + ci + d * (nc - 1 - 2 * ci)

    tok = pl.BlockSpec((t, hb * dh), lambda b, h, d, ci: (blk(b, d, ci), h))
    st5 = lambda r, c: pl.BlockSpec((None, None, hb, r, c), lambda b, h, d, ci: (b, d, h, 0, 0))
    state_dims = ((dh, dh), (1, dh), (1, 1))
    out_shape = [jax.ShapeDtypeStruct((2, n, w), F32)]
    out_specs = [pl.BlockSpec((None, t, hb * dh), lambda b, h, d, ci: (d, blk(b, d, ci), h))]
    if want_final:
        out_shape += [jax.ShapeDtypeStruct((batch, 2, heads) + rc, F32) for rc in state_dims]
        out_specs += [st5(*rc) for rc in state_dims]
    res = pl.pallas_call(
        functools.partial(_mlstm_cell_kernel, n_heads=heads, has_init=init is not None, want_final=want_final),
        out_shape=tuple(out_shape),
        grid=(batch, heads // hb, 2, nc),
        in_specs=[tok, tok,
                  pl.BlockSpec((hb * dh, t), lambda b, h, d, ci: (h, blk(b, d, ci))),
                  tok,
                  pl.BlockSpec((t, g.shape[1]), lambda b, h, d, ci: (blk(b, d, ci), 0)),
                  pl.BlockSpec((g.shape[1], t), lambda b, h, d, ci: (0, blk(b, d, ci)))]
        + ([st5(*rc) for rc in state_dims] if init is not None else []),
        out_specs=tuple(out_specs),
        scratch_shapes=[pltpu.VMEM((hb,) + rc, F32) for rc in state_dims],
        compiler_params=_cparams("parallel", "parallel", "arbitrary", "arbitrary"),
        name="mlstm_cell",
    )(q, k, k_t, v, g, g_t, *(init or ()))
    return res[0], (tuple(res[1:]) if want_final else None)


def _scale_epilogue(acc, *, scale):
    return acc * scale


def _ml_out_gate_epilogue(acc, b_ref, h_ref, xc_ref, gn_ref, skip_ref):
    hc = jax.nn.sigmoid(acc + b_ref[...]) * (h_ref[0] + h_ref[1])
    hn = hc * lax.rsqrt(jnp.mean(hc * hc, axis=-1, keepdims=True) + NORM_EPS)
    return hn * gn_ref[...] + skip_ref[...] * xc_ref[...]


def mlstm_mixer(h, x, mods, layer, stream, init, want_final, w_up, conv_w, conv_b, wq, wk, wk_t, wv, w_gate, b_gate,
                w_o, b_o, gn, skip, w_down):
    dh = wq.shape[1]
    xi = matmul(h, w_up, name="ml_up")
    xc, xc16 = ml_conv(xi, conv_w, conv_b, stream)
    q = matmul(xc16, wq, tm=MM_TM_DIAG, tn=dh, out_dtype=BF16, k_block=dh, name="ml_q")
    k = matmul(xc16, wk, tm=MM_TM_DIAG, tn=dh, out_dtype=BF16, k_block=dh, name="ml_k",
               epilogue=functools.partial(_scale_epilogue, scale=1.0 / math.sqrt(dh)))
    v = matmul(xi, wv, tm=MM_TM_DIAG, tn=dh, out_dtype=BF16, k_block=dh, name="ml_v")
    w = xi.shape[1]
    g = ml_gates(q, k, v, w_gate, b_gate, w // dh)
    k_t = matmul_t(xc16, wk_t, tm=MM_TM_DIAG, tn=dh, k_block=dh, name="ml_k_t",
                   epilogue=functools.partial(_scale_epilogue, scale=1.0 / math.sqrt(dh)))
    hdir, final = mlstm_cell(q, k, k_t, v, g, g.T, init, stream, w // dh, want_final)
    tm = min(MM_TM, h.shape[0])
    vec = lambda a: (a.reshape(1, w).astype(F32), (1, dh), lambda i, j: (0, j))
    y = matmul(h, w_o, tm=tm, tn=dh, out_dtype=BF16, name="ml_o", epilogue=_ml_out_gate_epilogue,
               extras=[vec(b_o), (hdir, (2, tm, dh), lambda i, j: (0, i, j)), (xc, (tm, dh), lambda i, j: (i, j)),
                       vec(gn), vec(skip)])
    return out_proj_residual(y, w_down, x, mods, layer, stream, name="ml_down"), final


def kernel(x_prompt, x_sample, state_s5, state_rglru, cache_dattn_k, cache_dattn_v, state_mlstm_C, state_mlstm_n, state_mlstm_m, c, c_ctx, ada_w, ada_b, norm_g, ffn1_w1, ffn1_w3, ffn1_w2, ffn2_w1, ffn2_w3, ffn2_w2, final_norm_g, s5_w_in, s5_a_re, s5_a_im, s5_log_dt, s5_b_re, s5_b_im, s5_c_re, s5_c_im, s5_d, s5_w_glu, s5_b_glu, s5_w_out, rg_w_in, rg_w_gate, rg_conv_w, rg_conv_b, rg_wa, rg_ba, rg_wx, rg_bx, rg_lam, rg_w_out, da_wq, da_wk, da_wv, da_wo, da_lam, da_subln_g, ml_w_up, ml_conv_w, ml_conv_b, ml_wq, ml_wk, ml_wv, ml_w_gate, ml_b_gate, ml_w_o, ml_b_o, ml_gn, ml_skip, ml_w_down):
    bc, lc, d = x_prompt.shape
    bl, ll, _ = x_sample.shape
    depth = ada_w.shape[0]
    cs = Stream(bc, lc, 0, False)
    ls = Stream(bl, ll, 1, True)
    ctx = x_prompt.reshape(cs.n, d)
    lat = x_sample.reshape(ls.n, d)
    c_all = jnp.concatenate([c_ctx[None, :], c, jnp.zeros((MOD_ROWS - 1 - bl, d), F32)], axis=0)
    mods = adaln(c_all, ada_w, ada_b)
    bf = lambda a: a.astype(BF16)
    ffn1 = (bf(ffn1_w1), bf(ffn1_w3), bf(ffn1_w2))
    ffn2 = (bf(ffn2_w1), bf(ffn2_w3), bf(ffn2_w2))

    new_s5 = new_rg = None
    for i in range(depth):
        ctx = ffn_half(ctx, norm_g[i, 0], mods, i, 0, *ffn1, cs)
        lat = ffn_half(lat, norm_g[i, 0], mods, i, 0, *ffn1, ls)
        hc = modnorm(ctx, norm_g[i, 1], mods, i, 3, cs)
        hl = modnorm(lat, norm_g[i, 1], mods, i, 3, ls)
        kind = i % 4
        if kind == 0:
            ops = _s5_slab_operators(s5_a_re, s5_a_im, s5_log_dt, s5_b_re, s5_b_im, s5_c_re, s5_c_im)
            p = s5_a_re.shape[2]
            slab_groups = LANES // s5_b_re.shape[3]
            args = (bf(s5_w_in), ops, s5_d, bf(s5_w_glu), s5_b_glu, bf(s5_w_out))
            zero_state = jnp.zeros((bc,) + state_s5.shape[1:], F32)
            ctx, fin = s5_slab_mixer(hc, ctx, mods, i, cs, s5_pack_state(zero_state, slab_groups), *args)
            lat, _ = s5_slab_mixer(hl, lat, mods, i, ls, s5_pack_state(state_s5, slab_groups), *args)
            new_s5 = s5_unpack_state(fin, slab_groups, p)
        elif kind == 1:
            w_gates = bf(jnp.concatenate([rg_wa, rg_wx], axis=-1))
            bias = jnp.stack([rg_ba, rg_bx], axis=1).astype(F32)
            args = (bf(rg_w_in), bf(rg_w_gate), rg_conv_w, rg_conv_b, w_gates, bias, rg_lam, bf(rg_w_out))
            ctx, new_rg = rg_mixer(hc, ctx, mods, i, cs, jnp.zeros((bc, 2, rg_lam.shape[1]), F32), *args)
            lat, _ = rg_mixer(hl, lat, mods, i, ls, state_rglru, *args)
        elif kind == 2:
            lam_init = 0.8 - 0.6 * math.exp(-0.3 * i)
            heads, dv = cache_dattn_v.shape[2], cache_dattn_v.shape[3]
            args = (bf(da_wq).T, bf(da_wk), bf(da_wv), bf(da_wv).T, bf(da_wo), da_lam, da_subln_g, lam_init)
            ctx, (k32, v32) = dattn_mixer(hc, ctx, mods, i, cs, None, *args)
            new_k = k32.reshape(bc, lc, heads, 2, dv // 2)
            new_v = v32.reshape(bc, lc, heads, dv)
            cache = (bf(cache_dattn_k).reshape(-1, heads * dv), bf(cache_dattn_v).transpose(0, 2, 3, 1))
            lat, _ = dattn_mixer(hl, lat, mods, i, ls, cache, *args)
        else:
            heads, dh = ml_wq.shape[0], ml_wq.shape[1]
            args = (bf(ml_w_up), ml_conv_w, ml_conv_b, bf(ml_wq).reshape(heads * dh, dh),
                    bf(ml_wk).reshape(heads * dh, dh), bf(ml_wk).transpose(0, 2, 1).reshape(heads * dh, dh),
                    bf(ml_wv).reshape(heads * dh, dh), bf(ml_w_gate), ml_b_gate,
                    bf(ml_w_o), ml_b_o, ml_gn, ml_skip, bf(ml_w_down))
            ctx, (cf, nf, mf) = mlstm_mixer(hc, ctx, mods, i, cs, None, True, *args)
            init = (state_mlstm_C.astype(F32), state_mlstm_n.astype(F32).reshape(bl, 2, heads, 1, dh),
                    state_mlstm_m.astype(F32).reshape(bl, 2, heads, 1, 1))
            lat, _ = mlstm_mixer(hl, lat, mods, i, ls, init, False, *args)
            new_c, new_n, new_m = cf, nf.reshape(bc, 2, heads, dh), mf.reshape(bc, 2, heads)
        ctx = ffn_half(ctx, norm_g[i, 2], mods, i, 6, *ffn2, cs)
        lat = ffn_half(lat, norm_g[i, 2], mods, i, 6, *ffn2, ls)
    y_prompt = rmsnorm(ctx, final_norm_g).reshape(bc, lc, d)
    y_sample = rmsnorm(lat, final_norm_g).reshape(bl, ll, d)
    return (y_prompt, y_sample, new_s5, new_rg, new_k, new_v, new_c, new_n, new_m)
```

```python
import functools
import math

import jax
import jax.numpy as jnp
from jax import lax
from jax.experimental import pallas as pl
from jax.experimental.pallas import tpu as pltpu

F32 = jnp.float32
BF16 = jnp.bfloat16

NORM_EPS = 1e-6
VMEM_LIMIT_BYTES = 56 * 1024 * 1024
MOD_ROWS = 8


def _cparams(*sem):
    return pltpu.CompilerParams(dimension_semantics=sem, vmem_limit_bytes=VMEM_LIMIT_BYTES)


def _bdot(a, b):
    return jnp.dot(a.astype(BF16), b.astype(BF16), preferred_element_type=F32)


def _mod_row(ref, row):
    return ref[pl.ds(row, 1), :]


class Stream:
    def __init__(self, batch, seq, row0, per_batch_mod):
        self.batch, self.seq, self.row0 = batch, seq, row0
        self.n = batch * seq
        self.rows_per_mod = seq if per_batch_mod else self.n

    def mod_row(self, first_token):
        return self.row0 + first_token // self.rows_per_mod


def _adaln_kernel(c_ref, w_ref, b_ref, o_ref):
    c = c_ref[...]
    s = c * jax.nn.sigmoid(c)
    o_ref[...] = _bdot(s, w_ref[...]) + b_ref[...]


def adaln(c_all, ada_w, ada_b, *, tn=1024):
    nl, d, m = ada_w.shape
    return pl.pallas_call(
        _adaln_kernel,
        out_shape=jax.ShapeDtypeStruct((nl, MOD_ROWS, m), F32),
        grid=(nl, m // tn),
        in_specs=[pl.BlockSpec((MOD_ROWS, d), lambda l, j: (0, 0)),
                  pl.BlockSpec((None, d, tn), lambda l, j: (l, 0, j)),
                  pl.BlockSpec((None, 1, tn), lambda l, j: (l, 0, j))],
        out_specs=pl.BlockSpec((None, MOD_ROWS, tn), lambda l, j: (l, 0, j)),
        compiler_params=_cparams("parallel", "parallel"),
        name="adaln",
    )(c_all, ada_w, ada_b.reshape(nl, 1, m))


def _mod_spec(d, layer, k):
    return pl.BlockSpec((None, MOD_ROWS, d), lambda *_: (layer, 0, k))


def _rms(x, g):
    return x * lax.rsqrt(jnp.mean(x * x, axis=-1, keepdims=True) + NORM_EPS) * g


def _modnorm_kernel(x_ref, g_ref, sh_ref, sc_ref, o_ref, *, stream, tm):
    row = stream.mod_row(pl.program_id(0) * tm)
    y = _rms(x_ref[...], g_ref[...])
    o_ref[...] = (y * (1.0 + _mod_row(sc_ref, row)) + _mod_row(sh_ref, row)).astype(o_ref.dtype)


def modnorm(x, g, mods, layer, k_shift, stream, *, tm=512):
    n, d = x.shape
    return pl.pallas_call(
        functools.partial(_modnorm_kernel, stream=stream, tm=tm),
        out_shape=jax.ShapeDtypeStruct((n, d), BF16),
        grid=(n // tm,),
        in_specs=[pl.BlockSpec((tm, d), lambda i: (i, 0)),
                  pl.BlockSpec((1, d), lambda i: (0, 0)),
                  _mod_spec(d, layer, k_shift),
                  _mod_spec(d, layer, k_shift + 1)],
        out_specs=pl.BlockSpec((tm, d), lambda i: (i, 0)),
        compiler_params=_cparams("parallel"),
        name="modnorm",
    )(x, g.reshape(1, d), mods, mods)


def _rmsnorm_kernel(x_ref, g_ref, o_ref):
    o_ref[...] = _rms(x_ref[...], g_ref[...])


def rmsnorm(x, g, *, tm=512):
    n, d = x.shape
    return pl.pallas_call(
        _rmsnorm_kernel,
        out_shape=jax.ShapeDtypeStruct((n, d), F32),
        grid=(n // tm,),
        in_specs=[pl.BlockSpec((tm, d), lambda i: (i, 0)), pl.BlockSpec((1, d), lambda i: (0, 0))],
        out_specs=pl.BlockSpec((tm, d), lambda i: (i, 0)),
        compiler_params=_cparams("parallel"),
        name="rmsnorm",
    )(x, g.reshape(1, d))


def _ffn_kernel(x_ref, g_ref, sh_ref, sc_ref, gt_ref, w1_ref, w3_ref, w2_ref, o_ref, h_ref, acc_ref,
                *, stream, tm):
    f = pl.program_id(1)
    row = stream.mod_row(pl.program_id(0) * tm)

    @pl.when(f == 0)
    def _():
        y = _rms(x_ref[...], g_ref[...])
        h_ref[...] = (y * (1.0 + _mod_row(sc_ref, row)) + _mod_row(sh_ref, row)).astype(BF16)
        acc_ref[...] = jnp.zeros_like(acc_ref)

    h = h_ref[...]
    a = jnp.dot(h, w1_ref[...], preferred_element_type=F32)
    b = jnp.dot(h, w3_ref[...], preferred_element_type=F32)
    u = (a * jax.nn.sigmoid(a)) * b
    acc_ref[...] += jnp.dot(u.astype(BF16), w2_ref[...], preferred_element_type=F32)

    @pl.when(f == pl.num_programs(1) - 1)
    def _():
        o_ref[...] = x_ref[...] + (0.5 * _mod_row(gt_ref, row)) * acc_ref[...]


def ffn_half(x, g, mods, layer, k_shift, w1, w3, w2, stream, *, tm=512, tf=512):
    n, d = x.shape
    dff = w1.shape[2]
    return pl.pallas_call(
        functools.partial(_ffn_kernel, stream=stream, tm=tm),
        out_shape=jax.ShapeDtypeStruct((n, d), F32),
        grid=(n // tm, dff // tf),
        in_specs=[pl.BlockSpec((tm, d), lambda i, f: (i, 0)),
                  pl.BlockSpec((1, d), lambda i, f: (0, 0)),
                  _mod_spec(d, layer, k_shift),
                  _mod_spec(d, layer, k_shift + 1),
                  _mod_spec(d, layer, k_shift + 2),
                  pl.BlockSpec((None, d, tf), lambda i, f: (layer, 0, f)),
                  pl.BlockSpec((None, d, tf), lambda i, f: (layer, 0, f)),
                  pl.BlockSpec((None, tf, d), lambda i, f: (layer, f, 0))],
        out_specs=pl.BlockSpec((tm, d), lambda i, f: (i, 0)),
        scratch_shapes=[pltpu.VMEM((tm, d), BF16), pltpu.VMEM((tm, d), F32)],
        compiler_params=_cparams("parallel", "arbitrary"),
        name="ffn_half",
    )(x, g.reshape(1, d), mods, mods, mods, w1, w3, w2)


def _mm_kernel(x_ref, w_ref, *rest, epilogue, n_extra):
    extra, o_refs = rest[:n_extra], rest[n_extra:]
    acc = _bdot(x_ref[...], w_ref[...])
    outs = epilogue(acc, *extra) if epilogue is not None else acc
    if not isinstance(outs, tuple):
        outs = (outs,)
    for o_ref, o in zip(o_refs, outs):
        o_ref[...] = o.astype(o_ref.dtype)


MM_TM = 1024
MM_TN = 512
MM_TM_DIAG = 2048


def matmul(x, w, *, tm=MM_TM, tn=MM_TN, out_dtype=F32, epilogue=None, extras=(), slab_out=False, name="matmul"):
    n, kb = x.shape
    m = w.shape[1]
    tm = min(tm, n)
    dts = out_dtype if isinstance(out_dtype, tuple) else (out_dtype,)
    if slab_out:
        out_shape = tuple(jax.ShapeDtypeStruct((m // 128, n, 128), dt) for dt in dts)
        out_specs = tuple(pl.BlockSpec((tn // 128, tm, 128), lambda i, j: (j, i, 0)) for _ in dts)
    else:
        out_shape = tuple(jax.ShapeDtypeStruct((n, m), dt) for dt in dts)
        out_specs = tuple(pl.BlockSpec((tm, tn), lambda i, j: (i, j)) for _ in dts)
    res = pl.pallas_call(
        functools.partial(_mm_kernel, epilogue=epilogue, n_extra=len(extras)),
        out_shape=out_shape,
        grid=(n // tm, m // tn),
        in_specs=[pl.BlockSpec((tm, kb), lambda i, j: (i, 0)), pl.BlockSpec((kb, tn), lambda i, j: (0, j))]
        + [pl.BlockSpec(bs, im) for _, bs, im in extras],
        out_specs=out_specs,
        compiler_params=_cparams("parallel", "parallel"),
        name=name,
    )(x, w, *[a for a, _, _ in extras])
    return res if isinstance(out_dtype, tuple) else res[0]


def _mm_t_kernel(x_ref, w_ref, *rest, epilogue, n_extra):
    extra, o_ref = rest[:n_extra], rest[n_extra]
    acc = lax.dot_general(w_ref[...], x_ref[...].astype(BF16), (((1,), (1,)), ((), ())), preferred_element_type=F32)
    o_ref[...] = (epilogue(acc, *extra) if epilogue is not None else acc).astype(o_ref.dtype)


def matmul_t(x, w_t, *, tm=MM_TM, tn=MM_TN, out_dtype=BF16, epilogue=None, extras=(), name="matmul_t"):
    n, kb = x.shape
    m = w_t.shape[0]
    tm = min(tm, n)
    return pl.pallas_call(
        functools.partial(_mm_t_kernel, epilogue=epilogue, n_extra=len(extras)),
        out_shape=jax.ShapeDtypeStruct((m, n), out_dtype),
        grid=(n // tm, m // tn),
        in_specs=[pl.BlockSpec((tm, kb), lambda i, j: (i, 0)), pl.BlockSpec((tn, kb), lambda i, j: (j, 0))]
        + [pl.BlockSpec(bs, im) for _, bs, im in extras],
        out_specs=pl.BlockSpec((tn, tm), lambda i, j: (j, i)),
        compiler_params=_cparams("parallel", "parallel"),
        name=name,
    )(x, w_t, *[a for a, _, _ in extras])


def _residual_epilogue(acc, x_ref, gt_ref, *, stream, tm):
    row = stream.mod_row(pl.program_id(0) * tm)
    return x_ref[...] + _mod_row(gt_ref, row) * acc


def out_proj_residual(z, w, x, mods, layer, stream, *, name="out_proj"):
    d = x.shape[1]
    tm, tn = min(MM_TM, x.shape[0]), MM_TN
    return matmul(z, w, tm=tm, tn=tn, name=name,
                  epilogue=functools.partial(_residual_epilogue, stream=stream, tm=tm),
                  extras=[(x, (tm, tn), lambda i, j: (i, j)),
                          (mods, (None, MOD_ROWS, tn), lambda i, j: (layer, 0, 5 * (d // tn) + j))])


def _glu_epilogue(acc, z_ref, b_ref):
    return z_ref[...] * jax.nn.sigmoid(acc + b_ref[...])


SUBLANES = 8
LANES = 128
SCAN_BLOCK = 256
SEQ_BLOCK_ELEMS = 1 << 20


def _conv_rows(x_ref, t0, tb, seq, cw, cb):
    x = x_ref[pl.ds(t0, tb), :]
    prev = x_ref[pl.ds(pl.multiple_of(jnp.maximum(t0 - SUBLANES, 0), SUBLANES), SUBLANES), :]
    nxt = x_ref[pl.ds(pl.multiple_of(jnp.minimum(t0 + tb, seq - SUBLANES), SUBLANES), SUBLANES), :]
    has_prev = t0 > 0
    has_next = t0 + tb < seq
    p1 = jnp.where(has_prev, prev[SUBLANES - 1:SUBLANES, :], 0.0)
    n0 = jnp.where(has_next, nxt[0:1, :], 0.0)
    n1 = jnp.where(has_next, nxt[1:2, :], 0.0)
    rows = lax.broadcasted_iota(jnp.int32, x.shape, 0)
    xm1 = jnp.where(rows == 0, p1, pltpu.roll(x, 1, axis=0))
    xp1 = jnp.where(rows == tb - 1, n0, pltpu.roll(x, tb - 1, axis=0))
    xp2 = jnp.where(rows == tb - 1, n1, jnp.where(rows == tb - 2, n0, pltpu.roll(x, tb - 2, axis=0)))
    return cb + xm1 * cw[0:1, :] + x * cw[1:2, :] + xp1 * cw[2:3, :] + xp2 * cw[3:4, :]


def _group_scan(a, b, reverse):
    tb, tc = a.shape
    a = a.reshape(tb // SUBLANES, SUBLANES, tc)
    b = b.reshape(tb // SUBLANES, SUBLANES, tc)
    pos = lax.broadcasted_iota(jnp.int32, a.shape, 1)
    s = 1
    while s < SUBLANES:
        ok = pos < SUBLANES - s if reverse else pos >= s
        shift = SUBLANES - s if reverse else s
        a_s, b_s = pltpu.roll(a, shift, axis=1), pltpu.roll(b, shift, axis=1)
        b = b + jnp.where(ok, a, 0.0) * b_s
        a = jnp.where(ok, a * a_s, a)
        s *= 2
    return a.reshape(tb, tc), b.reshape(tb, tc)


def _softplus(x):
    return jnp.maximum(x, 0.0) + jnp.log1p(jnp.exp(-jnp.abs(x)))


SLAB_CHUNK = 8
S5_RECUR_ROWS = 512


def _s5_slab_operators(a_re, a_im, log_dt, b_re, b_im, c_re, c_im):
    hp = lax.Precision.HIGHEST
    t = SLAB_CHUNK
    a_re, a_im, b_re, b_im, c_re, c_im = (z.astype(F32) for z in (a_re, a_im, b_re, b_im, c_re, c_im))
    ng, p = a_re.shape[1], a_re.shape[2]
    c = b_re.shape[3]
    gs = LANES // c
    ns = ng // gs
    dt = jnp.exp(log_dt.astype(F32))[..., None]
    mag = jnp.exp(a_re * dt)
    lr = mag * jnp.cos(a_im * dt)
    li = mag * jnp.sin(a_im * dt)
    den = a_re * a_re + a_im * a_im
    cr = ((lr - 1.0) * a_re + li * a_im) / den
    ci = (li * a_re - (lr - 1.0) * a_im) / den
    bbr = cr[..., None] * b_re - ci[..., None] * b_im
    bbi = cr[..., None] * b_im + ci[..., None] * b_re

    def powers(k):
        k = k.astype(F32)[None, None, :, None]
        pm = jnp.exp(k * (a_re * dt)[:, :, None, :])
        return pm * jnp.cos(k * (a_im * dt)[:, :, None, :]), pm * jnp.sin(k * (a_im * dt)[:, :, None, :])

    pr, pi = powers(jnp.arange(t + 1))
    pbr = (pr[:, :, :, :, None] * bbr[:, :, None] - pi[:, :, :, :, None] * bbi[:, :, None])
    pbi = (pr[:, :, :, :, None] * bbi[:, :, None] + pi[:, :, :, :, None] * bbr[:, :, None])
    flat = lambda z: z.transpose(0, 1, 3, 2, 4).reshape(2, ng, p, (t + 1) * c)
    m = (jnp.einsum('dgop,dgpx->dgox', c_re, flat(pbr), precision=hp)
         - jnp.einsum('dgop,dgpx->dgox', c_im, flat(pbi), precision=hp)).reshape(2, ng, c, t + 1, c)

    def block_diag(small, row_group, col_group, col_inner):
        rows, q = small.shape[-2], small.shape[-1]
        cols = jnp.arange(gs * q)
        expand = (jnp.arange(q)[:, None] == col_inner(cols)[None, :]).astype(BF16)
        wide = jnp.matmul(small.astype(BF16), expand, preferred_element_type=BF16)
        keep = row_group(jnp.arange(rows))[:, None] == col_group(cols)[None, :]
        return jnp.where(keep, wide, jnp.zeros((), BF16))

    s_idx = jnp.arange(t)[:, None]
    t_idx = jnp.arange(t)[None, :]
    lag = jnp.stack([jnp.clip(t_idx - s_idx, 0, t), jnp.clip(s_idx - t_idx, 0, t)])
    live = jnp.stack([t_idx >= s_idx, s_idx >= t_idx]).astype(F32)
    msel = sum((m[d][:, :, lag[d], :] * live[d][None, None, :, :, None]).transpose(0, 2, 3, 4, 1) for d in range(2))
    msel = msel.reshape(ns, gs, t, t, c, c).transpose(0, 2, 1, 4, 3, 5)
    in_group = lambda r: (r % LANES) // c
    out_col = lambda col: (col // LANES) * c + col % c
    state_group = lambda r: (r % (gs * p)) // p
    state_col = lambda col: (col // (gs * p)) * p + col % p
    toep = block_diag(msel.reshape(ns, t * LANES, t * c), in_group, in_group, out_col)
    pow_s = jnp.stack([t - 1 - jnp.arange(t), jnp.arange(t)])
    pb = jnp.stack([pbr, pbi], axis=2)
    s_sel = jnp.stack([pb[d][:, :, pow_s[d]] for d in range(2)])
    s_sel = s_sel.reshape(2, ns, gs, 2, t, p, c).transpose(0, 1, 4, 2, 6, 3, 5)
    s_op = block_diag(s_sel.reshape(2, ns, t * LANES, 2 * p), in_group, state_group, state_col)
    pow_w = jnp.stack([jnp.arange(t) + 1, t - jnp.arange(t)])
    zr = jnp.stack([c_re[d][:, None] * pr[d][:, pow_w[d]][:, :, None, :]
                    - c_im[d][:, None] * pi[d][:, pow_w[d]][:, :, None, :] for d in range(2)])
    zi = jnp.stack([c_re[d][:, None] * pi[d][:, pow_w[d]][:, :, None, :]
                    + c_im[d][:, None] * pr[d][:, pow_w[d]][:, :, None, :] for d in range(2)])
    w_sel = jnp.stack([zr, -zi], axis=2)
    w_sel = w_sel.reshape(2, ns, gs, 2, t, c, p).transpose(0, 1, 3, 2, 6, 4, 5)
    w_op = block_diag(w_sel.reshape(2, ns, 2 * gs * p, t * c), state_group, in_group, out_col)
    qr, qi = powers(t * (jnp.arange(SUBLANES) + 1))
    lanes = lambda z: z.reshape(2, ns, gs, SUBLANES, p).transpose(0, 1, 3, 2, 4).reshape(2, ns, SUBLANES, gs * p)
    pow_a = jnp.concatenate([lanes(qr), lanes(qr)], axis=-1)
    pow_b = jnp.concatenate([-lanes(qi), lanes(qi)], axis=-1)
    return toep, s_op, w_op, pow_a, pow_b


def _s5_state_kernel(u_ref, s_ref, o_ref):
    o_ref[...] = jnp.dot(u_ref[...], s_ref[...], preferred_element_type=F32)


def _s5_recur_slab_kernel(st_ref, pa_ref, pb_ref, h0_ref, hin_ref, fin_ref, loc_ref, *, nj):
    rows, w = loc_ref.shape
    half = w // 2
    ng = nj // SUBLANES
    swap = lambda z: jnp.concatenate([z[..., half:], z[..., :half]], axis=-1)
    for d in range(2):
        reverse = d == 1
        pa = pa_ref[d]
        pb = pb_ref[d]
        z = st_ref[d].reshape(rows // SUBLANES, SUBLANES, w)
        pos = lax.broadcasted_iota(jnp.int32, z.shape, 1)
        s = 1
        while s < SUBLANES:
            ok = pos < SUBLANES - s if reverse else pos >= s
            zs = pltpu.roll(z, SUBLANES - s if reverse else s, axis=1)
            z = z + jnp.where(ok, zs * pa[s - 1:s, :] + swap(zs) * pb[s - 1:s, :], 0.0)
            s *= 2
        loc_ref[...] = z.reshape(rows, w)
        row = lax.broadcasted_iota(jnp.int32, (SUBLANES, w), 0)
        if reverse:
            ca = jnp.concatenate([pa[SUBLANES - 1 - r:SUBLANES - r, :] for r in range(SUBLANES)], axis=0)
            cb = jnp.concatenate([pb[SUBLANES - 1 - r:SUBLANES - r, :] for r in range(SUBLANES)], axis=0)
        else:
            ca, cb = pa, pb

        for b in range(rows // nj):
            def group(i, carry, d=d, b=b, reverse=reverse, ca=ca, cb=cb):
                c, c_sw = carry
                r0 = pl.multiple_of(b * nj + (ng - 1 - i if reverse else i) * SUBLANES, SUBLANES)
                h = loc_ref[pl.ds(r0, SUBLANES), :] + c * ca + c_sw * cb
                shifted = pltpu.roll(h, SUBLANES - 1 if reverse else 1, axis=0)
                hin_ref[d, pl.ds(r0, SUBLANES), :] = jnp.where(row == (SUBLANES - 1 if reverse else 0), c, shifted)
                last = h[0:1, :] if reverse else h[SUBLANES - 1:SUBLANES, :]
                return last, swap(last)

            h0 = h0_ref[d, b]
            fin_ref[d, b] = lax.fori_loop(0, ng, group, (h0, swap(h0)))[0]


def _s5_out_kernel(u_ref, toep_ref, hin_ref, w_ref, o_ref):
    acc = jnp.dot(u_ref[...], toep_ref[...], preferred_element_type=F32)
    for d in range(2):
        acc = acc + jnp.dot(hin_ref[d].astype(BF16), w_ref[d], preferred_element_type=F32)
    o_ref[...] = acc


def s5_slab_scan(u16, h0, ops, batch, seq):
    toep, s_op, w_op, pow_a, pow_b = ops
    ns, n, _ = u16.shape
    t = SLAB_CHUNK
    kc = t * LANES
    w = s_op.shape[-1]
    nj = seq // t
    rows = n // t
    tr = min(rows, 1024)
    uc = u16.reshape(ns, rows, kc)
    st = pl.pallas_call(
        _s5_state_kernel,
        out_shape=jax.ShapeDtypeStruct((2, ns, rows, w), F32),
        grid=(ns, 2, rows // tr),
        in_specs=[pl.BlockSpec((None, tr, kc), lambda a, d, r: (a, r, 0)),
                  pl.BlockSpec((None, None, kc, w), lambda a, d, r: (d, a, 0, 0))],
        out_specs=pl.BlockSpec((None, None, tr, w), lambda a, d, r: (d, a, r, 0)),
        compiler_params=_cparams("parallel", "parallel", "parallel"),
        name="s5_state",
    )(uc, s_op)
    bb = max(1, min(batch, S5_RECUR_ROWS // nj))
    hin, fin = pl.pallas_call(
        functools.partial(_s5_recur_slab_kernel, nj=nj),
        out_shape=(jax.ShapeDtypeStruct((2, ns, rows, w), F32), jax.ShapeDtypeStruct((2, ns, batch, 1, w), F32)),
        grid=(ns, batch // bb),
        in_specs=[pl.BlockSpec((2, None, bb * nj, w), lambda a, b: (0, a, b, 0)),
                  pl.BlockSpec((2, None, SUBLANES, w), lambda a, b: (0, a, 0, 0)),
                  pl.BlockSpec((2, None, SUBLANES, w), lambda a, b: (0, a, 0, 0)),
                  pl.BlockSpec((2, None, bb, 1, w), lambda a, b: (0, a, b, 0, 0))],
        out_specs=(pl.BlockSpec((2, None, bb * nj, w), lambda a, b: (0, a, b, 0)),
                   pl.BlockSpec((2, None, bb, 1, w), lambda a, b: (0, a, b, 0, 0))),
        scratch_shapes=[pltpu.VMEM((bb * nj, w), F32)],
        compiler_params=_cparams("parallel", "parallel"),
        name="s5_recur",
    )(st, pow_a, pow_b, h0)
    y = pl.pallas_call(
        _s5_out_kernel,
        out_shape=jax.ShapeDtypeStruct((ns, rows, kc), F32),
        grid=(ns, rows // tr),
        in_specs=[pl.BlockSpec((None, tr, kc), lambda a, r: (a, r, 0)),
                  pl.BlockSpec((None, kc, kc), lambda a, r: (a, 0, 0)),
                  pl.BlockSpec((2, None, tr, w), lambda a, r: (0, a, r, 0)),
                  pl.BlockSpec((2, None, w, kc), lambda a, r: (0, a, 0, 0))],
        out_specs=pl.BlockSpec((None, tr, kc), lambda a, r: (a, r, 0)),
        compiler_params=_cparams("parallel", "parallel"),
        name="s5_out_chunks",
    )(uc, toep, hin, w_op)
    return y.reshape(ns, n, LANES), fin


def _slab_epilogue(acc):
    parts = jnp.stack([acc[:, s * LANES:(s + 1) * LANES] for s in range(acc.shape[1] // LANES)])
    return parts, parts


def _s5_post_slab_kernel(u_ref, y_ref, d_ref, o_ref):
    for s in range(u_ref.shape[0]):
        cols = slice(s * LANES, (s + 1) * LANES)
        o_ref[:, cols] = jax.nn.gelu(d_ref[:, cols] * u_ref[s] + y_ref[s])


def s5_post_slab(u, y, d_skip, *, tm=512):
    ns, n, _ = u.shape
    w = ns * LANES
    slab = pl.BlockSpec((ns, tm, LANES), lambda i: (0, i, 0))
    return pl.pallas_call(
        _s5_post_slab_kernel,
        out_shape=jax.ShapeDtypeStruct((n, w), F32),
        grid=(n // tm,),
        in_specs=[slab, slab, pl.BlockSpec((1, w), lambda i: (0, 0))],
        out_specs=pl.BlockSpec((tm, w), lambda i: (i, 0)),
        compiler_params=_cparams("parallel"),
        name="s5_post",
    )(u, y, d_skip.reshape(1, w).astype(F32))


def s5_slab_mixer(h, x, mods, layer, stream, state, w_in, ops, d_skip, w_glu, b_glu, w_out):
    u, u16 = matmul(h, w_in, out_dtype=(F32, BF16), epilogue=_slab_epilogue, slab_out=True, name="s5_in")
    y, fin = s5_slab_scan(u16, state, ops, stream.batch, stream.seq)
    z = s5_post_slab(u, y, d_skip)
    w = z.shape[1]
    zz = matmul(z, w_glu, out_dtype=BF16, name="s5_glu", epilogue=_glu_epilogue,
                extras=[(z, (min(MM_TM, z.shape[0]), MM_TN), lambda i, j: (i, j)),
                        (b_glu.reshape(1, w).astype(F32), (1, MM_TN), lambda i, j: (0, j))])
    return out_proj_residual(zz, w_out, x, mods, layer, stream, name="s5_out"), fin


def s5_pack_state(state, slab_groups):
    b, _, _, g, p = state.shape
    s = state.astype(F32).reshape(b, 2, 2, g // slab_groups, slab_groups * p)
    return s.transpose(1, 3, 0, 2, 4).reshape(2, g // slab_groups, b, 1, 2 * slab_groups * p)


def s5_unpack_state(fin, slab_groups, p):
    _, ns, b, _, w = fin.shape
    s = fin.reshape(2, ns, b, 2, slab_groups * p).transpose(2, 0, 3, 1, 4)
    return s.reshape(b, 2, 2, ns * slab_groups, p)


RG_C = 8.0


def _rg_core_kernel(x_ref, gate_ref, cw_ref, cb_ref, w_ref, bias_ref, lam_ref, h0_ref, o_ref, fin_ref,
                    y_ref, xc_ref, a_ref, b_ref, *, seq):
    tb = a_ref.shape[0]
    tc = a_ref.shape[1]
    n_blocks = seq // tb
    n_groups = tb // SUBLANES
    cw = cw_ref[...]
    cb = cb_ref[...]
    for d in range(2):
        reverse = d == 1
        sp = _softplus(-lam_ref[d:d + 1, :])
        bias_a = bias_ref[d, 0:1, :]
        bias_x = bias_ref[d, 1:2, :]

        def block(k, carry, d=d, reverse=reverse, sp=sp, bias_a=bias_a, bias_x=bias_x):
            t0 = pl.multiple_of((n_blocks - 1 - k if reverse else k) * tb, tb)
            if reverse:
                xc = xc_ref[pl.ds(t0, tb), :]
            else:
                xc = _conv_rows(x_ref, t0, tb, seq, cw, cb)
                xc_ref[pl.ds(t0, tb), :] = xc
            pre = [_bdot(xc[:, n * LANES:(n + 1) * LANES], w_ref[d, n]) for n in range(tc // LANES)]
            r = jax.nn.sigmoid(jnp.concatenate([p[:, :LANES] for p in pre], axis=1) + bias_a)
            ig = jax.nn.sigmoid(jnp.concatenate([p[:, LANES:] for p in pre], axis=1) + bias_x)
            log_a = -RG_C * r * sp
            a = jnp.exp(log_a)
            bv = jnp.sqrt(-jnp.tanh(log_a) * (a * a + 1.0)) * (ig * xc)
            a_ref[...], b_ref[...] = _group_scan(a, bv, reverse)

            def group(i, c):
                r0 = pl.multiple_of((n_groups - 1 - i if reverse else i) * SUBLANES, SUBLANES)
                h = b_ref[pl.ds(r0, SUBLANES), :] + a_ref[pl.ds(r0, SUBLANES), :] * c
                rows = pl.ds(pl.multiple_of(t0 + r0, SUBLANES), SUBLANES)
                if reverse:
                    y_ref[rows, :] += h
                    return h[0:1, :]
                y_ref[rows, :] = h
                return h[SUBLANES - 1:SUBLANES, :]

            carry = lax.fori_loop(0, n_groups, group, carry)
            if reverse:
                o_ref[pl.ds(t0, tb), :] = (y_ref[pl.ds(t0, tb), :] * gate_ref[pl.ds(t0, tb), :]).astype(o_ref.dtype)
            return carry

        fin_ref[d:d + 1, :] = lax.fori_loop(0, n_blocks, block, h0_ref[d:d + 1, :])


def rg_core(xin, gate, conv_w, conv_b, w_gates, bias, lam, h0, stream, *, tc=256):
    n, c = xin.shape
    seq, batch = stream.seq, stream.batch
    tb = min(SCAN_BLOCK, seq)
    nb = tc // LANES
    return pl.pallas_call(
        functools.partial(_rg_core_kernel, seq=seq),
        out_shape=(jax.ShapeDtypeStruct((n, c), BF16), jax.ShapeDtypeStruct((batch, 2, c), F32)),
        grid=(batch, c // tc),
        in_specs=[pl.BlockSpec((seq, tc), lambda b, j: (b, j)),
                  pl.BlockSpec((seq, tc), lambda b, j: (b, j)),
                  pl.BlockSpec((4, tc), lambda b, j: (0, j)),
                  pl.BlockSpec((1, tc), lambda b, j: (0, j)),
                  pl.BlockSpec((2, nb, LANES, 2 * LANES), lambda b, j: (0, j, 0, 0)),
                  pl.BlockSpec((2, 2, tc), lambda b, j: (0, 0, j)),
                  pl.BlockSpec((2, tc), lambda b, j: (0, j)),
                  pl.BlockSpec((None, 2, tc), lambda b, j: (b, 0, j))],
        out_specs=(pl.BlockSpec((seq, tc), lambda b, j: (b, j)),
                   pl.BlockSpec((None, 2, tc), lambda b, j: (b, 0, j))),
        scratch_shapes=[pltpu.VMEM((seq, tc), F32), pltpu.VMEM((seq, tc), F32),
                        pltpu.VMEM((tb, tc), F32), pltpu.VMEM((tb, tc), F32)],
        compiler_params=_cparams("parallel", "parallel"),
        name="rg_core",
    )(xin, gate, conv_w, conv_b.reshape(1, c), w_gates, bias, lam, h0)


def _gelu_epilogue(acc):
    return jax.nn.gelu(acc)


def rg_mixer(h, x, mods, layer, stream, state, w_in, w_gate, conv_w, conv_b, w_gates, bias, lam, w_out):
    xin = matmul(h, w_in, name="rg_in")
    gate = matmul(h, w_gate, epilogue=_gelu_epilogue, name="rg_gate")
    yg, fin = rg_core(xin, gate, conv_w, conv_b, w_gates, bias, lam, state, stream)
    return out_proj_residual(yg, w_out, x, mods, layer, stream, name="rg_out"), fin


GRID_W = 64
ROPE_BASE = 10000.0
ATTN_TQ = 512
ATTN_TK = 512


def rope_tables(seq, dk):
    half = dk // 2
    quarter = half // 2
    pos = jnp.arange(seq)
    inv = ROPE_BASE ** (-jnp.arange(0, half, 2, dtype=F32) / half)
    ang_r = (pos // GRID_W).astype(F32)[:, None] * inv
    ang_c = (pos % GRID_W).astype(F32)[:, None] * inv
    cos = jnp.concatenate([jnp.cos(ang_r)] * 2 + [jnp.cos(ang_c)] * 2, axis=1)
    sin = jnp.concatenate([-jnp.sin(ang_r), jnp.sin(ang_r), -jnp.sin(ang_c), jnp.sin(ang_c)], axis=1)
    assert cos.shape == (seq, 4 * quarter)
    return cos, sin


def _rope_epilogue(acc, cos_ref, sin_ref):
    cos = cos_ref[...]
    sin = sin_ref[...]
    dk = cos.shape[1]
    lane = lax.broadcasted_iota(jnp.int32, cos.shape, 1)
    first = (lane % (dk // 2)) < dk // 4
    outs = []
    for g in range(acc.shape[1] // dk):
        x = acc[:, g * dk:(g + 1) * dk]
        rot = jnp.where(first, pltpu.roll(x, dk - dk // 4, axis=1), pltpu.roll(x, dk // 4, axis=1))
        outs.append(x * cos + rot * sin)
    return jnp.concatenate(outs, axis=1)


def _attn_t_kernel(*refs, n_tiles, has_cache, lam_init, scale):
    if has_cache:
        qt_ref, k_ref, vt_ref, ck_ref, cvt_ref, lam_ref, g_ref, o_ref, s_ref, sc_ref = refs
    else:
        qt_ref, k_ref, vt_ref, lam_ref, g_ref, o_ref, s_ref = refs
    tq = qt_ref.shape[1]
    dk = qt_ref.shape[0] // 2
    dv = vt_ref.shape[0]
    tk = s_ref.shape[2]
    maps = range(2)
    qt = [qt_ref[c * dk:(c + 1) * dk, :] for c in maps]

    def scores(kt):
        return [jnp.dot(kt[:, c * dk:(c + 1) * dk], qt[c], preferred_element_type=F32) for c in maps]

    m = [jnp.full((1, tq), -jnp.inf, F32) for _ in maps]
    if has_cache:
        s = scores(ck_ref[...])
        for c in maps:
            sc_ref[c] = s[c]
            m[c] = jnp.maximum(m[c], jnp.max(s[c], axis=0, keepdims=True))

    for i in range(n_tiles):
        s = scores(k_ref[i * tk:(i + 1) * tk, :])
        for c in maps:
            s_ref[c, i] = s[c]
            m[c] = jnp.maximum(m[c], jnp.max(s[c], axis=0, keepdims=True))

    c2 = scale * math.log2(math.e)

    def accumulate(s, vt, st):
        new = []
        for c in maps:
            l, acc = st[c]
            p = jnp.exp2((s[c] - m[c]) * c2)
            new.append((l + jnp.sum(p, axis=0, keepdims=True),
                        acc + jnp.dot(vt, p.astype(BF16), preferred_element_type=F32)))
        return tuple(new)

    st = tuple((jnp.zeros((1, tq), F32), jnp.zeros((dv, tq), F32)) for _ in maps)
    if has_cache:
        st = accumulate([sc_ref[c] for c in maps], cvt_ref[...], st)

    for i in range(n_tiles):
        st = accumulate([s_ref[c, i] for c in maps], vt_ref[:, i * tk:(i + 1) * tk], st)
    lp = lam_ref[...]
    lam = (jnp.exp(jnp.sum(lp[0:1, :] * lp[1:2, :], axis=-1, keepdims=True))
           - jnp.exp(jnp.sum(lp[2:3, :] * lp[3:4, :], axis=-1, keepdims=True)) + lam_init)
    diff = st[0][1] / st[0][0] - lam * (st[1][1] / st[1][0])
    diff = diff * lax.rsqrt(jnp.mean(diff * diff, axis=0, keepdims=True) + NORM_EPS)
    o_ref[...] = (diff.T * g_ref[...] * (1.0 - lam_init)).astype(o_ref.dtype)


def diff_attention_t(q_t, k, v_t, *rest, stream, lam_init):
    has_cache = len(rest) == 4
    da_lam, subln_g = rest[-2], rest[-1]
    n = k.shape[0]
    dv = subln_g.shape[0]
    heads = v_t.shape[0] // dv
    batch, seq = stream.batch, stream.seq
    tq = min(ATTN_TQ, seq)
    tk = min(ATTN_TK, seq)
    n_tiles = seq // tk
    nq = seq // tq
    in_specs = [pl.BlockSpec((dv, tq), lambda b, h, i: (h, b * nq + i)),
                pl.BlockSpec((seq, dv), lambda b, h, i: (b, h)),
                pl.BlockSpec((dv, seq), lambda b, h, i: (h, b))]
    args = [q_t, k, v_t]
    scratch = [pltpu.VMEM((2, n_tiles, tk, tq), F32)]
    if has_cache:
        ck, cv_t = rest[0], rest[1]
        past = cv_t.shape[3]
        in_specs += [pl.BlockSpec((past, dv), lambda b, h, i: (b, h)),
                     pl.BlockSpec((None, None, dv, past), lambda b, h, i: (b, h, 0, 0))]
        args += [ck, cv_t]
        scratch += [pltpu.VMEM((2, past, tq), F32)]
    in_specs += [pl.BlockSpec(da_lam.shape, lambda b, h, i: (0, 0)), pl.BlockSpec((1, dv), lambda b, h, i: (0, 0))]
    args += [da_lam.astype(F32), subln_g.reshape(1, dv).astype(F32)]
    return pl.pallas_call(
        functools.partial(_attn_t_kernel, n_tiles=n_tiles, has_cache=has_cache, lam_init=lam_init,
                          scale=1.0 / math.sqrt(dv // 2)),
        out_shape=jax.ShapeDtypeStruct((n, heads * dv), BF16),
        grid=(batch, heads, nq),
        in_specs=in_specs,
        out_specs=pl.BlockSpec((tq, dv), lambda b, h, i: (b * nq + i, h)),
        scratch_shapes=scratch,
        compiler_params=_cparams("parallel", "parallel", "parallel"),
        name="diff_attention",
    )(*args)


def _rope_t_epilogue(acc, cos_ref, sin_ref):
    cos = cos_ref[...]
    sin = sin_ref[...]
    dk = cos.shape[0]
    row = lax.broadcasted_iota(jnp.int32, cos.shape, 0)
    first = (row % (dk // 2)) < dk // 4
    outs = []
    for g in range(acc.shape[0] // dk):
        x = acc[g * dk:(g + 1) * dk, :]
        rot = jnp.where(first, pltpu.roll(x, dk - dk // 4, axis=0), pltpu.roll(x, dk // 4, axis=0))
        outs.append(x * cos + rot * sin)
    return jnp.concatenate(outs, axis=0)


def _dup_epilogue(acc):
    return acc, acc


def dattn_mixer(h, x, mods, layer, stream, cache, wq_t, wk, wv, wv_t, wo, da_lam, subln_g, lam_init):
    dk = subln_g.shape[0] // 2
    v_t = matmul_t(h, wv_t, name="da_v_t")
    if cache is None:
        q_t = matmul_t(h, wq_t, name="da_q_t")
        k32, k = matmul(h, wk, out_dtype=(F32, BF16), epilogue=_dup_epilogue, name="da_k")
        v32 = matmul(h, wv, name="da_v")
        o = diff_attention_t(q_t, k, v_t, da_lam, subln_g, stream=stream, lam_init=lam_init)
        new = (k32, v32)
    else:
        cos, sin = rope_tables(stream.seq, dk)
        tm = min(MM_TM, stream.seq)
        per = stream.seq // tm
        rope = [(cos, (tm, dk), lambda i, j: (i % per, 0)), (sin, (tm, dk), lambda i, j: (i % per, 0))]
        rope_t = [(cos.T, (dk, tm), lambda i, j: (0, i % per)), (sin.T, (dk, tm), lambda i, j: (0, i % per))]
        q_t = matmul_t(h, wq_t, tm=tm, epilogue=_rope_t_epilogue, extras=rope_t, name="da_q_t_rope")
        k = matmul(h, wk, tm=tm, out_dtype=BF16, epilogue=_rope_epilogue, extras=rope, name="da_k_rope")
        o = diff_attention_t(q_t, k, v_t, cache[0], cache[1], da_lam, subln_g, stream=stream, lam_init=lam_init)
        new = None
    return out_proj_residual(o, wo, x, mods, layer, stream, name="da_out"), new


ML_CHUNK = 256


def _ml_conv_kernel(x_ref, cw_ref, cb_ref, o32_ref, o16_ref, *, seq):
    tb = min(SCAN_BLOCK, seq)
    cw = cw_ref[...]
    cb = cb_ref[...]

    def body(i, carry):
        t0 = pl.multiple_of(i * tb, tb)
        y = _conv_rows(x_ref, t0, tb, seq, cw, cb)
        y = y * jax.nn.sigmoid(y)
        o32_ref[pl.ds(t0, tb), :] = y
        o16_ref[pl.ds(t0, tb), :] = y.astype(BF16)
        return carry

    lax.fori_loop(0, seq // tb, body, 0)


def ml_conv(xi, conv_w, conv_b, stream):
    n, w = xi.shape
    seq = stream.seq
    tc = min(w, 1024, SEQ_BLOCK_ELEMS // seq)
    return pl.pallas_call(
        functools.partial(_ml_conv_kernel, seq=seq),
        out_shape=(jax.ShapeDtypeStruct((n, w), F32), jax.ShapeDtypeStruct((n, w), BF16)),
        grid=(stream.batch, w // tc),
        in_specs=[pl.BlockSpec((seq, tc), lambda b, j: (b, j)),
                  pl.BlockSpec((conv_w.shape[0], tc), lambda b, j: (0, j)),
                  pl.BlockSpec((1, tc), lambda b, j: (0, j))],
        out_specs=(pl.BlockSpec((seq, tc), lambda b, j: (b, j)), pl.BlockSpec((seq, tc), lambda b, j: (b, j))),
        compiler_params=_cparams("parallel", "parallel"),
        name="ml_conv",
    )(xi, conv_w.astype(F32), conv_b.reshape(1, w).astype(F32))


def _ml_qkv_kernel(xc_ref, xi_ref, wq_ref, wk_ref, wkt_ref, wv_ref, q_ref, k_ref, kt_ref, v_ref, *, scale):
    xc = xc_ref[...]
    q_ref[...] = jnp.dot(xc, wq_ref[...], preferred_element_type=F32).astype(q_ref.dtype)
    k_ref[...] = (jnp.dot(xc, wk_ref[...], preferred_element_type=F32) * scale).astype(k_ref.dtype)
    kt = lax.dot_general(wkt_ref[...], xc, (((1,), (1,)), ((), ())), preferred_element_type=F32)
    kt_ref[...] = (kt * scale).astype(kt_ref.dtype)
    v_ref[...] = _bdot(xi_ref[...], wv_ref[...]).astype(v_ref.dtype)


def ml_qkv(xc16, xi, wq, wk, wk_t, wv):
    n, w = xc16.shape
    dh = wq.shape[1]
    tm = min(MM_TM_DIAG, n)
    tok = pl.BlockSpec((tm, dh), lambda i, h: (i, h))
    wsp = pl.BlockSpec((dh, dh), lambda i, h: (h, 0))
    tokens = jax.ShapeDtypeStruct((n, w), BF16)
    return pl.pallas_call(
        functools.partial(_ml_qkv_kernel, scale=1.0 / math.sqrt(dh)),
        out_shape=(tokens, tokens, jax.ShapeDtypeStruct((w, n), BF16), tokens),
        grid=(n // tm, w // dh),
        in_specs=[tok, tok, wsp, wsp, wsp, wsp],
        out_specs=(tok, tok, pl.BlockSpec((dh, tm), lambda i, h: (h, i)), tok),
        compiler_params=_cparams("parallel", "parallel"),
        name="ml_qkv",
    )(xc16, xi, wq, wk, wk_t, wv)


def _ml_gates_kernel(q_ref, k_ref, v_ref, w_ref, b_ref, o_ref, *, n_heads):
    acc = jnp.dot(q_ref[...], w_ref[0], preferred_element_type=F32)
    acc = acc + jnp.dot(k_ref[...], w_ref[1], preferred_element_type=F32)
    acc = acc + jnp.dot(v_ref[...], w_ref[2], preferred_element_type=F32) + b_ref[...]
    col = lax.broadcasted_iota(jnp.int32, acc.shape, 1)
    is_forget = (col // n_heads) % 2 == 1
    o_ref[...] = jnp.where(is_forget, -_softplus(-acc), acc)


def ml_gates(q, k, v, w_gate, b_gate, n_heads, *, tm=512):
    n, w = q.shape
    ng = w_gate.shape[2]
    row = pl.BlockSpec((tm, w), lambda i: (i, 0))
    return pl.pallas_call(
        functools.partial(_ml_gates_kernel, n_heads=n_heads),
        out_shape=jax.ShapeDtypeStruct((n, ng), F32),
        grid=(n // tm,),
        in_specs=[row, row, row, pl.BlockSpec((3, w, ng), lambda i: (0, 0, 0)), pl.BlockSpec((1, ng), lambda i: (0, 0))],
        out_specs=pl.BlockSpec((tm, ng), lambda i: (i, 0)),
        compiler_params=_cparams("parallel"),
        name="ml_gates",
    )(q, k, v, w_gate, b_gate.reshape(1, ng).astype(F32))


ML_HEADS_PER_STEP = 2


def _mlstm_cell_kernel(*refs, n_heads, has_init, want_final):
    q_ref, k_ref, kt_ref, v_ref, g_ref, gt_ref = refs[:6]
    init_refs = refs[6:9] if has_init else None
    h_ref = refs[9 if has_init else 6]
    ct_s, n_s, m_s = refs[-3:]
    final_refs = refs[-6:-3] if want_final else None
    hb = ct_s.shape[0]
    dh = ct_s.shape[1]
    h0 = pl.program_id(1) * hb
    d = pl.program_id(2)
    ci = pl.program_id(3)
    reverse = d == 1
    t = q_ref.shape[0]

    @pl.when(ci == 0)
    def _():
        for j in range(hb):
            if has_init:
                ct_s[j] = init_refs[0][j].T
                n_s[j] = init_refs[1][j]
                m_s[j] = init_refs[2][j]
            else:
                ct_s[j] = jnp.zeros((dh, dh), F32)
                n_s[j] = jnp.zeros((1, dh), F32)
                m_s[j] = jnp.zeros((1, 1), F32)

    g = g_ref[...]
    gt = gt_ref[...]
    lane = lax.broadcasted_iota(jnp.int32, g.shape, 1)
    sub = lax.broadcasted_iota(jnp.int32, gt.shape, 0)
    r_idx = lax.broadcasted_iota(jnp.int32, (t, t), 0)
    s_idx = lax.broadcasted_iota(jnp.int32, (t, t), 1)
    ahead = (s_idx - r_idx) * jnp.where(reverse, -1, 1)
    mask = ahead <= 0
    mask_t = ahead >= 0
    row = lax.broadcasted_iota(jnp.int32, (t, 1), 0)
    last = row == jnp.where(reverse, 0, t - 1)

    for j in range(hb):
        cols = slice(j * dh, (j + 1) * dh)
        q = q_ref[:, cols]
        k = k_ref[:, cols]
        v = v_ref[:, cols]
        kt = kt_ref[cols, :]
        col_i = d * (2 * n_heads) + h0 + j
        col_f = col_i + n_heads
        ig_col = jnp.sum(jnp.where(lane == col_i, g, 0.0), axis=1, keepdims=True)
        lf_col = jnp.sum(jnp.where(lane == col_f, g, 0.0), axis=1, keepdims=True)
        ig_row = jnp.sum(jnp.where(sub == col_i, gt, 0.0), axis=0, keepdims=True)
        lf_row = jnp.sum(jnp.where(sub == col_f, gt, 0.0), axis=0, keepdims=True)
        b_col = jnp.sum(jnp.where(mask, lf_row, 0.0), axis=1, keepdims=True)
        b_row = jnp.sum(jnp.where(mask_t, lf_col, 0.0), axis=0, keepdims=True)
        m_prev = m_s[j]
        dlog = jnp.where(mask, b_col - b_row + ig_row, -jnp.inf)
        inter = b_col + m_prev
        m_t = jnp.maximum(inter, jnp.max(dlog, axis=1, keepdims=True))
        w_intra = jnp.exp(dlog - m_t)
        w_inter = jnp.exp(inter - m_t)
        s_mat = lax.dot_general(q, k, (((1,), (1,)), ((), ())), preferred_element_type=F32) * w_intra
        ct_old = ct_s[j]
        n_old = n_s[j]
        qc = jnp.dot(q, ct_old.astype(BF16), preferred_element_type=F32)
        num = jnp.dot(s_mat.astype(BF16), v, preferred_element_type=F32) + w_inter * qc
        n_rows = jnp.broadcast_to(n_old.astype(BF16), (SUBLANES, dh))
        qn = lax.dot_general(q, n_rows, (((1,), (1,)), ((), ())), preferred_element_type=F32)[:, 0:1]
        den = jnp.sum(s_mat, axis=1, keepdims=True) + w_inter * qn
        h_ref[:, cols] = num / jnp.maximum(jnp.abs(den), jnp.exp(-m_t))
        m_new = jnp.sum(jnp.where(last, m_t, 0.0), axis=0, keepdims=True)
        b_last = jnp.sum(jnp.where(last, b_col, 0.0), axis=0, keepdims=True)
        w_old = jnp.exp(b_last + m_prev - m_new)
        w_in = jnp.exp(b_last - b_col + ig_col - m_new)
        vw = (v.astype(F32) * w_in).astype(BF16)
        ct_s[j] = w_old * ct_old + jnp.dot(kt, vw, preferred_element_type=F32)
        n_s[j] = w_old * n_old + jnp.sum(w_in * k.astype(F32), axis=0, keepdims=True)
        m_s[j] = m_new

    if want_final:
        @pl.when(ci == pl.num_programs(3) - 1)
        def _():
            for j in range(hb):
                final_refs[0][j] = ct_s[j].T
                final_refs[1][j] = n_s[j]
                final_refs[2][j] = m_s[j]


def mlstm_cell(q, k, k_t, v, g, g_t, init, stream, heads, want_final):
    n, w = q.shape
    batch, seq = stream.batch, stream.seq
    dh = w // heads
    hb = ML_HEADS_PER_STEP
    t = min(ML_CHUNK, seq)
    nc = seq // t

    def blk(b, d, ci):
        return b * nc + ci + d * (nc - 1 - 2 * ci)

    tok = pl.BlockSpec((t, hb * dh), lambda b, h, d, ci: (blk(b, d, ci), h))
    st5 = lambda r, c: pl.BlockSpec((None, None, hb, r, c), lambda b, h, d, ci: (b, d, h, 0, 0))
    state_dims = ((dh, dh), (1, dh), (1, 1))
    out_shape = [jax.ShapeDtypeStruct((2, n, w), F32)]
    out_specs = [pl.BlockSpec((None, t, hb * dh), lambda b, h, d, ci: (d, blk(b, d, ci), h))]
    if want_final:
        out_shape += [jax.ShapeDtypeStruct((batch, 2, heads) + rc, F32) for rc in state_dims]
        out_specs += [st5(*rc) for rc in state_dims]
    res = pl.pallas_call(
        functools.partial(_mlstm_cell_kernel, n_heads=heads, has_init=init is not None, want_final=want_final),
        out_shape=tuple(out_shape),
        grid=(batch, heads // hb, 2, nc),
        in_specs=[tok, tok,
                  pl.BlockSpec((hb * dh, t), lambda b, h, d, ci: (h, blk(b, d, ci))),
                  tok,
                  pl.BlockSpec((t, g.shape[1]), lambda b, h, d, ci: (blk(b, d, ci), 0)),
                  pl.BlockSpec((g.shape[1], t), lambda b, h, d, ci: (0, blk(b, d, ci)))]
        + ([st5(*rc) for rc in state_dims] if init is not None else []),
        out_specs=tuple(out_specs),
        scratch_shapes=[pltpu.VMEM((hb,) + rc, F32) for rc in state_dims],
        compiler_params=_cparams("parallel", "parallel", "arbitrary", "arbitrary"),
        name="mlstm_cell",
    )(q, k, k_t, v, g, g_t, *(init or ()))
    return res[0], (tuple(res[1:]) if want_final else None)


def _ml_out_gate_epilogue(acc, b_ref, h_ref, xc_ref, gn_ref, skip_ref):
    hc = jax.nn.sigmoid(acc + b_ref[...]) * (h_ref[0] + h_ref[1])
    hn = hc * lax.rsqrt(jnp.mean(hc * hc, axis=-1, keepdims=True) + NORM_EPS)
    return hn * gn_ref[...] + skip_ref[...] * xc_ref[...]


def mlstm_mixer(h, x, mods, layer, stream, init, want_final, w_up, conv_w, conv_b, wq, wk, wk_t, wv, w_gate, b_gate,
                w_o, b_o, gn, skip, w_down):
    dh = wq.shape[1]
    xi = matmul(h, w_up, name="ml_up")
    xc, xc16 = ml_conv(xi, conv_w, conv_b, stream)
    q, k, k_t, v = ml_qkv(xc16, xi, wq, wk, wk_t, wv)
    w = xi.shape[1]
    g = ml_gates(q, k, v, w_gate, b_gate, w // dh)
    hdir, final = mlstm_cell(q, k, k_t, v, g, g.T, init, stream, w // dh, want_final)
    tm = min(MM_TM, h.shape[0])
    vec = lambda a: (a.reshape(1, w).astype(F32), (1, dh), lambda i, j: (0, j))
    y = matmul(h, w_o, tm=tm, tn=dh, out_dtype=BF16, name="ml_o", epilogue=_ml_out_gate_epilogue,
               extras=[vec(b_o), (hdir, (2, tm, dh), lambda i, j: (0, i, j)), (xc, (tm, dh), lambda i, j: (i, j)),
                       vec(gn), vec(skip)])
    return out_proj_residual(y, w_down, x, mods, layer, stream, name="ml_down"), final


def kernel(x_prompt, x_sample, state_s5, state_rglru, cache_dattn_k, cache_dattn_v, state_mlstm_C, state_mlstm_n, state_mlstm_m, c, c_ctx, ada_w, ada_b, norm_g, ffn1_w1, ffn1_w3, ffn1_w2, ffn2_w1, ffn2_w3, ffn2_w2, final_norm_g, s5_w_in, s5_a_re, s5_a_im, s5_log_dt, s5_b_re, s5_b_im, s5_c_re, s5_c_im, s5_d, s5_w_glu, s5_b_glu, s5_w_out, rg_w_in, rg_w_gate, rg_conv_w, rg_conv_b, rg_wa, rg_ba, rg_wx, rg_bx, rg_lam, rg_w_out, da_wq, da_wk, da_wv, da_wo, da_lam, da_subln_g, ml_w_up, ml_conv_w, ml_conv_b, ml_wq, ml_wk, ml_wv, ml_w_gate, ml_b_gate, ml_w_o, ml_b_o, ml_gn, ml_skip, ml_w_down):
    bc, lc, d = x_prompt.shape
    bl, ll, _ = x_sample.shape
    depth = ada_w.shape[0]
    cs = Stream(bc, lc, 0, False)
    ls = Stream(bl, ll, 1, True)
    ctx = x_prompt.reshape(cs.n, d)
    lat = x_sample.reshape(ls.n, d)
    c_all = jnp.concatenate([c_ctx[None, :], c, jnp.zeros((MOD_ROWS - 1 - bl, d), F32)], axis=0)
    mods = adaln(c_all, ada_w, ada_b)
    bf = lambda a: a.astype(BF16)
    ffn1 = (bf(ffn1_w1), bf(ffn1_w3), bf(ffn1_w2))
    ffn2 = (bf(ffn2_w1), bf(ffn2_w3), bf(ffn2_w2))

    new_s5 = new_rg = None
    for i in range(depth):
        ctx = ffn_half(ctx, norm_g[i, 0], mods, i, 0, *ffn1, cs)
        lat = ffn_half(lat, norm_g[i, 0], mods, i, 0, *ffn1, ls)
        hc = modnorm(ctx, norm_g[i, 1], mods, i, 3, cs)
        hl = modnorm(lat, norm_g[i, 1], mods, i, 3, ls)
        kind = i % 4
        if kind == 0:
            ops = _s5_slab_operators(s5_a_re, s5_a_im, s5_log_dt, s5_b_re, s5_b_im, s5_c_re, s5_c_im)
            p = s5_a_re.shape[2]
            slab_groups = LANES // s5_b_re.shape[3]
            args = (bf(s5_w_in), ops, s5_d, bf(s5_w_glu), s5_b_glu, bf(s5_w_out))
            zero_state = jnp.zeros((bc,) + state_s5.shape[1:], F32)
            ctx, fin = s5_slab_mixer(hc, ctx, mods, i, cs, s5_pack_state(zero_state, slab_groups), *args)
            lat, _ = s5_slab_mixer(hl, lat, mods, i, ls, s5_pack_state(state_s5, slab_groups), *args)
            new_s5 = s5_unpack_state(fin, slab_groups, p)
        elif kind == 1:
            w_gates = bf(jnp.concatenate([rg_wa, rg_wx], axis=-1))
            bias = jnp.stack([rg_ba, rg_bx], axis=1).astype(F32)
            args = (bf(rg_w_in), bf(rg_w_gate), rg_conv_w, rg_conv_b, w_gates, bias, rg_lam, bf(rg_w_out))
            ctx, new_rg = rg_mixer(hc, ctx, mods, i, cs, jnp.zeros((bc, 2, rg_lam.shape[1]), F32), *args)
            lat, _ = rg_mixer(hl, lat, mods, i, ls, state_rglru, *args)
        elif kind == 2:
            lam_init = 0.8 - 0.6 * math.exp(-0.3 * i)
            heads, dv = cache_dattn_v.shape[2], cache_dattn_v.shape[3]
            args = (bf(da_wq).T, bf(da_wk), bf(da_wv), bf(da_wv).T, bf(da_wo), da_lam, da_subln_g, lam_init)
            ctx, (k32, v32) = dattn_mixer(hc, ctx, mods, i, cs, None, *args)
            new_k = k32.reshape(bc, lc, heads, 2, dv // 2)
            new_v = v32.reshape(bc, lc, heads, dv)
            cache = (bf(cache_dattn_k).reshape(-1, heads * dv), bf(cache_dattn_v).transpose(0, 2, 3, 1))
            lat, _ = dattn_mixer(hl, lat, mods, i, ls, cache, *args)
        else:
            heads, dh = ml_wq.shape[0], ml_wq.shape[1]
            args = (bf(ml_w_up), ml_conv_w, ml_conv_b, bf(ml_wq).reshape(heads * dh, dh),
                    bf(ml_wk).reshape(heads * dh, dh), bf(ml_wk).transpose(0, 2, 1).reshape(heads * dh, dh),
                    bf(ml_wv).reshape(heads * dh, dh), bf(ml_w_gate), ml_b_gate,
                    bf(ml_w_o), ml_b_o, ml_gn, ml_skip, bf(ml_w_down))
            ctx, (cf, nf, mf) = mlstm_mixer(hc, ctx, mods, i, cs, None, True, *args)
            init = (state_mlstm_C.astype(F32), state_mlstm_n.astype(F32).reshape(bl, 2, heads, 1, dh),
                    state_mlstm_m.astype(F32).reshape(bl, 2, heads, 1, 1))
            lat, _ = mlstm_mixer(hl, lat, mods, i, ls, init, False, *args)
            new_c, new_n, new_m = cf, nf.reshape(bc, 2, heads, dh), mf.reshape(bc, 2, heads)
        ctx = ffn_half(ctx, norm_g[i, 2], mods, i, 6, *ffn2, cs)
        lat = ffn_half(lat, norm_g[i, 2], mods, i, 6, *ffn2, ls)
    y_prompt = rmsnorm(ctx, final_norm_g).reshape(bc, lc, d)
    y_sample = rmsnorm(lat, final_norm_g).reshape(bl, ll, d)
    return (y_prompt, y_sample, new_s5, new_rg, new_k, new_v, new_c, new_n, new_m)
```

```python
import functools
import math

import jax
import jax.numpy as jnp
from jax import lax
from jax.experimental import pallas as pl
from jax.experimental.pallas import tpu as pltpu

F32 = jnp.float32
BF16 = jnp.bfloat16

NORM_EPS = 1e-6
VMEM_LIMIT_BYTES = 56 * 1024 * 1024
MOD_ROWS = 8


def _cparams(*sem):
    return pltpu.CompilerParams(dimension_semantics=sem, vmem_limit_bytes=VMEM_LIMIT_BYTES)


def _bdot(a, b):
    return jnp.dot(a.astype(BF16), b.astype(BF16), preferred_element_type=F32)


def _mod_row(ref, row):
    return ref[pl.ds(row, 1), :]


class Stream:
    def __init__(self, batch, seq, row0, per_batch_mod):
        self.batch, self.seq, self.row0 = batch, seq, row0
        self.n = batch * seq
        self.rows_per_mod = seq if per_batch_mod else self.n

    def mod_row(self, first_token):
        return self.row0 + first_token // self.rows_per_mod


def _adaln_kernel(c_ref, w_ref, b_ref, o_ref):
    c = c_ref[...]
    s = c * jax.nn.sigmoid(c)
    o_ref[...] = _bdot(s, w_ref[...]) + b_ref[...]


def adaln(c_all, ada_w, ada_b, *, tn=1024):
    nl, d, m = ada_w.shape
    return pl.pallas_call(
        _adaln_kernel,
        out_shape=jax.ShapeDtypeStruct((nl, MOD_ROWS, m), F32),
        grid=(nl, m // tn),
        in_specs=[pl.BlockSpec((MOD_ROWS, d), lambda l, j: (0, 0)),
                  pl.BlockSpec((None, d, tn), lambda l, j: (l, 0, j)),
                  pl.BlockSpec((None, 1, tn), lambda l, j: (l, 0, j))],
        out_specs=pl.BlockSpec((None, MOD_ROWS, tn), lambda l, j: (l, 0, j)),
        compiler_params=_cparams("parallel", "parallel"),
        name="adaln",
    )(c_all, ada_w, ada_b.reshape(nl, 1, m))


def _mod_spec(d, layer, k):
    return pl.BlockSpec((None, MOD_ROWS, d), lambda *_: (layer, 0, k))


def _rms(x, g):
    return x * lax.rsqrt(jnp.mean(x * x, axis=-1, keepdims=True) + NORM_EPS) * g


def _modnorm_kernel(x_ref, g_ref, sh_ref, sc_ref, o_ref, *, stream, tm):
    row = stream.mod_row(pl.program_id(0) * tm)
    y = _rms(x_ref[...], g_ref[...])
    o_ref[...] = (y * (1.0 + _mod_row(sc_ref, row)) + _mod_row(sh_ref, row)).astype(o_ref.dtype)


def modnorm(x, g, mods, layer, k_shift, stream, *, tm=512):
    n, d = x.shape
    return pl.pallas_call(
        functools.partial(_modnorm_kernel, stream=stream, tm=tm),
        out_shape=jax.ShapeDtypeStruct((n, d), BF16),
        grid=(n // tm,),
        in_specs=[pl.BlockSpec((tm, d), lambda i: (i, 0)),
                  pl.BlockSpec((1, d), lambda i: (0, 0)),
                  _mod_spec(d, layer, k_shift),
                  _mod_spec(d, layer, k_shift + 1)],
        out_specs=pl.BlockSpec((tm, d), lambda i: (i, 0)),
        compiler_params=_cparams("parallel"),
        name="modnorm",
    )(x, g.reshape(1, d), mods, mods)


def _rmsnorm_kernel(x_ref, g_ref, o_ref):
    o_ref[...] = _rms(x_ref[...], g_ref[...])


def rmsnorm(x, g, *, tm=512):
    n, d = x.shape
    return pl.pallas_call(
        _rmsnorm_kernel,
        out_shape=jax.ShapeDtypeStruct((n, d), F32),
        grid=(n // tm,),
        in_specs=[pl.BlockSpec((tm, d), lambda i: (i, 0)), pl.BlockSpec((1, d), lambda i: (0, 0))],
        out_specs=pl.BlockSpec((tm, d), lambda i: (i, 0)),
        compiler_params=_cparams("parallel"),
        name="rmsnorm",
    )(x, g.reshape(1, d))


def _ffn_kernel(x_ref, g_ref, sh_ref, sc_ref, gt_ref, w1_ref, w3_ref, w2_ref, o_ref, h_ref, acc_ref,
                *, stream, tm):
    f = pl.program_id(1)
    row = stream.mod_row(pl.program_id(0) * tm)

    @pl.when(f == 0)
    def _():
        y = _rms(x_ref[...], g_ref[...])
        h_ref[...] = (y * (1.0 + _mod_row(sc_ref, row)) + _mod_row(sh_ref, row)).astype(BF16)
        acc_ref[...] = jnp.zeros_like(acc_ref)

    h = h_ref[...]
    a = jnp.dot(h, w1_ref[...], preferred_element_type=F32)
    b = jnp.dot(h, w3_ref[...], preferred_element_type=F32)
    u = (a * jax.nn.sigmoid(a)) * b
    acc_ref[...] += jnp.dot(u.astype(BF16), w2_ref[...], preferred_element_type=F32)

    @pl.when(f == pl.num_programs(1) - 1)
    def _():
        o_ref[...] = x_ref[...] + (0.5 * _mod_row(gt_ref, row)) * acc_ref[...]


def ffn_half(x, g, mods, layer, k_shift, w1, w3, w2, stream, *, tm=512, tf=512):
    n, d = x.shape
    dff = w1.shape[2]
    return pl.pallas_call(
        functools.partial(_ffn_kernel, stream=stream, tm=tm),
        out_shape=jax.ShapeDtypeStruct((n, d), F32),
        grid=(n // tm, dff // tf),
        in_specs=[pl.BlockSpec((tm, d), lambda i, f: (i, 0)),
                  pl.BlockSpec((1, d), lambda i, f: (0, 0)),
                  _mod_spec(d, layer, k_shift),
                  _mod_spec(d, layer, k_shift + 1),
                  _mod_spec(d, layer, k_shift + 2),
                  pl.BlockSpec((None, d, tf), lambda i, f: (layer, 0, f)),
                  pl.BlockSpec((None, d, tf), lambda i, f: (layer, 0, f)),
                  pl.BlockSpec((None, tf, d), lambda i, f: (layer, f, 0))],
        out_specs=pl.BlockSpec((tm, d), lambda i, f: (i, 0)),
        scratch_shapes=[pltpu.VMEM((tm, d), BF16), pltpu.VMEM((tm, d), F32)],
        compiler_params=_cparams("parallel", "arbitrary"),
        name="ffn_half",
    )(x, g.reshape(1, d), mods, mods, mods, w1, w3, w2)


def _mm_kernel(x_ref, w_ref, *rest, epilogue, n_extra, slab_out):
    extra, o_refs = rest[:n_extra], rest[n_extra:]
    w = w_ref[...]
    tm = x_ref.shape[0]
    chunk = min(MM_ROW_CHUNK, tm)
    for r in range(tm // chunk):
        rows = slice(r * chunk, (r + 1) * chunk)
        acc = _bdot(x_ref[rows, :], w)
        outs = epilogue(acc, rows, *extra) if epilogue is not None else acc
        if not isinstance(outs, tuple):
            outs = (outs,)
        for o_ref, o in zip(o_refs, outs):
            if slab_out:
                o_ref[:, rows, :] = o.astype(o_ref.dtype)
            else:
                o_ref[rows, :] = o.astype(o_ref.dtype)


MM_TM = 1024
MM_TN = 1024
MM_K_WIDE = 2048
MM_ROW_CHUNK = 512
MM_TM_DIAG = 2048


def matmul(x, w, *, tm=MM_TM, tn=MM_TN, out_dtype=F32, epilogue=None, extras=(), slab_out=False, name="matmul"):
    n, kb = x.shape
    m = w.shape[1]
    tm = min(tm, n)
    dts = out_dtype if isinstance(out_dtype, tuple) else (out_dtype,)
    if slab_out:
        out_shape = tuple(jax.ShapeDtypeStruct((m // 128, n, 128), dt) for dt in dts)
        out_specs = tuple(pl.BlockSpec((tn // 128, tm, 128), lambda i, j: (j, i, 0)) for _ in dts)
    else:
        out_shape = tuple(jax.ShapeDtypeStruct((n, m), dt) for dt in dts)
        out_specs = tuple(pl.BlockSpec((tm, tn), lambda i, j: (i, j)) for _ in dts)
    res = pl.pallas_call(
        functools.partial(_mm_kernel, epilogue=epilogue, n_extra=len(extras), slab_out=slab_out),
        out_shape=out_shape,
        grid=(n // tm, m // tn),
        in_specs=[pl.BlockSpec((tm, kb), lambda i, j: (i, 0)), pl.BlockSpec((kb, tn), lambda i, j: (0, j))]
        + [pl.BlockSpec(bs, im) for _, bs, im in extras],
        out_specs=out_specs,
        compiler_params=_cparams("parallel", "parallel"),
        name=name,
    )(x, w, *[a for a, _, _ in extras])
    return res if isinstance(out_dtype, tuple) else res[0]


def _mm_t_kernel(x_ref, w_ref, *rest, epilogue, n_extra):
    extra, o_ref = rest[:n_extra], rest[n_extra]
    acc = lax.dot_general(w_ref[...], x_ref[...].astype(BF16), (((1,), (1,)), ((), ())), preferred_element_type=F32)
    o_ref[...] = (epilogue(acc, *extra) if epilogue is not None else acc).astype(o_ref.dtype)


def matmul_t(x, w_t, *, tm=MM_TM, tn=MM_TN, out_dtype=BF16, epilogue=None, extras=(), name="matmul_t"):
    n, kb = x.shape
    m = w_t.shape[0]
    tm = min(tm, n)
    return pl.pallas_call(
        functools.partial(_mm_t_kernel, epilogue=epilogue, n_extra=len(extras)),
        out_shape=jax.ShapeDtypeStruct((m, n), out_dtype),
        grid=(n // tm, m // tn),
        in_specs=[pl.BlockSpec((tm, kb), lambda i, j: (i, 0)), pl.BlockSpec((tn, kb), lambda i, j: (j, 0))]
        + [pl.BlockSpec(bs, im) for _, bs, im in extras],
        out_specs=pl.BlockSpec((tn, tm), lambda i, j: (j, i)),
        compiler_params=_cparams("parallel", "parallel"),
        name=name,
    )(x, w_t, *[a for a, _, _ in extras])


def _residual_epilogue(acc, rows, x_ref, gt_ref, *, stream, tm):
    row = stream.mod_row(pl.program_id(0) * tm)
    return x_ref[rows, :] + _mod_row(gt_ref, row) * acc


def out_proj_residual(z, w, x, mods, layer, stream, *, name="out_proj"):
    d = x.shape[1]
    tm = min(MM_TM, x.shape[0])
    tn = MM_TN if z.shape[1] <= MM_K_WIDE else MM_TN // 2
    return matmul(z, w, tm=tm, tn=tn, name=name,
                  epilogue=functools.partial(_residual_epilogue, stream=stream, tm=tm),
                  extras=[(x, (tm, tn), lambda i, j: (i, j)),
                          (mods, (None, MOD_ROWS, tn), lambda i, j: (layer, 0, 5 * (d // tn) + j))])


def _glu_epilogue(acc, rows, z_ref, b_ref):
    return z_ref[rows, :] * jax.nn.sigmoid(acc + b_ref[...])


SUBLANES = 8
LANES = 128
SCAN_BLOCK = 256
SEQ_BLOCK_ELEMS = 1 << 20


def _conv_rows(x_ref, t0, tb, seq, cw, cb):
    x = x_ref[pl.ds(t0, tb), :]
    prev = x_ref[pl.ds(pl.multiple_of(jnp.maximum(t0 - SUBLANES, 0), SUBLANES), SUBLANES), :]
    nxt = x_ref[pl.ds(pl.multiple_of(jnp.minimum(t0 + tb, seq - SUBLANES), SUBLANES), SUBLANES), :]
    has_prev = t0 > 0
    has_next = t0 + tb < seq
    p1 = jnp.where(has_prev, prev[SUBLANES - 1:SUBLANES, :], 0.0)
    n0 = jnp.where(has_next, nxt[0:1, :], 0.0)
    n1 = jnp.where(has_next, nxt[1:2, :], 0.0)
    rows = lax.broadcasted_iota(jnp.int32, x.shape, 0)
    xm1 = jnp.where(rows == 0, p1, pltpu.roll(x, 1, axis=0))
    xp1 = jnp.where(rows == tb - 1, n0, pltpu.roll(x, tb - 1, axis=0))
    xp2 = jnp.where(rows == tb - 1, n1, jnp.where(rows == tb - 2, n0, pltpu.roll(x, tb - 2, axis=0)))
    return cb + xm1 * cw[0:1, :] + x * cw[1:2, :] + xp1 * cw[2:3, :] + xp2 * cw[3:4, :]


def _group_scan(a, b, reverse):
    tb, tc = a.shape
    a = a.reshape(tb // SUBLANES, SUBLANES, tc)
    b = b.reshape(tb // SUBLANES, SUBLANES, tc)
    pos = lax.broadcasted_iota(jnp.int32, a.shape, 1)
    s = 1
    while s < SUBLANES:
        ok = pos < SUBLANES - s if reverse else pos >= s
        shift = SUBLANES - s if reverse else s
        a_s, b_s = pltpu.roll(a, shift, axis=1), pltpu.roll(b, shift, axis=1)
        b = b + jnp.where(ok, a, 0.0) * b_s
        a = jnp.where(ok, a * a_s, a)
        s *= 2
    return a.reshape(tb, tc), b.reshape(tb, tc)


def _softplus(x):
    return jnp.maximum(x, 0.0) + jnp.log1p(jnp.exp(-jnp.abs(x)))


SLAB_CHUNK = 8
S5_RECUR_ROWS = 512


def _s5_slab_operators(a_re, a_im, log_dt, b_re, b_im, c_re, c_im):
    hp = lax.Precision.HIGHEST
    t = SLAB_CHUNK
    a_re, a_im, b_re, b_im, c_re, c_im = (z.astype(F32) for z in (a_re, a_im, b_re, b_im, c_re, c_im))
    ng, p = a_re.shape[1], a_re.shape[2]
    c = b_re.shape[3]
    gs = LANES // c
    ns = ng // gs
    dt = jnp.exp(log_dt.astype(F32))[..., None]
    mag = jnp.exp(a_re * dt)
    lr = mag * jnp.cos(a_im * dt)
    li = mag * jnp.sin(a_im * dt)
    den = a_re * a_re + a_im * a_im
    cr = ((lr - 1.0) * a_re + li * a_im) / den
    ci = (li * a_re - (lr - 1.0) * a_im) / den
    bbr = cr[..., None] * b_re - ci[..., None] * b_im
    bbi = cr[..., None] * b_im + ci[..., None] * b_re

    def powers(k):
        k = k.astype(F32)[None, None, :, None]
        pm = jnp.exp(k * (a_re * dt)[:, :, None, :])
        return pm * jnp.cos(k * (a_im * dt)[:, :, None, :]), pm * jnp.sin(k * (a_im * dt)[:, :, None, :])

    pr, pi = powers(jnp.arange(t + 1))
    pbr = (pr[:, :, :, :, None] * bbr[:, :, None] - pi[:, :, :, :, None] * bbi[:, :, None])
    pbi = (pr[:, :, :, :, None] * bbi[:, :, None] + pi[:, :, :, :, None] * bbr[:, :, None])
    flat = lambda z: z.transpose(0, 1, 3, 2, 4).reshape(2, ng, p, (t + 1) * c)
    m = (jnp.einsum('dgop,dgpx->dgox', c_re, flat(pbr), precision=hp)
         - jnp.einsum('dgop,dgpx->dgox', c_im, flat(pbi), precision=hp)).reshape(2, ng, c, t + 1, c)

    def block_diag(small, row_group, col_group, col_inner):
        rows, q = small.shape[-2], small.shape[-1]
        cols = jnp.arange(gs * q)
        expand = (jnp.arange(q)[:, None] == col_inner(cols)[None, :]).astype(BF16)
        wide = jnp.matmul(small.astype(BF16), expand, preferred_element_type=BF16)
        keep = row_group(jnp.arange(rows))[:, None] == col_group(cols)[None, :]
        return jnp.where(keep, wide, jnp.zeros((), BF16))

    s_idx = jnp.arange(t)[:, None]
    t_idx = jnp.arange(t)[None, :]
    lag = jnp.stack([jnp.clip(t_idx - s_idx, 0, t), jnp.clip(s_idx - t_idx, 0, t)])
    live = jnp.stack([t_idx >= s_idx, s_idx >= t_idx]).astype(F32)
    msel = sum((m[d][:, :, lag[d], :] * live[d][None, None, :, :, None]).transpose(0, 2, 3, 4, 1) for d in range(2))
    msel = msel.reshape(ns, gs, t, t, c, c).transpose(0, 2, 1, 4, 3, 5)
    in_group = lambda r: (r % LANES) // c
    out_col = lambda col: (col // LANES) * c + col % c
    state_group = lambda r: (r % (gs * p)) // p
    state_col = lambda col: (col // (gs * p)) * p + col % p
    toep = block_diag(msel.reshape(ns, t * LANES, t * c), in_group, in_group, out_col)
    pow_s = jnp.stack([t - 1 - jnp.arange(t), jnp.arange(t)])
    pb = jnp.stack([pbr, pbi], axis=2)
    s_sel = jnp.stack([pb[d][:, :, pow_s[d]] for d in range(2)])
    s_sel = s_sel.reshape(2, ns, gs, 2, t, p, c).transpose(0, 1, 4, 2, 6, 3, 5)
    s_op = block_diag(s_sel.reshape(2, ns, t * LANES, 2 * p), in_group, state_group, state_col)
    pow_w = jnp.stack([jnp.arange(t) + 1, t - jnp.arange(t)])
    zr = jnp.stack([c_re[d][:, None] * pr[d][:, pow_w[d]][:, :, None, :]
                    - c_im[d][:, None] * pi[d][:, pow_w[d]][:, :, None, :] for d in range(2)])
    zi = jnp.stack([c_re[d][:, None] * pi[d][:, pow_w[d]][:, :, None, :]
                    + c_im[d][:, None] * pr[d][:, pow_w[d]][:, :, None, :] for d in range(2)])
    w_sel = jnp.stack([zr, -zi], axis=2)
    w_sel = w_sel.reshape(2, ns, gs, 2, t, c, p).transpose(0, 1, 3, 2, 6, 4, 5)
    w_op = block_diag(w_sel.reshape(2, ns, 2 * gs * p, t * c), state_group, in_group, out_col)
    qr, qi = powers(t * (jnp.arange(SUBLANES) + 1))
    lanes = lambda z: z.reshape(2, ns, gs, SUBLANES, p).transpose(0, 1, 3, 2, 4).reshape(2, ns, SUBLANES, gs * p)
    pow_a = jnp.concatenate([lanes(qr), lanes(qr)], axis=-1)
    pow_b = jnp.concatenate([-lanes(qi), lanes(qi)], axis=-1)
    return toep, s_op, w_op, pow_a, pow_b


def _s5_state_kernel(u_ref, s_ref, o_ref):
    o_ref[...] = jnp.dot(u_ref[...], s_ref[...], preferred_element_type=F32)


def _s5_recur_slab_kernel(st_ref, pa_ref, pb_ref, h0_ref, hin_ref, fin_ref, loc_ref, *, nj):
    rows, w = loc_ref.shape
    half = w // 2
    ng = nj // SUBLANES
    swap = lambda z: jnp.concatenate([z[..., half:], z[..., :half]], axis=-1)
    for d in range(2):
        reverse = d == 1
        pa = pa_ref[d]
        pb = pb_ref[d]
        z = st_ref[d].reshape(rows // SUBLANES, SUBLANES, w)
        pos = lax.broadcasted_iota(jnp.int32, z.shape, 1)
        s = 1
        while s < SUBLANES:
            ok = pos < SUBLANES - s if reverse else pos >= s
            zs = pltpu.roll(z, SUBLANES - s if reverse else s, axis=1)
            z = z + jnp.where(ok, zs * pa[s - 1:s, :] + swap(zs) * pb[s - 1:s, :], 0.0)
            s *= 2
        loc_ref[...] = z.reshape(rows, w)
        row = lax.broadcasted_iota(jnp.int32, (SUBLANES, w), 0)
        if reverse:
            ca = jnp.concatenate([pa[SUBLANES - 1 - r:SUBLANES - r, :] for r in range(SUBLANES)], axis=0)
            cb = jnp.concatenate([pb[SUBLANES - 1 - r:SUBLANES - r, :] for r in range(SUBLANES)], axis=0)
        else:
            ca, cb = pa, pb

        for b in range(rows // nj):
            def group(i, carry, d=d, b=b, reverse=reverse, ca=ca, cb=cb):
                c, c_sw = carry
                r0 = pl.multiple_of(b * nj + (ng - 1 - i if reverse else i) * SUBLANES, SUBLANES)
                h = loc_ref[pl.ds(r0, SUBLANES), :] + c * ca + c_sw * cb
                shifted = pltpu.roll(h, SUBLANES - 1 if reverse else 1, axis=0)
                hin_ref[d, pl.ds(r0, SUBLANES), :] = jnp.where(row == (SUBLANES - 1 if reverse else 0), c, shifted)
                last = h[0:1, :] if reverse else h[SUBLANES - 1:SUBLANES, :]
                return last, swap(last)

            h0 = h0_ref[d, b]
            fin_ref[d, b] = lax.fori_loop(0, ng, group, (h0, swap(h0)))[0]


def _s5_out_kernel(u_ref, toep_ref, hin_ref, w_ref, o_ref):
    acc = jnp.dot(u_ref[...], toep_ref[...], preferred_element_type=F32)
    for d in range(2):
        acc = acc + jnp.dot(hin_ref[d].astype(BF16), w_ref[d], preferred_element_type=F32)
    o_ref[...] = acc


def s5_slab_scan(u16, h0, ops, batch, seq):
    toep, s_op, w_op, pow_a, pow_b = ops
    ns, n, _ = u16.shape
    t = SLAB_CHUNK
    kc = t * LANES
    w = s_op.shape[-1]
    nj = seq // t
    rows = n // t
    tr = min(rows, 1024)
    uc = u16.reshape(ns, rows, kc)
    st = pl.pallas_call(
        _s5_state_kernel,
        out_shape=jax.ShapeDtypeStruct((2, ns, rows, w), F32),
        grid=(ns, 2, rows // tr),
        in_specs=[pl.BlockSpec((None, tr, kc), lambda a, d, r: (a, r, 0)),
                  pl.BlockSpec((None, None, kc, w), lambda a, d, r: (d, a, 0, 0))],
        out_specs=pl.BlockSpec((None, None, tr, w), lambda a, d, r: (d, a, r, 0)),
        compiler_params=_cparams("parallel", "parallel", "parallel"),
        name="s5_state",
    )(uc, s_op)
    bb = max(1, min(batch, S5_RECUR_ROWS // nj))
    hin, fin = pl.pallas_call(
        functools.partial(_s5_recur_slab_kernel, nj=nj),
        out_shape=(jax.ShapeDtypeStruct((2, ns, rows, w), F32), jax.ShapeDtypeStruct((2, ns, batch, 1, w), F32)),
        grid=(ns, batch // bb),
        in_specs=[pl.BlockSpec((2, None, bb * nj, w), lambda a, b: (0, a, b, 0)),
                  pl.BlockSpec((2, None, SUBLANES, w), lambda a, b: (0, a, 0, 0)),
                  pl.BlockSpec((2, None, SUBLANES, w), lambda a, b: (0, a, 0, 0)),
                  pl.BlockSpec((2, None, bb, 1, w), lambda a, b: (0, a, b, 0, 0))],
        out_specs=(pl.BlockSpec((2, None, bb * nj, w), lambda a, b: (0, a, b, 0)),
                   pl.BlockSpec((2, None, bb, 1, w), lambda a, b: (0, a, b, 0, 0))),
        scratch_shapes=[pltpu.VMEM((bb * nj, w), F32)],
        compiler_params=_cparams("parallel", "parallel"),
        name="s5_recur",
    )(st, pow_a, pow_b, h0)
    y = pl.pallas_call(
        _s5_out_kernel,
        out_shape=jax.ShapeDtypeStruct((ns, rows, kc), F32),
        grid=(ns, rows // tr),
        in_specs=[pl.BlockSpec((None, tr, kc), lambda a, r: (a, r, 0)),
                  pl.BlockSpec((None, kc, kc), lambda a, r: (a, 0, 0)),
                  pl.BlockSpec((2, None, tr, w), lambda a, r: (0, a, r, 0)),
                  pl.BlockSpec((2, None, w, kc), lambda a, r: (0, a, 0, 0))],
        out_specs=pl.BlockSpec((None, tr, kc), lambda a, r: (a, r, 0)),
        compiler_params=_cparams("parallel", "parallel"),
        name="s5_out_chunks",
    )(uc, toep, hin, w_op)
    return y.reshape(ns, n, LANES), fin


def _slab_epilogue(acc, rows):
    parts = jnp.stack([acc[:, s * LANES:(s + 1) * LANES] for s in range(acc.shape[1] // LANES)])
    return parts, parts


def _s5_post_slab_kernel(u_ref, y_ref, d_ref, o_ref):
    for s in range(u_ref.shape[0]):
        cols = slice(s * LANES, (s + 1) * LANES)
        o_ref[:, cols] = jax.nn.gelu(d_ref[:, cols] * u_ref[s] + y_ref[s])


def s5_post_slab(u, y, d_skip, *, tm=512):
    ns, n, _ = u.shape
    w = ns * LANES
    slab = pl.BlockSpec((ns, tm, LANES), lambda i: (0, i, 0))
    return pl.pallas_call(
        _s5_post_slab_kernel,
        out_shape=jax.ShapeDtypeStruct((n, w), F32),
        grid=(n // tm,),
        in_specs=[slab, slab, pl.BlockSpec((1, w), lambda i: (0, 0))],
        out_specs=pl.BlockSpec((tm, w), lambda i: (i, 0)),
        compiler_params=_cparams("parallel"),
        name="s5_post",
    )(u, y, d_skip.reshape(1, w).astype(F32))


def s5_slab_mixer(h, x, mods, layer, stream, state, w_in, ops, d_skip, w_glu, b_glu, w_out):
    u, u16 = matmul(h, w_in, out_dtype=(F32, BF16), epilogue=_slab_epilogue, slab_out=True, name="s5_in")
    y, fin = s5_slab_scan(u16, state, ops, stream.batch, stream.seq)
    z = s5_post_slab(u, y, d_skip)
    w = z.shape[1]
    zz = matmul(z, w_glu, out_dtype=BF16, name="s5_glu", epilogue=_glu_epilogue,
                extras=[(z, (min(MM_TM, z.shape[0]), MM_TN), lambda i, j: (i, j)),
                        (b_glu.reshape(1, w).astype(F32), (1, MM_TN), lambda i, j: (0, j))])
    return out_proj_residual(zz, w_out, x, mods, layer, stream, name="s5_out"), fin


def s5_pack_state(state, slab_groups):
    b, _, _, g, p = state.shape
    s = state.astype(F32).reshape(b, 2, 2, g // slab_groups, slab_groups * p)
    return s.transpose(1, 3, 0, 2, 4).reshape(2, g // slab_groups, b, 1, 2 * slab_groups * p)


def s5_unpack_state(fin, slab_groups, p):
    _, ns, b, _, w = fin.shape
    s = fin.reshape(2, ns, b, 2, slab_groups * p).transpose(2, 0, 3, 1, 4)
    return s.reshape(b, 2, 2, ns * slab_groups, p)


RG_C = 8.0


def _rg_core_kernel(x_ref, gate_ref, cw_ref, cb_ref, w_ref, bias_ref, lam_ref, h0_ref, o_ref, fin_ref,
                    y_ref, xc_ref, a_ref, b_ref, *, seq):
    tb = a_ref.shape[0]
    tc = a_ref.shape[1]
    n_blocks = seq // tb
    n_groups = tb // SUBLANES
    cw = cw_ref[...]
    cb = cb_ref[...]
    for d in range(2):
        reverse = d == 1
        sp = _softplus(-lam_ref[d:d + 1, :])
        bias_a = bias_ref[d, 0:1, :]
        bias_x = bias_ref[d, 1:2, :]

        def block(k, carry, d=d, reverse=reverse, sp=sp, bias_a=bias_a, bias_x=bias_x):
            t0 = pl.multiple_of((n_blocks - 1 - k if reverse else k) * tb, tb)
            if reverse:
                xc = xc_ref[pl.ds(t0, tb), :]
            else:
                xc = _conv_rows(x_ref, t0, tb, seq, cw, cb)
                xc_ref[pl.ds(t0, tb), :] = xc
            pre = [_bdot(xc[:, n * LANES:(n + 1) * LANES], w_ref[d, n]) for n in range(tc // LANES)]
            r = jax.nn.sigmoid(jnp.concatenate([p[:, :LANES] for p in pre], axis=1) + bias_a)
            ig = jax.nn.sigmoid(jnp.concatenate([p[:, LANES:] for p in pre], axis=1) + bias_x)
            log_a = -RG_C * r * sp
            a = jnp.exp(log_a)
            bv = jnp.sqrt(-jnp.tanh(log_a) * (a * a + 1.0)) * (ig * xc)
            a_ref[...], b_ref[...] = _group_scan(a, bv, reverse)

            def group(i, c):
                r0 = pl.multiple_of((n_groups - 1 - i if reverse else i) * SUBLANES, SUBLANES)
                h = b_ref[pl.ds(r0, SUBLANES), :] + a_ref[pl.ds(r0, SUBLANES), :] * c
                rows = pl.ds(pl.multiple_of(t0 + r0, SUBLANES), SUBLANES)
                if reverse:
                    y_ref[rows, :] += h
                    return h[0:1, :]
                y_ref[rows, :] = h
                return h[SUBLANES - 1:SUBLANES, :]

            carry = lax.fori_loop(0, n_groups, group, carry)
            if reverse:
                o_ref[pl.ds(t0, tb), :] = (y_ref[pl.ds(t0, tb), :] * gate_ref[pl.ds(t0, tb), :]).astype(o_ref.dtype)
            return carry

        fin_ref[d:d + 1, :] = lax.fori_loop(0, n_blocks, block, h0_ref[d:d + 1, :])


def rg_core(xin, gate, conv_w, conv_b, w_gates, bias, lam, h0, stream, *, tc=256):
    n, c = xin.shape
    seq, batch = stream.seq, stream.batch
    tb = min(SCAN_BLOCK, seq)
    nb = tc // LANES
    return pl.pallas_call(
        functools.partial(_rg_core_kernel, seq=seq),
        out_shape=(jax.ShapeDtypeStruct((n, c), BF16), jax.ShapeDtypeStruct((batch, 2, c), F32)),
        grid=(batch, c // tc),
        in_specs=[pl.BlockSpec((seq, tc), lambda b, j: (b, j)),
                  pl.BlockSpec((seq, tc), lambda b, j: (b, j)),
                  pl.BlockSpec((4, tc), lambda b, j: (0, j)),
                  pl.BlockSpec((1, tc), lambda b, j: (0, j)),
                  pl.BlockSpec((2, nb, LANES, 2 * LANES), lambda b, j: (0, j, 0, 0)),
                  pl.BlockSpec((2, 2, tc), lambda b, j: (0, 0, j)),
                  pl.BlockSpec((2, tc), lambda b, j: (0, j)),
                  pl.BlockSpec((None, 2, tc), lambda b, j: (b, 0, j))],
        out_specs=(pl.BlockSpec((seq, tc), lambda b, j: (b, j)),
                   pl.BlockSpec((None, 2, tc), lambda b, j: (b, 0, j))),
        scratch_shapes=[pltpu.VMEM((seq, tc), F32), pltpu.VMEM((seq, tc), F32),
                        pltpu.VMEM((tb, tc), F32), pltpu.VMEM((tb, tc), F32)],
        compiler_params=_cparams("parallel", "parallel"),
        name="rg_core",
    )(xin, gate, conv_w, conv_b.reshape(1, c), w_gates, bias, lam, h0)


def _gelu_epilogue(acc, rows):
    return jax.nn.gelu(acc)


def rg_mixer(h, x, mods, layer, stream, state, w_in, w_gate, conv_w, conv_b, w_gates, bias, lam, w_out):
    xin = matmul(h, w_in, name="rg_in")
    gate = matmul(h, w_gate, epilogue=_gelu_epilogue, name="rg_gate")
    yg, fin = rg_core(xin, gate, conv_w, conv_b, w_gates, bias, lam, state, stream)
    return out_proj_residual(yg, w_out, x, mods, layer, stream, name="rg_out"), fin


GRID_W = 64
ROPE_BASE = 10000.0
ATTN_TQ = 512
ATTN_TK = 512


def rope_tables(seq, dk):
    half = dk // 2
    quarter = half // 2
    pos = jnp.arange(seq)
    inv = ROPE_BASE ** (-jnp.arange(0, half, 2, dtype=F32) / half)
    ang_r = (pos // GRID_W).astype(F32)[:, None] * inv
    ang_c = (pos % GRID_W).astype(F32)[:, None] * inv
    cos = jnp.concatenate([jnp.cos(ang_r)] * 2 + [jnp.cos(ang_c)] * 2, axis=1)
    sin = jnp.concatenate([-jnp.sin(ang_r), jnp.sin(ang_r), -jnp.sin(ang_c), jnp.sin(ang_c)], axis=1)
    assert cos.shape == (seq, 4 * quarter)
    return cos, sin


def _rope_epilogue(acc, rows, cos_ref, sin_ref):
    cos = cos_ref[rows, :]
    sin = sin_ref[rows, :]
    dk = cos.shape[1]
    lane = lax.broadcasted_iota(jnp.int32, cos.shape, 1)
    first = (lane % (dk // 2)) < dk // 4
    outs = []
    for g in range(acc.shape[1] // dk):
        x = acc[:, g * dk:(g + 1) * dk]
        rot = jnp.where(first, pltpu.roll(x, dk - dk // 4, axis=1), pltpu.roll(x, dk // 4, axis=1))
        outs.append(x * cos + rot * sin)
    return jnp.concatenate(outs, axis=1)


def _attn_t_kernel(*refs, n_tiles, has_cache, lam_init, scale):
    if has_cache:
        qt_ref, k_ref, vt_ref, ck_ref, cvt_ref, lam_ref, g_ref, o_ref, s_ref, sc_ref = refs
    else:
        qt_ref, k_ref, vt_ref, lam_ref, g_ref, o_ref, s_ref = refs
    tq = qt_ref.shape[1]
    dk = qt_ref.shape[0] // 2
    dv = vt_ref.shape[0]
    tk = s_ref.shape[2]
    maps = range(2)
    qt = [qt_ref[c * dk:(c + 1) * dk, :] for c in maps]

    def scores(kt):
        return [jnp.dot(kt[:, c * dk:(c + 1) * dk], qt[c], preferred_element_type=F32) for c in maps]

    m = [jnp.full((1, tq), -jnp.inf, F32) for _ in maps]
    if has_cache:
        s = scores(ck_ref[...])
        for c in maps:
            sc_ref[c] = s[c]
            m[c] = jnp.maximum(m[c], jnp.max(s[c], axis=0, keepdims=True))

    for i in range(n_tiles):
        s = scores(k_ref[i * tk:(i + 1) * tk, :])
        for c in maps:
            s_ref[c, i] = s[c]
            m[c] = jnp.maximum(m[c], jnp.max(s[c], axis=0, keepdims=True))

    c2 = scale * math.log2(math.e)

    def accumulate(s, vt, st):
        new = []
        for c in maps:
            l, acc = st[c]
            p = jnp.exp2((s[c] - m[c]) * c2)
            new.append((l + jnp.sum(p, axis=0, keepdims=True),
                        acc + jnp.dot(vt, p.astype(BF16), preferred_element_type=F32)))
        return tuple(new)

    st = tuple((jnp.zeros((1, tq), F32), jnp.zeros((dv, tq), F32)) for _ in maps)
    if has_cache:
        st = accumulate([sc_ref[c] for c in maps], cvt_ref[...], st)

    for i in range(n_tiles):
        st = accumulate([s_ref[c, i] for c in maps], vt_ref[:, i * tk:(i + 1) * tk], st)
    lp = lam_ref[...]
    lam = (jnp.exp(jnp.sum(lp[0:1, :] * lp[1:2, :], axis=-1, keepdims=True))
           - jnp.exp(jnp.sum(lp[2:3, :] * lp[3:4, :], axis=-1, keepdims=True)) + lam_init)
    diff = st[0][1] / st[0][0] - lam * (st[1][1] / st[1][0])
    diff = diff * lax.rsqrt(jnp.mean(diff * diff, axis=0, keepdims=True) + NORM_EPS)
    o_ref[...] = (diff.T * g_ref[...] * (1.0 - lam_init)).astype(o_ref.dtype)


def diff_attention_t(q_t, k, v_t, *rest, stream, lam_init):
    has_cache = len(rest) == 4
    da_lam, subln_g = rest[-2], rest[-1]
    n = k.shape[0]
    dv = subln_g.shape[0]
    heads = v_t.shape[0] // dv
    batch, seq = stream.batch, stream.seq
    tq = min(ATTN_TQ, seq)
    tk = min(ATTN_TK, seq)
    n_tiles = seq // tk
    nq = seq // tq
    in_specs = [pl.BlockSpec((dv, tq), lambda b, h, i: (h, b * nq + i)),
                pl.BlockSpec((seq, dv), lambda b, h, i: (b, h)),
                pl.BlockSpec((dv, seq), lambda b, h, i: (h, b))]
    args = [q_t, k, v_t]
    scratch = [pltpu.VMEM((2, n_tiles, tk, tq), F32)]
    if has_cache:
        ck, cv_t = rest[0], rest[1]
        past = cv_t.shape[3]
        in_specs += [pl.BlockSpec((past, dv), lambda b, h, i: (b, h)),
                     pl.BlockSpec((None, None, dv, past), lambda b, h, i: (b, h, 0, 0))]
        args += [ck, cv_t]
        scratch += [pltpu.VMEM((2, past, tq), F32)]
    in_specs += [pl.BlockSpec(da_lam.shape, lambda b, h, i: (0, 0)), pl.BlockSpec((1, dv), lambda b, h, i: (0, 0))]
    args += [da_lam.astype(F32), subln_g.reshape(1, dv).astype(F32)]
    return pl.pallas_call(
        functools.partial(_attn_t_kernel, n_tiles=n_tiles, has_cache=has_cache, lam_init=lam_init,
                          scale=1.0 / math.sqrt(dv // 2)),
        out_shape=jax.ShapeDtypeStruct((n, heads * dv), BF16),
        grid=(batch, heads, nq),
        in_specs=in_specs,
        out_specs=pl.BlockSpec((tq, dv), lambda b, h, i: (b * nq + i, h)),
        scratch_shapes=scratch,
        compiler_params=_cparams("parallel", "parallel", "parallel"),
        name="diff_attention",
    )(*args)


def _rope_t_epilogue(acc, cos_ref, sin_ref):
    cos = cos_ref[...]
    sin = sin_ref[...]
    dk = cos.shape[0]
    row = lax.broadcasted_iota(jnp.int32, cos.shape, 0)
    first = (row % (dk // 2)) < dk // 4
    outs = []
    for g in range(acc.shape[0] // dk):
        x = acc[g * dk:(g + 1) * dk, :]
        rot = jnp.where(first, pltpu.roll(x, dk - dk // 4, axis=0), pltpu.roll(x, dk // 4, axis=0))
        outs.append(x * cos + rot * sin)
    return jnp.concatenate(outs, axis=0)


def _dup_epilogue(acc, rows):
    return acc, acc


def dattn_mixer(h, x, mods, layer, stream, cache, wq_t, wk, wv, wv_t, wo, da_lam, subln_g, lam_init):
    dk = subln_g.shape[0] // 2
    v_t = matmul_t(h, wv_t, name="da_v_t")
    if cache is None:
        q_t = matmul_t(h, wq_t, name="da_q_t")
        k32, k = matmul(h, wk, out_dtype=(F32, BF16), epilogue=_dup_epilogue, name="da_k")
        v32 = matmul(h, wv, name="da_v")
        o = diff_attention_t(q_t, k, v_t, da_lam, subln_g, stream=stream, lam_init=lam_init)
        new = (k32, v32)
    else:
        cos, sin = rope_tables(stream.seq, dk)
        tm = min(MM_TM, stream.seq)
        per = stream.seq // tm
        rope = [(cos, (tm, dk), lambda i, j: (i % per, 0)), (sin, (tm, dk), lambda i, j: (i % per, 0))]
        rope_t = [(cos.T, (dk, tm), lambda i, j: (0, i % per)), (sin.T, (dk, tm), lambda i, j: (0, i % per))]
        q_t = matmul_t(h, wq_t, tm=tm, epilogue=_rope_t_epilogue, extras=rope_t, name="da_q_t_rope")
        k = matmul(h, wk, tm=tm, out_dtype=BF16, epilogue=_rope_epilogue, extras=rope, name="da_k_rope")
        o = diff_attention_t(q_t, k, v_t, cache[0], cache[1], da_lam, subln_g, stream=stream, lam_init=lam_init)
        new = None
    return out_proj_residual(o, wo, x, mods, layer, stream, name="da_out"), new


ML_CHUNK = 256


def _ml_conv_kernel(x_ref, cw_ref, cb_ref, o32_ref, o16_ref, *, seq):
    tb = min(SCAN_BLOCK, seq)
    cw = cw_ref[...]
    cb = cb_ref[...]

    def body(i, carry):
        t0 = pl.multiple_of(i * tb, tb)
        y = _conv_rows(x_ref, t0, tb, seq, cw, cb)
        y = y * jax.nn.sigmoid(y)
        o32_ref[pl.ds(t0, tb), :] = y
        o16_ref[pl.ds(t0, tb), :] = y.astype(BF16)
        return carry

    lax.fori_loop(0, seq // tb, body, 0)


def ml_conv(xi, conv_w, conv_b, stream):
    n, w = xi.shape
    seq = stream.seq
    tc = min(w, 1024, SEQ_BLOCK_ELEMS // seq)
    return pl.pallas_call(
        functools.partial(_ml_conv_kernel, seq=seq),
        out_shape=(jax.ShapeDtypeStruct((n, w), F32), jax.ShapeDtypeStruct((n, w), BF16)),
        grid=(stream.batch, w // tc),
        in_specs=[pl.BlockSpec((seq, tc), lambda b, j: (b, j)),
                  pl.BlockSpec((conv_w.shape[0], tc), lambda b, j: (0, j)),
                  pl.BlockSpec((1, tc), lambda b, j: (0, j))],
        out_specs=(pl.BlockSpec((seq, tc), lambda b, j: (b, j)), pl.BlockSpec((seq, tc), lambda b, j: (b, j))),
        compiler_params=_cparams("parallel", "parallel"),
        name="ml_conv",
    )(xi, conv_w.astype(F32), conv_b.reshape(1, w).astype(F32))


def _ml_qkv_kernel(xc_ref, xi_ref, wq_ref, wk_ref, wkt_ref, wv_ref, q_ref, k_ref, kt_ref, v_ref, *, scale):
    xc = xc_ref[...]
    q_ref[...] = jnp.dot(xc, wq_ref[...], preferred_element_type=F32).astype(q_ref.dtype)
    k_ref[...] = (jnp.dot(xc, wk_ref[...], preferred_element_type=F32) * scale).astype(k_ref.dtype)
    kt = lax.dot_general(wkt_ref[...], xc, (((1,), (1,)), ((), ())), preferred_element_type=F32)
    kt_ref[...] = (kt * scale).astype(kt_ref.dtype)
    v_ref[...] = _bdot(xi_ref[...], wv_ref[...]).astype(v_ref.dtype)


def ml_qkv(xc16, xi, wq, wk, wk_t, wv):
    n, w = xc16.shape
    dh = wq.shape[1]
    tm = min(MM_TM_DIAG, n)
    tok = pl.BlockSpec((tm, dh), lambda i, h: (i, h))
    wsp = pl.BlockSpec((dh, dh), lambda i, h: (h, 0))
    tokens = jax.ShapeDtypeStruct((n, w), BF16)
    return pl.pallas_call(
        functools.partial(_ml_qkv_kernel, scale=1.0 / math.sqrt(dh)),
        out_shape=(tokens, tokens, jax.ShapeDtypeStruct((w, n), BF16), tokens),
        grid=(n // tm, w // dh),
        in_specs=[tok, tok, wsp, wsp, wsp, wsp],
        out_specs=(tok, tok, pl.BlockSpec((dh, tm), lambda i, h: (h, i)), tok),
        compiler_params=_cparams("parallel", "parallel"),
        name="ml_qkv",
    )(xc16, xi, wq, wk, wk_t, wv)


def _ml_gates_kernel(q_ref, k_ref, v_ref, w_ref, b_ref, o_ref, *, n_heads):
    acc = jnp.dot(q_ref[...], w_ref[0], preferred_element_type=F32)
    acc = acc + jnp.dot(k_ref[...], w_ref[1], preferred_element_type=F32)
    acc = acc + jnp.dot(v_ref[...], w_ref[2], preferred_element_type=F32) + b_ref[...]
    col = lax.broadcasted_iota(jnp.int32, acc.shape, 1)
    is_forget = (col // n_heads) % 2 == 1
    o_ref[...] = jnp.where(is_forget, -_softplus(-acc), acc)


def ml_gates(q, k, v, w_gate, b_gate, n_heads, *, tm=512):
    n, w = q.shape
    ng = w_gate.shape[2]
    row = pl.BlockSpec((tm, w), lambda i: (i, 0))
    return pl.pallas_call(
        functools.partial(_ml_gates_kernel, n_heads=n_heads),
        out_shape=jax.ShapeDtypeStruct((n, ng), F32),
        grid=(n // tm,),
        in_specs=[row, row, row, pl.BlockSpec((3, w, ng), lambda i: (0, 0, 0)), pl.BlockSpec((1, ng), lambda i: (0, 0))],
        out_specs=pl.BlockSpec((tm, ng), lambda i: (i, 0)),
        compiler_params=_cparams("parallel"),
        name="ml_gates",
    )(q, k, v, w_gate, b_gate.reshape(1, ng).astype(F32))


ML_HEADS_PER_STEP = 2


def _mlstm_cell_kernel(*refs, n_heads, has_init, want_final):
    q_ref, k_ref, kt_ref, v_ref, g_ref, gt_ref = refs[:6]
    init_refs = refs[6:9] if has_init else None
    h_ref = refs[9 if has_init else 6]
    ct_s, n_s, m_s = refs[-3:]
    final_refs = refs[-6:-3] if want_final else None
    hb = ct_s.shape[0]
    dh = ct_s.shape[1]
    h0 = pl.program_id(1) * hb
    d = pl.program_id(2)
    ci = pl.program_id(3)
    reverse = d == 1
    t = q_ref.shape[0]

    @pl.when(ci == 0)
    def _():
        for j in range(hb):
            if has_init:
                ct_s[j] = init_refs[0][j].T
                n_s[j] = init_refs[1][j]
                m_s[j] = init_refs[2][j]
            else:
                ct_s[j] = jnp.zeros((dh, dh), F32)
                n_s[j] = jnp.zeros((1, dh), F32)
                m_s[j] = jnp.zeros((1, 1), F32)

    g = g_ref[...]
    gt = gt_ref[...]
    lane = lax.broadcasted_iota(jnp.int32, g.shape, 1)
    sub = lax.broadcasted_iota(jnp.int32, gt.shape, 0)
    r_idx = lax.broadcasted_iota(jnp.int32, (t, t), 0)
    s_idx = lax.broadcasted_iota(jnp.int32, (t, t), 1)
    ahead = (s_idx - r_idx) * jnp.where(reverse, -1, 1)
    mask = ahead <= 0
    mask_t = ahead >= 0
    row = lax.broadcasted_iota(jnp.int32, (t, 1), 0)
    last = row == jnp.where(reverse, 0, t - 1)

    for j in range(hb):
        cols = slice(j * dh, (j + 1) * dh)
        q = q_ref[:, cols]
        k = k_ref[:, cols]
        v = v_ref[:, cols]
        kt = kt_ref[cols, :]
        col_i = d * (2 * n_heads) + h0 + j
        col_f = col_i + n_heads
        ig_col = jnp.sum(jnp.where(lane == col_i, g, 0.0), axis=1, keepdims=True)
        lf_col = jnp.sum(jnp.where(lane == col_f, g, 0.0), axis=1, keepdims=True)
        ig_row = jnp.sum(jnp.where(sub == col_i, gt, 0.0), axis=0, keepdims=True)
        lf_row = jnp.sum(jnp.where(sub == col_f, gt, 0.0), axis=0, keepdims=True)
        b_col = jnp.sum(jnp.where(mask, lf_row, 0.0), axis=1, keepdims=True)
        b_row = jnp.sum(jnp.where(mask_t, lf_col, 0.0), axis=0, keepdims=True)
        m_prev = m_s[j]
        dlog = jnp.where(mask, b_col - b_row + ig_row, -jnp.inf)
        inter = b_col + m_prev
        m_t = jnp.maximum(inter, jnp.max(dlog, axis=1, keepdims=True))
        w_intra = jnp.exp(dlog - m_t)
        w_inter = jnp.exp(inter - m_t)
        s_mat = lax.dot_general(q, k, (((1,), (1,)), ((), ())), preferred_element_type=F32) * w_intra
        ct_old = ct_s[j]
        n_old = n_s[j]
        qc = jnp.dot(q, ct_old.astype(BF16), preferred_element_type=F32)
        num = jnp.dot(s_mat.astype(BF16), v, preferred_element_type=F32) + w_inter * qc
        n_rows = jnp.broadcast_to(n_old.astype(BF16), (SUBLANES, dh))
        qn = lax.dot_general(q, n_rows, (((1,), (1,)), ((), ())), preferred_element_type=F32)[:, 0:1]
        den = jnp.sum(s_mat, axis=1, keepdims=True) + w_inter * qn
        h_ref[:, cols] = num / jnp.maximum(jnp.abs(den), jnp.exp(-m_t))
        m_new = jnp.sum(jnp.where(last, m_t, 0.0), axis=0, keepdims=True)
        b_last = jnp.sum(jnp.where(last, b_col, 0.0), axis=0, keepdims=True)
        w_old = jnp.exp(b_last + m_prev - m_new)
        w_in = jnp.exp(b_last - b_col + ig_col - m_new)
        vw = (v.astype(F32) * w_in).astype(BF16)
        ct_s[j] = w_old * ct_old + jnp.dot(kt, vw, preferred_element_type=F32)
        n_s[j] = w_old * n_old + jnp.sum(w_in * k.astype(F32), axis=0, keepdims=True)
        m_s[j] = m_new

    if want_final:
        @pl.when(ci == pl.num_programs(3) - 1)
        def _():
            for j in range(hb):
                final_refs[0][j] = ct_s[j].T
                final_refs[1][j] = n_s[j]
                final_refs[2][j] = m_s[j]


def mlstm_cell(q, k, k_t, v, g, g_t, init, stream, heads, want_final):
    n, w = q.shape
    batch, seq = stream.batch, stream.seq
    dh = w // heads
    hb = ML_HEADS_PER_STEP
    t = min(ML_CHUNK, seq)
    nc = seq // t

    def blk(b, d, ci):
        return b * nc + ci + d * (nc - 1 - 2 * ci)

    tok = pl.BlockSpec((t, hb * dh), lambda b, h, d, ci: (blk(b, d, ci), h))
    st5 = lambda r, c: pl.BlockSpec((None, None, hb, r, c), lambda b, h, d, ci: (b, d, h, 0, 0))
    state_dims = ((dh, dh), (1, dh), (1, 1))
    out_shape = [jax.ShapeDtypeStruct((2, n, w), F32)]
    out_specs = [pl.BlockSpec((None, t, hb * dh), lambda b, h, d, ci: (d, blk(b, d, ci), h))]
    if want_final:
        out_shape += [jax.ShapeDtypeStruct((batch, 2, heads) + rc, F32) for rc in state_dims]
        out_specs += [st5(*rc) for rc in state_dims]
    res = pl.pallas_call(
        functools.partial(_mlstm_cell_kernel, n_heads=heads, has_init=init is not None, want_final=want_final),
        out_shape=tuple(out_shape),
        grid=(batch, heads // hb, 2, nc),
        in_specs=[tok, tok,
                  pl.BlockSpec((hb * dh, t), lambda b, h, d, ci: (h, blk(b, d, ci))),
                  tok,
                  pl.BlockSpec((t, g.shape[1]), lambda b, h, d, ci: (blk(b, d, ci), 0)),
                  pl.BlockSpec((g.shape[1], t), lambda b, h, d, ci: (0, blk(b, d, ci)))]
        + ([st5(*rc) for rc in state_dims] if init is not None else []),
        out_specs=tuple(out_specs),
        scratch_shapes=[pltpu.VMEM((hb,) + rc, F32) for rc in state_dims],
        compiler_params=_cparams("parallel", "parallel", "arbitrary", "arbitrary"),
        name="mlstm_cell",
    )(q, k, k_t, v, g, g_t, *(init or ()))
    return res[0], (tuple(res[1:]) if want_final else None)


def _ml_out_gate_epilogue(acc, rows, b_ref, h_ref, xc_ref, gn_ref, skip_ref):
    hc = jax.nn.sigmoid(acc + b_ref[...]) * (h_ref[0, rows, :] + h_ref[1, rows, :])
    hn = hc * lax.rsqrt(jnp.mean(hc * hc, axis=-1, keepdims=True) + NORM_EPS)
    return hn * gn_ref[...] + skip_ref[...] * xc_ref[rows, :]


def mlstm_mixer(h, x, mods, layer, stream, init, want_final, w_up, conv_w, conv_b, wq, wk, wk_t, wv, w_gate, b_gate,
                w_o, b_o, gn, skip, w_down):
    dh = wq.shape[1]
    xi = matmul(h, w_up, name="ml_up")
    xc, xc16 = ml_conv(xi, conv_w, conv_b, stream)
    q, k, k_t, v = ml_qkv(xc16, xi, wq, wk, wk_t, wv)
    w = xi.shape[1]
    g = ml_gates(q, k, v, w_gate, b_gate, w // dh)
    hdir, final = mlstm_cell(q, k, k_t, v, g, g.T, init, stream, w // dh, want_final)
    tm = min(MM_TM, h.shape[0])
    vec = lambda a: (a.reshape(1, w).astype(F32), (1, dh), lambda i, j: (0, j))
    y = matmul(h, w_o, tm=tm, tn=dh, out_dtype=BF16, name="ml_o", epilogue=_ml_out_gate_epilogue,
               extras=[vec(b_o), (hdir, (2, tm, dh), lambda i, j: (0, i, j)), (xc, (tm, dh), lambda i, j: (i, j)),
                       vec(gn), vec(skip)])
    return out_proj_residual(y, w_down, x, mods, layer, stream, name="ml_down"), final


def kernel(x_prompt, x_sample, state_s5, state_rglru, cache_dattn_k, cache_dattn_v, state_mlstm_C, state_mlstm_n, state_mlstm_m, c, c_ctx, ada_w, ada_b, norm_g, ffn1_w1, ffn1_w3, ffn1_w2, ffn2_w1, ffn2_w3, ffn2_w2, final_norm_g, s5_w_in, s5_a_re, s5_a_im, s5_log_dt, s5_b_re, s5_b_im, s5_c_re, s5_c_im, s5_d, s5_w_glu, s5_b_glu, s5_w_out, rg_w_in, rg_w_gate, rg_conv_w, rg_conv_b, rg_wa, rg_ba, rg_wx, rg_bx, rg_lam, rg_w_out, da_wq, da_wk, da_wv, da_wo, da_lam, da_subln_g, ml_w_up, ml_conv_w, ml_conv_b, ml_wq, ml_wk, ml_wv, ml_w_gate, ml_b_gate, ml_w_o, ml_b_o, ml_gn, ml_skip, ml_w_down):
    bc, lc, d = x_prompt.shape
    bl, ll, _ = x_sample.shape
    depth = ada_w.shape[0]
    cs = Stream(bc, lc, 0, False)
    ls = Stream(bl, ll, 1, True)
    ctx = x_prompt.reshape(cs.n, d)
    lat = x_sample.reshape(ls.n, d)
    c_all = jnp.concatenate([c_ctx[None, :], c, jnp.zeros((MOD_ROWS - 1 - bl, d), F32)], axis=0)
    mods = adaln(c_all, ada_w, ada_b)
    bf = lambda a: a.astype(BF16)
    ffn1 = (bf(ffn1_w1), bf(ffn1_w3), bf(ffn1_w2))
    ffn2 = (bf(ffn2_w1), bf(ffn2_w3), bf(ffn2_w2))

    new_s5 = new_rg = None
    for i in range(depth):
        ctx = ffn_half(ctx, norm_g[i, 0], mods, i, 0, *ffn1, cs)
        lat = ffn_half(lat, norm_g[i, 0], mods, i, 0, *ffn1, ls)
        hc = modnorm(ctx, norm_g[i, 1], mods, i, 3, cs)
        hl = modnorm(lat, norm_g[i, 1], mods, i, 3, ls)
        kind = i % 4
        if kind == 0:
            ops = _s5_slab_operators(s5_a_re, s5_a_im, s5_log_dt, s5_b_re, s5_b_im, s5_c_re, s5_c_im)
            p = s5_a_re.shape[2]
            slab_groups = LANES // s5_b_re.shape[3]
            args = (bf(s5_w_in), ops, s5_d, bf(s5_w_glu), s5_b_glu, bf(s5_w_out))
            zero_state = jnp.zeros((bc,) + state_s5.shape[1:], F32)
            ctx, fin = s5_slab_mixer(hc, ctx, mods, i, cs, s5_pack_state(zero_state, slab_groups), *args)
            lat, _ = s5_slab_mixer(hl, lat, mods, i, ls, s5_pack_state(state_s5, slab_groups), *args)
            new_s5 = s5_unpack_state(fin, slab_groups, p)
        elif kind == 1:
            w_gates = bf(jnp.concatenate([rg_wa, rg_wx], axis=-1))
            bias = jnp.stack([rg_ba, rg_bx], axis=1).astype(F32)
            args = (bf(rg_w_in), bf(rg_w_gate), rg_conv_w, rg_conv_b, w_gates, bias, rg_lam, bf(rg_w_out))
            ctx, new_rg = rg_mixer(hc, ctx, mods, i, cs, jnp.zeros((bc, 2, rg_lam.shape[1]), F32), *args)
            lat, _ = rg_mixer(hl, lat, mods, i, ls, state_rglru, *args)
        elif kind == 2:
            lam_init = 0.8 - 0.6 * math.exp(-0.3 * i)
            heads, dv = cache_dattn_v.shape[2], cache_dattn_v.shape[3]
            args = (bf(da_wq).T, bf(da_wk), bf(da_wv), bf(da_wv).T, bf(da_wo), da_lam, da_subln_g, lam_init)
            ctx, (k32, v32) = dattn_mixer(hc, ctx, mods, i, cs, None, *args)
            new_k = k32.reshape(bc, lc, heads, 2, dv // 2)
            new_v = v32.reshape(bc, lc, heads, dv)
            cache = (bf(cache_dattn_k).reshape(-1, heads * dv), bf(cache_dattn_v).transpose(0, 2, 3, 1))
            lat, _ = dattn_mixer(hl, lat, mods, i, ls, cache, *args)
        else:
            heads, dh = ml_wq.shape[0], ml_wq.shape[1]
            args = (bf(ml_w_up), ml_conv_w, ml_conv_b, bf(ml_wq).reshape(heads * dh, dh),
                    bf(ml_wk).reshape(heads * dh, dh), bf(ml_wk).transpose(0, 2, 1).reshape(heads * dh, dh),
                    bf(ml_wv).reshape(heads * dh, dh), bf(ml_w_gate), ml_b_gate,
                    bf(ml_w_o), ml_b_o, ml_gn, ml_skip, bf(ml_w_down))
            ctx, (cf, nf, mf) = mlstm_mixer(hc, ctx, mods, i, cs, None, True, *args)
            init = (state_mlstm_C.astype(F32), state_mlstm_n.astype(F32).reshape(bl, 2, heads, 1, dh),
                    state_mlstm_m.astype(F32).reshape(bl, 2, heads, 1, 1))
            lat, _ = mlstm_mixer(hl, lat, mods, i, ls, init, False, *args)
            new_c, new_n, new_m = cf, nf.reshape(bc, 2, heads, dh), mf.reshape(bc, 2, heads)
        ctx = ffn_half(ctx, norm_g[i, 2], mods, i, 6, *ffn2, cs)
        lat = ffn_half(lat, norm_g[i, 2], mods, i, 6, *ffn2, ls)
    y_prompt = rmsnorm(ctx, final_norm_g).reshape(bc, lc, d)
    y_sample = rmsnorm(lat, final_norm_g).reshape(bl, ll, d)
    return (y_prompt, y_sample, new_s5, new_rg, new_k, new_v, new_c, new_n, new_m)
```

```python
import functools
import math

import jax
import jax.numpy as jnp
from jax import lax
from jax.experimental import pallas as pl
from jax.experimental.pallas import tpu as pltpu

F32 = jnp.float32
BF16 = jnp.bfloat16

NORM_EPS = 1e-6
VMEM_LIMIT_BYTES = 56 * 1024 * 1024
MOD_ROWS = 8
ROW_TILE = 512
FFN_HIDDEN_TILE = 512
ADALN_TILE = 1024


def _cparams(*sem):
    return pltpu.CompilerParams(dimension_semantics=sem, vmem_limit_bytes=VMEM_LIMIT_BYTES)


def _bdot(a, b):
    return jnp.dot(a.astype(BF16), b.astype(BF16), preferred_element_type=F32)


def _mod_row(ref, row):
    return ref[pl.ds(row, 1), :]


class Stream:
    def __init__(self, batch, seq, row0, per_batch_mod):
        self.batch, self.seq, self.row0 = batch, seq, row0
        self.n = batch * seq
        self.rows_per_mod = seq if per_batch_mod else self.n

    def mod_row(self, first_token):
        return self.row0 + first_token // self.rows_per_mod


def _adaln_kernel(c_ref, w_ref, b_ref, o_ref):
    c = c_ref[...]
    s = c * jax.nn.sigmoid(c)
    o_ref[...] = _bdot(s, w_ref[...]) + b_ref[...]


def adaln(c_all, ada_w, ada_b, *, tn=ADALN_TILE):
    nl, d, m = ada_w.shape
    return pl.pallas_call(
        _adaln_kernel,
        out_shape=jax.ShapeDtypeStruct((nl, MOD_ROWS, m), F32),
        grid=(nl, m // tn),
        in_specs=[pl.BlockSpec((MOD_ROWS, d), lambda l, j: (0, 0)),
                  pl.BlockSpec((None, d, tn), lambda l, j: (l, 0, j)),
                  pl.BlockSpec((None, 1, tn), lambda l, j: (l, 0, j))],
        out_specs=pl.BlockSpec((None, MOD_ROWS, tn), lambda l, j: (l, 0, j)),
        compiler_params=_cparams("parallel", "parallel"),
        name="adaln",
    )(c_all, ada_w, ada_b.reshape(nl, 1, m))


def _mod_spec(d, layer, k):
    return pl.BlockSpec((None, MOD_ROWS, d), lambda *_: (layer, 0, k))


def _rms(x, g):
    return x * lax.rsqrt(jnp.mean(x * x, axis=-1, keepdims=True) + NORM_EPS) * g


def _modnorm_kernel(x_ref, g_ref, sh_ref, sc_ref, o_ref, *, stream, tm):
    row = stream.mod_row(pl.program_id(0) * tm)
    y = _rms(x_ref[...], g_ref[...])
    o_ref[...] = (y * (1.0 + _mod_row(sc_ref, row)) + _mod_row(sh_ref, row)).astype(o_ref.dtype)


def modnorm(x, g, mods, layer, k_shift, stream, *, tm=ROW_TILE):
    n, d = x.shape
    return pl.pallas_call(
        functools.partial(_modnorm_kernel, stream=stream, tm=tm),
        out_shape=jax.ShapeDtypeStruct((n, d), BF16),
        grid=(n // tm,),
        in_specs=[pl.BlockSpec((tm, d), lambda i: (i, 0)),
                  pl.BlockSpec((1, d), lambda i: (0, 0)),
                  _mod_spec(d, layer, k_shift),
                  _mod_spec(d, layer, k_shift + 1)],
        out_specs=pl.BlockSpec((tm, d), lambda i: (i, 0)),
        compiler_params=_cparams("parallel"),
        name="modnorm",
    )(x, g.reshape(1, d), mods, mods)


def _rmsnorm_kernel(x_ref, g_ref, o_ref):
    o_ref[...] = _rms(x_ref[...], g_ref[...])


def rmsnorm(x, g, *, tm=ROW_TILE):
    n, d = x.shape
    return pl.pallas_call(
        _rmsnorm_kernel,
        out_shape=jax.ShapeDtypeStruct((n, d), F32),
        grid=(n // tm,),
        in_specs=[pl.BlockSpec((tm, d), lambda i: (i, 0)), pl.BlockSpec((1, d), lambda i: (0, 0))],
        out_specs=pl.BlockSpec((tm, d), lambda i: (i, 0)),
        compiler_params=_cparams("parallel"),
        name="rmsnorm",
    )(x, g.reshape(1, d))


def _ffn_kernel(x_ref, g_ref, sh_ref, sc_ref, gt_ref, w1_ref, w3_ref, w2_ref, o_ref, h_ref, acc_ref,
                *, stream, tm):
    f = pl.program_id(1)
    row = stream.mod_row(pl.program_id(0) * tm)

    @pl.when(f == 0)
    def _():
        y = _rms(x_ref[...], g_ref[...])
        h_ref[...] = (y * (1.0 + _mod_row(sc_ref, row)) + _mod_row(sh_ref, row)).astype(BF16)
        acc_ref[...] = jnp.zeros_like(acc_ref)

    h = h_ref[...]
    a = jnp.dot(h, w1_ref[...], preferred_element_type=F32)
    b = jnp.dot(h, w3_ref[...], preferred_element_type=F32)
    u = (a * jax.nn.sigmoid(a)) * b
    acc_ref[...] += jnp.dot(u.astype(BF16), w2_ref[...], preferred_element_type=F32)

    @pl.when(f == pl.num_programs(1) - 1)
    def _():
        o_ref[...] = x_ref[...] + (0.5 * _mod_row(gt_ref, row)) * acc_ref[...]


def ffn_half(x, g, mods, layer, k_shift, w1, w3, w2, stream, *, tm=ROW_TILE, tf=FFN_HIDDEN_TILE):
    n, d = x.shape
    dff = w1.shape[2]
    return pl.pallas_call(
        functools.partial(_ffn_kernel, stream=stream, tm=tm),
        out_shape=jax.ShapeDtypeStruct((n, d), F32),
        grid=(n // tm, dff // tf),
        in_specs=[pl.BlockSpec((tm, d), lambda i, f: (i, 0)),
                  pl.BlockSpec((1, d), lambda i, f: (0, 0)),
                  _mod_spec(d, layer, k_shift),
                  _mod_spec(d, layer, k_shift + 1),
                  _mod_spec(d, layer, k_shift + 2),
                  pl.BlockSpec((None, d, tf), lambda i, f: (layer, 0, f)),
                  pl.BlockSpec((None, d, tf), lambda i, f: (layer, 0, f)),
                  pl.BlockSpec((None, tf, d), lambda i, f: (layer, f, 0))],
        out_specs=pl.BlockSpec((tm, d), lambda i, f: (i, 0)),
        scratch_shapes=[pltpu.VMEM((tm, d), BF16), pltpu.VMEM((tm, d), F32)],
        compiler_params=_cparams("parallel", "arbitrary"),
        name="ffn_half",
    )(x, g.reshape(1, d), mods, mods, mods, w1, w3, w2)


def _mm_kernel(x_ref, w_ref, *rest, epilogue, n_extra, slab_out):
    extra, o_refs = rest[:n_extra], rest[n_extra:]
    w = w_ref[...]
    tm = x_ref.shape[0]
    chunk = min(MM_ROW_CHUNK, tm)
    for r in range(tm // chunk):
        rows = slice(r * chunk, (r + 1) * chunk)
        acc = _bdot(x_ref[rows, :], w)
        outs = epilogue(acc, rows, *extra) if epilogue is not None else acc
        if not isinstance(outs, tuple):
            outs = (outs,)
        for o_ref, o in zip(o_refs, outs):
            if slab_out:
                o_ref[:, rows, :] = o.astype(o_ref.dtype)
            else:
                o_ref[rows, :] = o.astype(o_ref.dtype)


MM_TM = 1024
MM_TN = 1024
MM_K_WIDE = 2048
MM_ROW_CHUNK = 512
MM_TM_DIAG = 2048


def matmul(x, w, *, tm=MM_TM, tn=MM_TN, out_dtype=F32, epilogue=None, extras=(), slab_out=False, name="matmul"):
    n, kb = x.shape
    m = w.shape[1]
    tm = min(tm, n)
    dts = out_dtype if isinstance(out_dtype, tuple) else (out_dtype,)
    if slab_out:
        out_shape = tuple(jax.ShapeDtypeStruct((m // 128, n, 128), dt) for dt in dts)
        out_specs = tuple(pl.BlockSpec((tn // 128, tm, 128), lambda i, j: (j, i, 0)) for _ in dts)
    else:
        out_shape = tuple(jax.ShapeDtypeStruct((n, m), dt) for dt in dts)
        out_specs = tuple(pl.BlockSpec((tm, tn), lambda i, j: (i, j)) for _ in dts)
    res = pl.pallas_call(
        functools.partial(_mm_kernel, epilogue=epilogue, n_extra=len(extras), slab_out=slab_out),
        out_shape=out_shape,
        grid=(n // tm, m // tn),
        in_specs=[pl.BlockSpec((tm, kb), lambda i, j: (i, 0)), pl.BlockSpec((kb, tn), lambda i, j: (0, j))]
        + [pl.BlockSpec(bs, im) for _, bs, im in extras],
        out_specs=out_specs,
        compiler_params=_cparams("parallel", "parallel"),
        name=name,
    )(x, w, *[a for a, _, _ in extras])
    return res if isinstance(out_dtype, tuple) else res[0]


def _mm_t_kernel(x_ref, w_ref, *rest, epilogue, n_extra):
    extra, o_ref = rest[:n_extra], rest[n_extra]
    acc = lax.dot_general(w_ref[...], x_ref[...].astype(BF16), (((1,), (1,)), ((), ())), preferred_element_type=F32)
    o_ref[...] = (epilogue(acc, *extra) if epilogue is not None else acc).astype(o_ref.dtype)


def matmul_t(x, w_t, *, tm=MM_TM, tn=MM_TN, out_dtype=BF16, epilogue=None, extras=(), name="matmul_t"):
    n, kb = x.shape
    m = w_t.shape[0]
    tm = min(tm, n)
    return pl.pallas_call(
        functools.partial(_mm_t_kernel, epilogue=epilogue, n_extra=len(extras)),
        out_shape=jax.ShapeDtypeStruct((m, n), out_dtype),
        grid=(n // tm, m // tn),
        in_specs=[pl.BlockSpec((tm, kb), lambda i, j: (i, 0)), pl.BlockSpec((tn, kb), lambda i, j: (j, 0))]
        + [pl.BlockSpec(bs, im) for _, bs, im in extras],
        out_specs=pl.BlockSpec((tn, tm), lambda i, j: (j, i)),
        compiler_params=_cparams("parallel", "parallel"),
        name=name,
    )(x, w_t, *[a for a, _, _ in extras])


def _residual_epilogue(acc, rows, x_ref, gt_ref, *, stream, tm):
    row = stream.mod_row(pl.program_id(0) * tm)
    return x_ref[rows, :] + _mod_row(gt_ref, row) * acc


def out_proj_residual(z, w, x, mods, layer, stream, *, name="out_proj"):
    d = x.shape[1]
    tm = min(MM_TM, x.shape[0])
    tn = MM_TN if z.shape[1] <= MM_K_WIDE else MM_TN // 2
    return matmul(z, w, tm=tm, tn=tn, name=name,
                  epilogue=functools.partial(_residual_epilogue, stream=stream, tm=tm),
                  extras=[(x, (tm, tn), lambda i, j: (i, j)),
                          (mods, (None, MOD_ROWS, tn), lambda i, j: (layer, 0, 5 * (d // tn) + j))])


def _glu_epilogue(acc, rows, z_ref, b_ref):
    return z_ref[rows, :] * jax.nn.sigmoid(acc + b_ref[...])


SUBLANES = 8
LANES = 128
SCAN_BLOCK = 256
SEQ_BLOCK_ELEMS = 1 << 20


def _conv_rows(x_ref, t0, tb, seq, cw, cb):
    x = x_ref[pl.ds(t0, tb), :]
    prev = x_ref[pl.ds(pl.multiple_of(jnp.maximum(t0 - SUBLANES, 0), SUBLANES), SUBLANES), :]
    nxt = x_ref[pl.ds(pl.multiple_of(jnp.minimum(t0 + tb, seq - SUBLANES), SUBLANES), SUBLANES), :]
    has_prev = t0 > 0
    has_next = t0 + tb < seq
    p1 = jnp.where(has_prev, prev[SUBLANES - 1:SUBLANES, :], 0.0)
    n0 = jnp.where(has_next, nxt[0:1, :], 0.0)
    n1 = jnp.where(has_next, nxt[1:2, :], 0.0)
    rows = lax.broadcasted_iota(jnp.int32, x.shape, 0)
    xm1 = jnp.where(rows == 0, p1, pltpu.roll(x, 1, axis=0))
    xp1 = jnp.where(rows == tb - 1, n0, pltpu.roll(x, tb - 1, axis=0))
    xp2 = jnp.where(rows == tb - 1, n1, jnp.where(rows == tb - 2, n0, pltpu.roll(x, tb - 2, axis=0)))
    return cb + xm1 * cw[0:1, :] + x * cw[1:2, :] + xp1 * cw[2:3, :] + xp2 * cw[3:4, :]


def _group_scan(a, b, reverse):
    tb, tc = a.shape
    a = a.reshape(tb // SUBLANES, SUBLANES, tc)
    b = b.reshape(tb // SUBLANES, SUBLANES, tc)
    pos = lax.broadcasted_iota(jnp.int32, a.shape, 1)
    s = 1
    while s < SUBLANES:
        ok = pos < SUBLANES - s if reverse else pos >= s
        shift = SUBLANES - s if reverse else s
        a_s, b_s = pltpu.roll(a, shift, axis=1), pltpu.roll(b, shift, axis=1)
        b = b + jnp.where(ok, a, 0.0) * b_s
        a = jnp.where(ok, a * a_s, a)
        s *= 2
    return a.reshape(tb, tc), b.reshape(tb, tc)


def _softplus(x):
    return jnp.maximum(x, 0.0) + jnp.log1p(jnp.exp(-jnp.abs(x)))


SLAB_CHUNK = 8
S5_RECUR_ROWS = 512


def _s5_slab_operators(a_re, a_im, log_dt, b_re, b_im, c_re, c_im):
    hp = lax.Precision.HIGHEST
    t = SLAB_CHUNK
    a_re, a_im, b_re, b_im, c_re, c_im = (z.astype(F32) for z in (a_re, a_im, b_re, b_im, c_re, c_im))
    ng, p = a_re.shape[1], a_re.shape[2]
    c = b_re.shape[3]
    gs = LANES // c
    ns = ng // gs
    dt = jnp.exp(log_dt.astype(F32))[..., None]
    mag = jnp.exp(a_re * dt)
    lr = mag * jnp.cos(a_im * dt)
    li = mag * jnp.sin(a_im * dt)
    den = a_re * a_re + a_im * a_im
    cr = ((lr - 1.0) * a_re + li * a_im) / den
    ci = (li * a_re - (lr - 1.0) * a_im) / den
    bbr = cr[..., None] * b_re - ci[..., None] * b_im
    bbi = cr[..., None] * b_im + ci[..., None] * b_re

    def powers(k):
        k = k.astype(F32)[None, None, :, None]
        pm = jnp.exp(k * (a_re * dt)[:, :, None, :])
        return pm * jnp.cos(k * (a_im * dt)[:, :, None, :]), pm * jnp.sin(k * (a_im * dt)[:, :, None, :])

    pr, pi = powers(jnp.arange(t + 1))
    pbr = (pr[:, :, :, :, None] * bbr[:, :, None] - pi[:, :, :, :, None] * bbi[:, :, None])
    pbi = (pr[:, :, :, :, None] * bbi[:, :, None] + pi[:, :, :, :, None] * bbr[:, :, None])
    flat = lambda z: z.transpose(0, 1, 3, 2, 4).reshape(2, ng, p, (t + 1) * c)
    m = (jnp.einsum('dgop,dgpx->dgox', c_re, flat(pbr), precision=hp)
         - jnp.einsum('dgop,dgpx->dgox', c_im, flat(pbi), precision=hp)).reshape(2, ng, c, t + 1, c)

    def block_diag(small, row_group, col_group, col_inner):
        rows, q = small.shape[-2], small.shape[-1]
        cols = jnp.arange(gs * q)
        expand = (jnp.arange(q)[:, None] == col_inner(cols)[None, :]).astype(BF16)
        wide = jnp.matmul(small.astype(BF16), expand, preferred_element_type=BF16)
        keep = row_group(jnp.arange(rows))[:, None] == col_group(cols)[None, :]
        return jnp.where(keep, wide, jnp.zeros((), BF16))

    s_idx = jnp.arange(t)[:, None]
    t_idx = jnp.arange(t)[None, :]
    lag = jnp.stack([jnp.clip(t_idx - s_idx, 0, t), jnp.clip(s_idx - t_idx, 0, t)])
    live = jnp.stack([t_idx >= s_idx, s_idx >= t_idx]).astype(F32)
    msel = sum((m[d][:, :, lag[d], :] * live[d][None, None, :, :, None]).transpose(0, 2, 3, 4, 1) for d in range(2))
    msel = msel.reshape(ns, gs, t, t, c, c).transpose(0, 2, 1, 4, 3, 5)
    in_group = lambda r: (r % LANES) // c
    out_col = lambda col: (col // LANES) * c + col % c
    state_group = lambda r: (r % (gs * p)) // p
    state_col = lambda col: (col // (gs * p)) * p + col % p
    toep = block_diag(msel.reshape(ns, t * LANES, t * c), in_group, in_group, out_col)
    pow_s = jnp.stack([t - 1 - jnp.arange(t), jnp.arange(t)])
    pb = jnp.stack([pbr, pbi], axis=2)
    s_sel = jnp.stack([pb[d][:, :, pow_s[d]] for d in range(2)])
    s_sel = s_sel.reshape(2, ns, gs, 2, t, p, c).transpose(0, 1, 4, 2, 6, 3, 5)
    s_op = block_diag(s_sel.reshape(2, ns, t * LANES, 2 * p), in_group, state_group, state_col)
    pow_w = jnp.stack([jnp.arange(t) + 1, t - jnp.arange(t)])
    zr = jnp.stack([c_re[d][:, None] * pr[d][:, pow_w[d]][:, :, None, :]
                    - c_im[d][:, None] * pi[d][:, pow_w[d]][:, :, None, :] for d in range(2)])
    zi = jnp.stack([c_re[d][:, None] * pi[d][:, pow_w[d]][:, :, None, :]
                    + c_im[d][:, None] * pr[d][:, pow_w[d]][:, :, None, :] for d in range(2)])
    w_sel = jnp.stack([zr, -zi], axis=2)
    w_sel = w_sel.reshape(2, ns, gs, 2, t, c, p).transpose(0, 1, 3, 2, 6, 4, 5)
    w_op = block_diag(w_sel.reshape(2, ns, 2 * gs * p, t * c), state_group, in_group, out_col)
    qr, qi = powers(t * (jnp.arange(SUBLANES) + 1))
    lanes = lambda z: z.reshape(2, ns, gs, SUBLANES, p).transpose(0, 1, 3, 2, 4).reshape(2, ns, SUBLANES, gs * p)
    pow_a = jnp.concatenate([lanes(qr), lanes(qr)], axis=-1)
    pow_b = jnp.concatenate([-lanes(qi), lanes(qi)], axis=-1)
    return toep, s_op, w_op, pow_a, pow_b


def _s5_state_kernel(u_ref, s_ref, o_ref):
    o_ref[...] = jnp.dot(u_ref[...], s_ref[...], preferred_element_type=F32)


def _s5_recur_slab_kernel(st_ref, pa_ref, pb_ref, h0_ref, hin_ref, fin_ref, loc_ref, *, nj):
    rows, w = loc_ref.shape
    half = w // 2
    ng = nj // SUBLANES
    swap = lambda z: jnp.concatenate([z[..., half:], z[..., :half]], axis=-1)
    for d in range(2):
        reverse = d == 1
        pa = pa_ref[d]
        pb = pb_ref[d]
        z = st_ref[d].reshape(rows // SUBLANES, SUBLANES, w)
        pos = lax.broadcasted_iota(jnp.int32, z.shape, 1)
        s = 1
        while s < SUBLANES:
            ok = pos < SUBLANES - s if reverse else pos >= s
            zs = pltpu.roll(z, SUBLANES - s if reverse else s, axis=1)
            z = z + jnp.where(ok, zs * pa[s - 1:s, :] + swap(zs) * pb[s - 1:s, :], 0.0)
            s *= 2
        loc_ref[...] = z.reshape(rows, w)
        row = lax.broadcasted_iota(jnp.int32, (SUBLANES, w), 0)
        if reverse:
            ca = jnp.concatenate([pa[SUBLANES - 1 - r:SUBLANES - r, :] for r in range(SUBLANES)], axis=0)
            cb = jnp.concatenate([pb[SUBLANES - 1 - r:SUBLANES - r, :] for r in range(SUBLANES)], axis=0)
        else:
            ca, cb = pa, pb

        for b in range(rows // nj):
            def group(i, carry, d=d, b=b, reverse=reverse, ca=ca, cb=cb):
                c, c_sw = carry
                r0 = pl.multiple_of(b * nj + (ng - 1 - i if reverse else i) * SUBLANES, SUBLANES)
                h = loc_ref[pl.ds(r0, SUBLANES), :] + c * ca + c_sw * cb
                shifted = pltpu.roll(h, SUBLANES - 1 if reverse else 1, axis=0)
                hin_ref[d, pl.ds(r0, SUBLANES), :] = jnp.where(row == (SUBLANES - 1 if reverse else 0), c, shifted)
                last = h[0:1, :] if reverse else h[SUBLANES - 1:SUBLANES, :]
                return last, swap(last)

            h0 = h0_ref[d, b]
            fin_ref[d, b] = lax.fori_loop(0, ng, group, (h0, swap(h0)))[0]


def _s5_out_kernel(u_ref, toep_ref, hin_ref, w_ref, o_ref):
    acc = jnp.dot(u_ref[...], toep_ref[...], preferred_element_type=F32)
    for d in range(2):
        acc = acc + jnp.dot(hin_ref[d].astype(BF16), w_ref[d], preferred_element_type=F32)
    o_ref[...] = acc


def s5_slab_scan(u16, h0, ops, batch, seq):
    toep, s_op, w_op, pow_a, pow_b = ops
    ns, n, _ = u16.shape
    t = SLAB_CHUNK
    kc = t * LANES
    w = s_op.shape[-1]
    nj = seq // t
    rows = n // t
    tr = min(rows, MM_TM)
    uc = u16.reshape(ns, rows, kc)
    st = pl.pallas_call(
        _s5_state_kernel,
        out_shape=jax.ShapeDtypeStruct((2, ns, rows, w), F32),
        grid=(ns, 2, rows // tr),
        in_specs=[pl.BlockSpec((None, tr, kc), lambda a, d, r: (a, r, 0)),
                  pl.BlockSpec((None, None, kc, w), lambda a, d, r: (d, a, 0, 0))],
        out_specs=pl.BlockSpec((None, None, tr, w), lambda a, d, r: (d, a, r, 0)),
        compiler_params=_cparams("parallel", "parallel", "parallel"),
        name="s5_state",
    )(uc, s_op)
    bb = max(1, min(batch, S5_RECUR_ROWS // nj))
    hin, fin = pl.pallas_call(
        functools.partial(_s5_recur_slab_kernel, nj=nj),
        out_shape=(jax.ShapeDtypeStruct((2, ns, rows, w), F32), jax.ShapeDtypeStruct((2, ns, batch, 1, w), F32)),
        grid=(ns, batch // bb),
        in_specs=[pl.BlockSpec((2, None, bb * nj, w), lambda a, b: (0, a, b, 0)),
                  pl.BlockSpec((2, None, SUBLANES, w), lambda a, b: (0, a, 0, 0)),
                  pl.BlockSpec((2, None, SUBLANES, w), lambda a, b: (0, a, 0, 0)),
                  pl.BlockSpec((2, None, bb, 1, w), lambda a, b: (0, a, b, 0, 0))],
        out_specs=(pl.BlockSpec((2, None, bb * nj, w), lambda a, b: (0, a, b, 0)),
                   pl.BlockSpec((2, None, bb, 1, w), lambda a, b: (0, a, b, 0, 0))),
        scratch_shapes=[pltpu.VMEM((bb * nj, w), F32)],
        compiler_params=_cparams("parallel", "parallel"),
        name="s5_recur",
    )(st, pow_a, pow_b, h0)
    y = pl.pallas_call(
        _s5_out_kernel,
        out_shape=jax.ShapeDtypeStruct((ns, rows, kc), F32),
        grid=(ns, rows // tr),
        in_specs=[pl.BlockSpec((None, tr, kc), lambda a, r: (a, r, 0)),
                  pl.BlockSpec((None, kc, kc), lambda a, r: (a, 0, 0)),
                  pl.BlockSpec((2, None, tr, w), lambda a, r: (0, a, r, 0)),
                  pl.BlockSpec((2, None, w, kc), lambda a, r: (0, a, 0, 0))],
        out_specs=pl.BlockSpec((None, tr, kc), lambda a, r: (a, r, 0)),
        compiler_params=_cparams("parallel", "parallel"),
        name="s5_out_chunks",
    )(uc, toep, hin, w_op)
    return y.reshape(ns, n, LANES), fin


def _slab_epilogue(acc, rows):
    parts = jnp.stack([acc[:, s * LANES:(s + 1) * LANES] for s in range(acc.shape[1] // LANES)])
    return parts, parts


def _s5_post_slab_kernel(u_ref, y_ref, d_ref, o_ref):
    for s in range(u_ref.shape[0]):
        cols = slice(s * LANES, (s + 1) * LANES)
        o_ref[:, cols] = jax.nn.gelu(d_ref[:, cols] * u_ref[s] + y_ref[s])


def s5_post_slab(u, y, d_skip, *, tm=ROW_TILE):
    ns, n, _ = u.shape
    w = ns * LANES
    slab = pl.BlockSpec((ns, tm, LANES), lambda i: (0, i, 0))
    return pl.pallas_call(
        _s5_post_slab_kernel,
        out_shape=jax.ShapeDtypeStruct((n, w), F32),
        grid=(n // tm,),
        in_specs=[slab, slab, pl.BlockSpec((1, w), lambda i: (0, 0))],
        out_specs=pl.BlockSpec((tm, w), lambda i: (i, 0)),
        compiler_params=_cparams("parallel"),
        name="s5_post",
    )(u, y, d_skip.reshape(1, w).astype(F32))


def s5_slab_mixer(h, x, mods, layer, stream, state, w_in, ops, d_skip, w_glu, b_glu, w_out):
    u, u16 = matmul(h, w_in, out_dtype=(F32, BF16), epilogue=_slab_epilogue, slab_out=True, name="s5_in")
    y, fin = s5_slab_scan(u16, state, ops, stream.batch, stream.seq)
    z = s5_post_slab(u, y, d_skip)
    w = z.shape[1]
    zz = matmul(z, w_glu, out_dtype=BF16, name="s5_glu", epilogue=_glu_epilogue,
                extras=[(z, (min(MM_TM, z.shape[0]), MM_TN), lambda i, j: (i, j)),
                        (b_glu.reshape(1, w).astype(F32), (1, MM_TN), lambda i, j: (0, j))])
    return out_proj_residual(zz, w_out, x, mods, layer, stream, name="s5_out"), fin


def s5_pack_state(state, slab_groups):
    b, _, _, g, p = state.shape
    s = state.astype(F32).reshape(b, 2, 2, g // slab_groups, slab_groups * p)
    return s.transpose(1, 3, 0, 2, 4).reshape(2, g // slab_groups, b, 1, 2 * slab_groups * p)


def s5_unpack_state(fin, slab_groups, p):
    _, ns, b, _, w = fin.shape
    s = fin.reshape(2, ns, b, 2, slab_groups * p).transpose(2, 0, 3, 1, 4)
    return s.reshape(b, 2, 2, ns * slab_groups, p)


RG_C = 8.0


def _rg_core_kernel(x_ref, gate_ref, cw_ref, cb_ref, w_ref, bias_ref, lam_ref, h0_ref, o_ref, fin_ref,
                    y_ref, xc_ref, a_ref, b_ref, *, seq):
    tb = a_ref.shape[0]
    tc = a_ref.shape[1]
    n_blocks = seq // tb
    n_groups = tb // SUBLANES
    cw = cw_ref[...]
    cb = cb_ref[...]
    for d in range(2):
        reverse = d == 1
        sp = _softplus(-lam_ref[d:d + 1, :])
        bias_a = bias_ref[d, 0:1, :]
        bias_x = bias_ref[d, 1:2, :]

        def block(k, carry, d=d, reverse=reverse, sp=sp, bias_a=bias_a, bias_x=bias_x):
            t0 = pl.multiple_of((n_blocks - 1 - k if reverse else k) * tb, tb)
            if reverse:
                xc = xc_ref[pl.ds(t0, tb), :]
            else:
                xc = _conv_rows(x_ref, t0, tb, seq, cw, cb)
                xc_ref[pl.ds(t0, tb), :] = xc
            pre = [_bdot(xc[:, n * LANES:(n + 1) * LANES], w_ref[d, n]) for n in range(tc // LANES)]
            r = jax.nn.sigmoid(jnp.concatenate([p[:, :LANES] for p in pre], axis=1) + bias_a)
            ig = jax.nn.sigmoid(jnp.concatenate([p[:, LANES:] for p in pre], axis=1) + bias_x)
            log_a = -RG_C * r * sp
            a = jnp.exp(log_a)
            bv = jnp.sqrt(-jnp.tanh(log_a) * (a * a + 1.0)) * (ig * xc)
            a_ref[...], b_ref[...] = _group_scan(a, bv, reverse)

            def group(i, c):
                r0 = pl.multiple_of((n_groups - 1 - i if reverse else i) * SUBLANES, SUBLANES)
                h = b_ref[pl.ds(r0, SUBLANES), :] + a_ref[pl.ds(r0, SUBLANES), :] * c
                rows = pl.ds(pl.multiple_of(t0 + r0, SUBLANES), SUBLANES)
                if reverse:
                    y_ref[rows, :] += h
                    return h[0:1, :]
                y_ref[rows, :] = h
                return h[SUBLANES - 1:SUBLANES, :]

            carry = lax.fori_loop(0, n_groups, group, carry)
            if reverse:
                o_ref[pl.ds(t0, tb), :] = (y_ref[pl.ds(t0, tb), :] * gate_ref[pl.ds(t0, tb), :]).astype(o_ref.dtype)
            return carry

        fin_ref[d:d + 1, :] = lax.fori_loop(0, n_blocks, block, h0_ref[d:d + 1, :])


def rg_core(xin, gate, conv_w, conv_b, w_gates, bias, lam, h0, stream, *, tc=2 * LANES):
    n, c = xin.shape
    seq, batch = stream.seq, stream.batch
    tb = min(SCAN_BLOCK, seq)
    nb = tc // LANES
    return pl.pallas_call(
        functools.partial(_rg_core_kernel, seq=seq),
        out_shape=(jax.ShapeDtypeStruct((n, c), BF16), jax.ShapeDtypeStruct((batch, 2, c), F32)),
        grid=(batch, c // tc),
        in_specs=[pl.BlockSpec((seq, tc), lambda b, j: (b, j)),
                  pl.BlockSpec((seq, tc), lambda b, j: (b, j)),
                  pl.BlockSpec((4, tc), lambda b, j: (0, j)),
                  pl.BlockSpec((1, tc), lambda b, j: (0, j)),
                  pl.BlockSpec((2, nb, LANES, 2 * LANES), lambda b, j: (0, j, 0, 0)),
                  pl.BlockSpec((2, 2, tc), lambda b, j: (0, 0, j)),
                  pl.BlockSpec((2, tc), lambda b, j: (0, j)),
                  pl.BlockSpec((None, 2, tc), lambda b, j: (b, 0, j))],
        out_specs=(pl.BlockSpec((seq, tc), lambda b, j: (b, j)),
                   pl.BlockSpec((None, 2, tc), lambda b, j: (b, 0, j))),
        scratch_shapes=[pltpu.VMEM((seq, tc), F32), pltpu.VMEM((seq, tc), F32),
                        pltpu.VMEM((tb, tc), F32), pltpu.VMEM((tb, tc), F32)],
        compiler_params=_cparams("parallel", "parallel"),
        name="rg_core",
    )(xin, gate, conv_w, conv_b.reshape(1, c), w_gates, bias, lam, h0)


def _gelu_epilogue(acc, rows):
    return jax.nn.gelu(acc)


def rg_mixer(h, x, mods, layer, stream, state, w_in, w_gate, conv_w, conv_b, w_gates, bias, lam, w_out):
    xin = matmul(h, w_in, name="rg_in")
    gate = matmul(h, w_gate, epilogue=_gelu_epilogue, name="rg_gate")
    yg, fin = rg_core(xin, gate, conv_w, conv_b, w_gates, bias, lam, state, stream)
    return out_proj_residual(yg, w_out, x, mods, layer, stream, name="rg_out"), fin


GRID_W = 64
ROPE_BASE = 10000.0
ATTN_TQ = 512
ATTN_TK = 512


def rope_tables(seq, dk):
    half = dk // 2
    quarter = half // 2
    pos = jnp.arange(seq)
    inv = ROPE_BASE ** (-jnp.arange(0, half, 2, dtype=F32) / half)
    ang_r = (pos // GRID_W).astype(F32)[:, None] * inv
    ang_c = (pos % GRID_W).astype(F32)[:, None] * inv
    cos = jnp.concatenate([jnp.cos(ang_r)] * 2 + [jnp.cos(ang_c)] * 2, axis=1)
    sin = jnp.concatenate([-jnp.sin(ang_r), jnp.sin(ang_r), -jnp.sin(ang_c), jnp.sin(ang_c)], axis=1)
    assert cos.shape == (seq, 4 * quarter)
    return cos, sin


def _rope_epilogue(acc, rows, cos_ref, sin_ref):
    cos = cos_ref[rows, :]
    sin = sin_ref[rows, :]
    dk = cos.shape[1]
    lane = lax.broadcasted_iota(jnp.int32, cos.shape, 1)
    first = (lane % (dk // 2)) < dk // 4
    outs = []
    for g in range(acc.shape[1] // dk):
        x = acc[:, g * dk:(g + 1) * dk]
        rot = jnp.where(first, pltpu.roll(x, dk - dk // 4, axis=1), pltpu.roll(x, dk // 4, axis=1))
        outs.append(x * cos + rot * sin)
    return jnp.concatenate(outs, axis=1)


def _attn_t_kernel(*refs, n_tiles, has_cache, lam_init, scale):
    if has_cache:
        qt_ref, k_ref, vt_ref, ck_ref, cvt_ref, lam_ref, g_ref, o_ref, s_ref, sc_ref = refs
    else:
        qt_ref, k_ref, vt_ref, lam_ref, g_ref, o_ref, s_ref = refs
    tq = qt_ref.shape[1]
    dk = qt_ref.shape[0] // 2
    dv = vt_ref.shape[0]
    tk = s_ref.shape[2]
    maps = range(2)
    qt = [qt_ref[c * dk:(c + 1) * dk, :] for c in maps]

    def scores(kt):
        return [jnp.dot(kt[:, c * dk:(c + 1) * dk], qt[c], preferred_element_type=F32) for c in maps]

    m = [jnp.full((1, tq), -jnp.inf, F32) for _ in maps]
    if has_cache:
        s = scores(ck_ref[...])
        for c in maps:
            sc_ref[c] = s[c]
            m[c] = jnp.maximum(m[c], jnp.max(s[c], axis=0, keepdims=True))

    for i in range(n_tiles):
        s = scores(k_ref[i * tk:(i + 1) * tk, :])
        for c in maps:
            s_ref[c, i] = s[c]
            m[c] = jnp.maximum(m[c], jnp.max(s[c], axis=0, keepdims=True))

    c2 = scale * math.log2(math.e)

    def accumulate(s, vt, st):
        new = []
        for c in maps:
            l, acc = st[c]
            p = jnp.exp2((s[c] - m[c]) * c2)
            new.append((l + jnp.sum(p, axis=0, keepdims=True),
                        acc + jnp.dot(vt, p.astype(BF16), preferred_element_type=F32)))
        return tuple(new)

    st = tuple((jnp.zeros((1, tq), F32), jnp.zeros((dv, tq), F32)) for _ in maps)
    if has_cache:
        st = accumulate([sc_ref[c] for c in maps], cvt_ref[...], st)

    for i in range(n_tiles):
        st = accumulate([s_ref[c, i] for c in maps], vt_ref[:, i * tk:(i + 1) * tk], st)
    lp = lam_ref[...]
    lam = (jnp.exp(jnp.sum(lp[0:1, :] * lp[1:2, :], axis=-1, keepdims=True))
           - jnp.exp(jnp.sum(lp[2:3, :] * lp[3:4, :], axis=-1, keepdims=True)) + lam_init)
    diff = st[0][1] / st[0][0] - lam * (st[1][1] / st[1][0])
    diff = diff * lax.rsqrt(jnp.mean(diff * diff, axis=0, keepdims=True) + NORM_EPS)
    o_ref[...] = (diff.T * g_ref[...] * (1.0 - lam_init)).astype(o_ref.dtype)


def diff_attention_t(q_t, k, v_t, *rest, stream, lam_init):
    has_cache = len(rest) == 4
    da_lam, subln_g = rest[-2], rest[-1]
    n = k.shape[0]
    dv = subln_g.shape[0]
    heads = v_t.shape[0] // dv
    batch, seq = stream.batch, stream.seq
    tq = min(ATTN_TQ, seq)
    tk = min(ATTN_TK, seq)
    n_tiles = seq // tk
    nq = seq // tq
    in_specs = [pl.BlockSpec((dv, tq), lambda b, h, i: (h, b * nq + i)),
                pl.BlockSpec((seq, dv), lambda b, h, i: (b, h)),
                pl.BlockSpec((dv, seq), lambda b, h, i: (h, b))]
    args = [q_t, k, v_t]
    scratch = [pltpu.VMEM((2, n_tiles, tk, tq), F32)]
    if has_cache:
        ck, cv_t = rest[0], rest[1]
        past = cv_t.shape[3]
        in_specs += [pl.BlockSpec((past, dv), lambda b, h, i: (b, h)),
                     pl.BlockSpec((None, None, dv, past), lambda b, h, i: (b, h, 0, 0))]
        args += [ck, cv_t]
        scratch += [pltpu.VMEM((2, past, tq), F32)]
    in_specs += [pl.BlockSpec(da_lam.shape, lambda b, h, i: (0, 0)), pl.BlockSpec((1, dv), lambda b, h, i: (0, 0))]
    args += [da_lam.astype(F32), subln_g.reshape(1, dv).astype(F32)]
    return pl.pallas_call(
        functools.partial(_attn_t_kernel, n_tiles=n_tiles, has_cache=has_cache, lam_init=lam_init,
                          scale=1.0 / math.sqrt(dv // 2)),
        out_shape=jax.ShapeDtypeStruct((n, heads * dv), BF16),
        grid=(batch, heads, nq),
        in_specs=in_specs,
        out_specs=pl.BlockSpec((tq, dv), lambda b, h, i: (b * nq + i, h)),
        scratch_shapes=scratch,
        compiler_params=_cparams("parallel", "parallel", "parallel"),
        name="diff_attention",
    )(*args)


def _rope_t_epilogue(acc, cos_ref, sin_ref):
    cos = cos_ref[...]
    sin = sin_ref[...]
    dk = cos.shape[0]
    row = lax.broadcasted_iota(jnp.int32, cos.shape, 0)
    first = (row % (dk // 2)) < dk // 4
    outs = []
    for g in range(acc.shape[0] // dk):
        x = acc[g * dk:(g + 1) * dk, :]
        rot = jnp.where(first, pltpu.roll(x, dk - dk // 4, axis=0), pltpu.roll(x, dk // 4, axis=0))
        outs.append(x * cos + rot * sin)
    return jnp.concatenate(outs, axis=0)


def _dup_epilogue(acc, rows):
    return acc, acc


def dattn_mixer(h, x, mods, layer, stream, cache, wq_t, wk, wv, wv_t, wo, da_lam, subln_g, lam_init):
    dk = subln_g.shape[0] // 2
    v_t = matmul_t(h, wv_t, name="da_v_t")
    if cache is None:
        q_t = matmul_t(h, wq_t, name="da_q_t")
        k32, k = matmul(h, wk, out_dtype=(F32, BF16), epilogue=_dup_epilogue, name="da_k")
        v32 = matmul(h, wv, name="da_v")
        o = diff_attention_t(q_t, k, v_t, da_lam, subln_g, stream=stream, lam_init=lam_init)
        new = (k32, v32)
    else:
        cos, sin = rope_tables(stream.seq, dk)
        tm = min(MM_TM, stream.seq)
        per = stream.seq // tm
        rope = [(cos, (tm, dk), lambda i, j: (i % per, 0)), (sin, (tm, dk), lambda i, j: (i % per, 0))]
        rope_t = [(cos.T, (dk, tm), lambda i, j: (0, i % per)), (sin.T, (dk, tm), lambda i, j: (0, i % per))]
        q_t = matmul_t(h, wq_t, tm=tm, epilogue=_rope_t_epilogue, extras=rope_t, name="da_q_t_rope")
        k = matmul(h, wk, tm=tm, out_dtype=BF16, epilogue=_rope_epilogue, extras=rope, name="da_k_rope")
        o = diff_attention_t(q_t, k, v_t, cache[0], cache[1], da_lam, subln_g, stream=stream, lam_init=lam_init)
        new = None
    return out_proj_residual(o, wo, x, mods, layer, stream, name="da_out"), new


ML_CHUNK = 256


def _ml_conv_kernel(x_ref, cw_ref, cb_ref, o32_ref, o16_ref, *, seq):
    tb = min(SCAN_BLOCK, seq)
    cw = cw_ref[...]
    cb = cb_ref[...]

    def body(i, carry):
        t0 = pl.multiple_of(i * tb, tb)
        y = _conv_rows(x_ref, t0, tb, seq, cw, cb)
        y = y * jax.nn.sigmoid(y)
        o32_ref[pl.ds(t0, tb), :] = y
        o16_ref[pl.ds(t0, tb), :] = y.astype(BF16)
        return carry

    lax.fori_loop(0, seq // tb, body, 0)


def ml_conv(xi, conv_w, conv_b, stream):
    n, w = xi.shape
    seq = stream.seq
    tc = min(w, MM_TM, SEQ_BLOCK_ELEMS // seq)
    return pl.pallas_call(
        functools.partial(_ml_conv_kernel, seq=seq),
        out_shape=(jax.ShapeDtypeStruct((n, w), F32), jax.ShapeDtypeStruct((n, w), BF16)),
        grid=(stream.batch, w // tc),
        in_specs=[pl.BlockSpec((seq, tc), lambda b, j: (b, j)),
                  pl.BlockSpec((conv_w.shape[0], tc), lambda b, j: (0, j)),
                  pl.BlockSpec((1, tc), lambda b, j: (0, j))],
        out_specs=(pl.BlockSpec((seq, tc), lambda b, j: (b, j)), pl.BlockSpec((seq, tc), lambda b, j: (b, j))),
        compiler_params=_cparams("parallel", "parallel"),
        name="ml_conv",
    )(xi, conv_w.astype(F32), conv_b.reshape(1, w).astype(F32))


def _ml_qkv_kernel(xc_ref, xi_ref, wq_ref, wk_ref, wkt_ref, wv_ref, wg_ref, bg_ref,
                   q_ref, k_ref, kt_ref, v_ref, g_ref, *, scale, n_heads):
    hd = pl.program_id(1)
    xc = xc_ref[...]
    q = jnp.dot(xc, wq_ref[...], preferred_element_type=F32).astype(BF16)
    k = (jnp.dot(xc, wk_ref[...], preferred_element_type=F32) * scale).astype(BF16)
    kt = lax.dot_general(wkt_ref[...], xc, (((1,), (1,)), ((), ())), preferred_element_type=F32)
    v = _bdot(xi_ref[...], wv_ref[...]).astype(BF16)
    q_ref[...] = q
    k_ref[...] = k
    kt_ref[...] = (kt * scale).astype(BF16)
    v_ref[...] = v
    part = (jnp.dot(q, wg_ref[0], preferred_element_type=F32) + jnp.dot(k, wg_ref[1], preferred_element_type=F32)
            + jnp.dot(v, wg_ref[2], preferred_element_type=F32))

    @pl.when(hd == 0)
    def _():
        g_ref[...] = part + bg_ref[...]

    @pl.when(hd > 0)
    def _():
        g_ref[...] += part

    @pl.when(hd == n_heads - 1)
    def _():
        g = g_ref[...]
        col = lax.broadcasted_iota(jnp.int32, g.shape, 1)
        is_forget = (col // n_heads) % 2 == 1
        g_ref[...] = jnp.where(is_forget, -_softplus(-g), g)


def ml_qkv(xc16, xi, wq, wk, wk_t, wv, w_gate, b_gate):
    n, w = xc16.shape
    dh = wq.shape[1]
    ng = w_gate.shape[2]
    tm = min(MM_TM_DIAG, n)
    tok = pl.BlockSpec((tm, dh), lambda i, h: (i, h))
    wsp = pl.BlockSpec((dh, dh), lambda i, h: (h, 0))
    tokens = jax.ShapeDtypeStruct((n, w), BF16)
    return pl.pallas_call(
        functools.partial(_ml_qkv_kernel, scale=1.0 / math.sqrt(dh), n_heads=w // dh),
        out_shape=(tokens, tokens, jax.ShapeDtypeStruct((w, n), BF16), tokens, jax.ShapeDtypeStruct((n, ng), F32)),
        grid=(n // tm, w // dh),
        in_specs=[tok, tok, wsp, wsp, wsp, wsp,
                  pl.BlockSpec((3, dh, ng), lambda i, h: (0, h, 0)), pl.BlockSpec((1, ng), lambda i, h: (0, 0))],
        out_specs=(tok, tok, pl.BlockSpec((dh, tm), lambda i, h: (h, i)), tok,
                   pl.BlockSpec((tm, ng), lambda i, h: (i, 0))),
        compiler_params=_cparams("parallel", "arbitrary"),
        name="ml_qkv",
    )(xc16, xi, wq, wk, wk_t, wv, w_gate, b_gate.reshape(1, ng).astype(F32))


ML_HEADS_PER_STEP = 2


def _mlstm_cell_kernel(*refs, n_heads, has_init, want_final):
    q_ref, k_ref, kt_ref, v_ref, g_ref, gt_ref = refs[:6]
    init_refs = refs[6:9] if has_init else None
    h_ref = refs[9 if has_init else 6]
    ct_s, n_s, m_s = refs[-3:]
    final_refs = refs[-6:-3] if want_final else None
    hb = ct_s.shape[0]
    dh = ct_s.shape[1]
    h0 = pl.program_id(1) * hb
    d = pl.program_id(2)
    ci = pl.program_id(3)
    reverse = d == 1
    t = q_ref.shape[0]

    @pl.when(ci == 0)
    def _():
        for j in range(hb):
            if has_init:
                ct_s[j] = init_refs[0][j].T
                n_s[j] = init_refs[1][j]
                m_s[j] = init_refs[2][j]
            else:
                ct_s[j] = jnp.zeros((dh, dh), F32)
                n_s[j] = jnp.zeros((1, dh), F32)
                m_s[j] = jnp.zeros((1, 1), F32)

    g = g_ref[...]
    gt = gt_ref[...]
    lane = lax.broadcasted_iota(jnp.int32, g.shape, 1)
    sub = lax.broadcasted_iota(jnp.int32, gt.shape, 0)
    r_idx = lax.broadcasted_iota(jnp.int32, (t, t), 0)
    s_idx = lax.broadcasted_iota(jnp.int32, (t, t), 1)
    ahead = (s_idx - r_idx) * jnp.where(reverse, -1, 1)
    mask = ahead <= 0
    mask_t = ahead >= 0
    row = lax.broadcasted_iota(jnp.int32, (t, 1), 0)
    last = row == jnp.where(reverse, 0, t - 1)

    for j in range(hb):
        cols = slice(j * dh, (j + 1) * dh)
        q = q_ref[:, cols]
        k = k_ref[:, cols]
        v = v_ref[:, cols]
        kt = kt_ref[cols, :]
        col_i = d * (2 * n_heads) + h0 + j
        col_f = col_i + n_heads
        ig_col = jnp.sum(jnp.where(lane == col_i, g, 0.0), axis=1, keepdims=True)
        lf_col = jnp.sum(jnp.where(lane == col_f, g, 0.0), axis=1, keepdims=True)
        ig_row = jnp.sum(jnp.where(sub == col_i, gt, 0.0), axis=0, keepdims=True)
        lf_row = jnp.sum(jnp.where(sub == col_f, gt, 0.0), axis=0, keepdims=True)
        b_col = jnp.sum(jnp.where(mask, lf_row, 0.0), axis=1, keepdims=True)
        b_row = jnp.sum(jnp.where(mask_t, lf_col, 0.0), axis=0, keepdims=True)
        m_prev = m_s[j]
        dlog = jnp.where(mask, b_col - b_row + ig_row, -jnp.inf)
        inter = b_col + m_prev
        m_t = jnp.maximum(inter, jnp.max(dlog, axis=1, keepdims=True))
        w_intra = jnp.exp(dlog - m_t)
        w_inter = jnp.exp(inter - m_t)
        s_mat = lax.dot_general(q, k, (((1,), (1,)), ((), ())), preferred_element_type=F32) * w_intra
        ct_old = ct_s[j]
        n_old = n_s[j]
        qc = jnp.dot(q, ct_old.astype(BF16), preferred_element_type=F32)
        num = jnp.dot(s_mat.astype(BF16), v, preferred_element_type=F32) + w_inter * qc
        n_rows = jnp.broadcast_to(n_old.astype(BF16), (SUBLANES, dh))
        qn = lax.dot_general(q, n_rows, (((1,), (1,)), ((), ())), preferred_element_type=F32)[:, 0:1]
        den = jnp.sum(s_mat, axis=1, keepdims=True) + w_inter * qn
        h_ref[:, cols] = num / jnp.maximum(jnp.abs(den), jnp.exp(-m_t))
        m_new = jnp.sum(jnp.where(last, m_t, 0.0), axis=0, keepdims=True)
        b_last = jnp.sum(jnp.where(last, b_col, 0.0), axis=0, keepdims=True)
        w_old = jnp.exp(b_last + m_prev - m_new)
        w_in = jnp.exp(b_last - b_col + ig_col - m_new)
        vw = (v.astype(F32) * w_in).astype(BF16)
        ct_s[j] = w_old * ct_old + jnp.dot(kt, vw, preferred_element_type=F32)
        n_s[j] = w_old * n_old + jnp.sum(w_in * k.astype(F32), axis=0, keepdims=True)
        m_s[j] = m_new

    if want_final:
        @pl.when(ci == pl.num_programs(3) - 1)
        def _():
            for j in range(hb):
                final_refs[0][j] = ct_s[j].T
                final_refs[1][j] = n_s[j]
                final_refs[2][j] = m_s[j]


def mlstm_cell(q, k, k_t, v, g, g_t, init, stream, heads, want_final):
    n, w = q.shape
    batch, seq = stream.batch, stream.seq
    dh = w // heads
    hb = ML_HEADS_PER_STEP
    t = min(ML_CHUNK, seq)
    nc = seq // t

    def blk(b, d, ci):
        return b * nc + ci + d * (nc - 1 - 2 * ci)

    tok = pl.BlockSpec((t, hb * dh), lambda b, h, d, ci: (blk(b, d, ci), h))
    st5 = lambda r, c: pl.BlockSpec((None, None, hb, r, c), lambda b, h, d, ci: (b, d, h, 0, 0))
    state_dims = ((dh, dh), (1, dh), (1, 1))
    out_shape = [jax.ShapeDtypeStruct((2, n, w), F32)]
    out_specs = [pl.BlockSpec((None, t, hb * dh), lambda b, h, d, ci: (d, blk(b, d, ci), h))]
    if want_final:
        out_shape += [jax.ShapeDtypeStruct((batch, 2, heads) + rc, F32) for rc in state_dims]
        out_specs += [st5(*rc) for rc in state_dims]
    res = pl.pallas_call(
        functools.partial(_mlstm_cell_kernel, n_heads=heads, has_init=init is not None, want_final=want_final),
        out_shape=tuple(out_shape),
        grid=(batch, heads // hb, 2, nc),
        in_specs=[tok, tok,
                  pl.BlockSpec((hb * dh, t), lambda b, h, d, ci: (h, blk(b, d, ci))),
                  tok,
                  pl.BlockSpec((t, g.shape[1]), lambda b, h, d, ci: (blk(b, d, ci), 0)),
                  pl.BlockSpec((g.shape[1], t), lambda b, h, d, ci: (0, blk(b, d, ci)))]
        + ([st5(*rc) for rc in state_dims] if init is not None else []),
        out_specs=tuple(out_specs),
        scratch_shapes=[pltpu.VMEM((hb,) + rc, F32) for rc in state_dims],
        compiler_params=_cparams("parallel", "parallel", "arbitrary", "arbitrary"),
        name="mlstm_cell",
    )(q, k, k_t, v, g, g_t, *(init or ()))
    return res[0], (tuple(res[1:]) if want_final else None)


def _ml_out_gate_epilogue(acc, rows, b_ref, h_ref, xc_ref, gn_ref, skip_ref):
    hc = jax.nn.sigmoid(acc + b_ref[...]) * (h_ref[0, rows, :] + h_ref[1, rows, :])
    hn = hc * lax.rsqrt(jnp.mean(hc * hc, axis=-1, keepdims=True) + NORM_EPS)
    return hn * gn_ref[...] + skip_ref[...] * xc_ref[rows, :]


def mlstm_mixer(h, x, mods, layer, stream, init, want_final, w_up, conv_w, conv_b, wq, wk, wk_t, wv, w_gate, b_gate,
                w_o, b_o, gn, skip, w_down):
    dh = wq.shape[1]
    xi = matmul(h, w_up, name="ml_up")
    xc, xc16 = ml_conv(xi, conv_w, conv_b, stream)
    q, k, k_t, v, g = ml_qkv(xc16, xi, wq, wk, wk_t, wv, w_gate, b_gate)
    w = xi.shape[1]
    hdir, final = mlstm_cell(q, k, k_t, v, g, g.T, init, stream, w // dh, want_final)
    tm = min(MM_TM, h.shape[0])
    vec = lambda a: (a.reshape(1, w).astype(F32), (1, dh), lambda i, j: (0, j))
    y = matmul(h, w_o, tm=tm, tn=dh, out_dtype=BF16, name="ml_o", epilogue=_ml_out_gate_epilogue,
               extras=[vec(b_o), (hdir, (2, tm, dh), lambda i, j: (0, i, j)), (xc, (tm, dh), lambda i, j: (i, j)),
                       vec(gn), vec(skip)])
    return out_proj_residual(y, w_down, x, mods, layer, stream, name="ml_down"), final


def kernel(x_prompt, x_sample, state_s5, state_rglru, cache_dattn_k, cache_dattn_v, state_mlstm_C, state_mlstm_n, state_mlstm_m, c, c_ctx, ada_w, ada_b, norm_g, ffn1_w1, ffn1_w3, ffn1_w2, ffn2_w1, ffn2_w3, ffn2_w2, final_norm_g, s5_w_in, s5_a_re, s5_a_im, s5_log_dt, s5_b_re, s5_b_im, s5_c_re, s5_c_im, s5_d, s5_w_glu, s5_b_glu, s5_w_out, rg_w_in, rg_w_gate, rg_conv_w, rg_conv_b, rg_wa, rg_ba, rg_wx, rg_bx, rg_lam, rg_w_out, da_wq, da_wk, da_wv, da_wo, da_lam, da_subln_g, ml_w_up, ml_conv_w, ml_conv_b, ml_wq, ml_wk, ml_wv, ml_w_gate, ml_b_gate, ml_w_o, ml_b_o, ml_gn, ml_skip, ml_w_down):
    bc, lc, d = x_prompt.shape
    bl, ll, _ = x_sample.shape
    depth = ada_w.shape[0]
    cs = Stream(bc, lc, 0, False)
    ls = Stream(bl, ll, 1, True)
    ctx = x_prompt.reshape(cs.n, d)
    lat = x_sample.reshape(ls.n, d)
    c_all = jnp.concatenate([c_ctx[None, :], c, jnp.zeros((MOD_ROWS - 1 - bl, d), F32)], axis=0)
    mods = adaln(c_all, ada_w, ada_b)
    bf = lambda a: a.astype(BF16)
    ffn1 = (bf(ffn1_w1), bf(ffn1_w3), bf(ffn1_w2))
    ffn2 = (bf(ffn2_w1), bf(ffn2_w3), bf(ffn2_w2))

    new_s5 = new_rg = None
    for i in range(depth):
        ctx = ffn_half(ctx, norm_g[i, 0], mods, i, 0, *ffn1, cs)
        lat = ffn_half(lat, norm_g[i, 0], mods, i, 0, *ffn1, ls)
        hc = modnorm(ctx, norm_g[i, 1], mods, i, 3, cs)
        hl = modnorm(lat, norm_g[i, 1], mods, i, 3, ls)
        kind = i % 4
        if kind == 0:
            ops = _s5_slab_operators(s5_a_re, s5_a_im, s5_log_dt, s5_b_re, s5_b_im, s5_c_re, s5_c_im)
            p = s5_a_re.shape[2]
            slab_groups = LANES // s5_b_re.shape[3]
            args = (bf(s5_w_in), ops, s5_d, bf(s5_w_glu), s5_b_glu, bf(s5_w_out))
            zero_state = jnp.zeros((bc,) + state_s5.shape[1:], F32)
            ctx, fin = s5_slab_mixer(hc, ctx, mods, i, cs, s5_pack_state(zero_state, slab_groups), *args)
            lat, _ = s5_slab_mixer(hl, lat, mods, i, ls, s5_pack_state(state_s5, slab_groups), *args)
            new_s5 = s5_unpack_state(fin, slab_groups, p)
        elif kind == 1:
            w_gates = bf(jnp.concatenate([rg_wa, rg_wx], axis=-1))
            bias = jnp.stack([rg_ba, rg_bx], axis=1).astype(F32)
            args = (bf(rg_w_in), bf(rg_w_gate), rg_conv_w, rg_conv_b, w_gates, bias, rg_lam, bf(rg_w_out))
            ctx, new_rg = rg_mixer(hc, ctx, mods, i, cs, jnp.zeros((bc, 2, rg_lam.shape[1]), F32), *args)
            lat, _ = rg_mixer(hl, lat, mods, i, ls, state_rglru, *args)
        elif kind == 2:
            lam_init = 0.8 - 0.6 * math.exp(-0.3 * i)
            heads, dv = cache_dattn_v.shape[2], cache_dattn_v.shape[3]
            args = (bf(da_wq).T, bf(da_wk), bf(da_wv), bf(da_wv).T, bf(da_wo), da_lam, da_subln_g, lam_init)
            ctx, (k32, v32) = dattn_mixer(hc, ctx, mods, i, cs, None, *args)
            new_k = k32.reshape(bc, lc, heads, 2, dv // 2)
            new_v = v32.reshape(bc, lc, heads, dv)
            cache = (bf(cache_dattn_k).reshape(-1, heads * dv), bf(cache_dattn_v).transpose(0, 2, 3, 1))
            lat, _ = dattn_mixer(hl, lat, mods, i, ls, cache, *args)
        else:
            heads, dh = ml_wq.shape[0], ml_wq.shape[1]
            args = (bf(ml_w_up), ml_conv_w, ml_conv_b, bf(ml_wq).reshape(heads * dh, dh),
                    bf(ml_wk).reshape(heads * dh, dh), bf(ml_wk).transpose(0, 2, 1).reshape(heads * dh, dh),
                    bf(ml_wv).reshape(heads * dh, dh), bf(ml_w_gate), ml_b_gate,
                    bf(ml_w_o), ml_b_o, ml_gn, ml_skip, bf(ml_w_down))
            ctx, (cf, nf, mf) = mlstm_mixer(hc, ctx, mods, i, cs, None, True, *args)
            init = (state_mlstm_C.astype(F32), state_mlstm_n.astype(F32).reshape(bl, 2, heads, 1, dh),
                    state_mlstm_m.astype(F32).reshape(bl, 2, heads, 1, 1))
            lat, _ = mlstm_mixer(hl, lat, mods, i, ls, init, False, *args)
            new_c, new_n, new_m = cf, nf.reshape(bc, 2, heads, dh), mf.reshape(bc, 2, heads)
        ctx = ffn_half(ctx, norm_g[i, 2], mods, i, 6, *ffn2, cs)
        lat = ffn_half(lat, norm_g[i, 2], mods, i, 6, *ffn2, ls)
    y_prompt = rmsnorm(ctx, final_norm_g).reshape(bc, lc, d)
    y_sample = rmsnorm(lat, final_norm_g).reshape(bl, ll, d)
    return (y_prompt, y_sample, new_s5, new_rg, new_k, new_v, new_c, new_n, new_m)
```

```python
import functools
import math

import jax
import jax.numpy as jnp
from jax import lax
from jax.experimental import pallas as pl
from jax.experimental.pallas import tpu as pltpu

F32 = jnp.float32
BF16 = jnp.bfloat16

NORM_EPS = 1e-6
VMEM_LIMIT_BYTES = 56 * 1024 * 1024
MOD_ROWS = 8
ROW_TILE = 512
FFN_ROW_TILE = 1024
FFN_HIDDEN_TILE = 256
ADALN_TILE = 1024


def _cparams(*sem):
    return pltpu.CompilerParams(dimension_semantics=sem, vmem_limit_bytes=VMEM_LIMIT_BYTES)


def _bdot(a, b):
    return jnp.dot(a.astype(BF16), b.astype(BF16), preferred_element_type=F32)


def _mod_row(ref, row):
    return ref[pl.ds(row, 1), :]


class Stream:
    def __init__(self, batch, seq, row0, per_batch_mod):
        self.batch, self.seq, self.row0 = batch, seq, row0
        self.n = batch * seq
        self.rows_per_mod = seq if per_batch_mod else self.n

    def mod_row(self, first_token):
        return self.row0 + first_token // self.rows_per_mod


def _adaln_kernel(c_ref, w_ref, b_ref, o_ref):
    c = c_ref[...]
    s = c * jax.nn.sigmoid(c)
    o_ref[...] = _bdot(s, w_ref[...]) + b_ref[...]


def adaln(c_all, ada_w, ada_b, *, tn=ADALN_TILE):
    nl, d, m = ada_w.shape
    return pl.pallas_call(
        _adaln_kernel,
        out_shape=jax.ShapeDtypeStruct((nl, MOD_ROWS, m), F32),
        grid=(nl, m // tn),
        in_specs=[pl.BlockSpec((MOD_ROWS, d), lambda l, j: (0, 0)),
                  pl.BlockSpec((None, d, tn), lambda l, j: (l, 0, j)),
                  pl.BlockSpec((None, 1, tn), lambda l, j: (l, 0, j))],
        out_specs=pl.BlockSpec((None, MOD_ROWS, tn), lambda l, j: (l, 0, j)),
        compiler_params=_cparams("parallel", "parallel"),
        name="adaln",
    )(c_all, ada_w, ada_b.reshape(nl, 1, m))


def _mod_spec(d, layer, k):
    return pl.BlockSpec((None, MOD_ROWS, d), lambda *_: (layer, 0, k))


def _rms(x, g):
    return x * lax.rsqrt(jnp.mean(x * x, axis=-1, keepdims=True) + NORM_EPS) * g


def _modnorm_kernel(x_ref, g_ref, sh_ref, sc_ref, o_ref, *, stream, tm):
    row = stream.mod_row(pl.program_id(0) * tm)
    y = _rms(x_ref[...], g_ref[...])
    o_ref[...] = (y * (1.0 + _mod_row(sc_ref, row)) + _mod_row(sh_ref, row)).astype(o_ref.dtype)


def modnorm(x, g, mods, layer, k_shift, stream, *, tm=ROW_TILE):
    n, d = x.shape
    return pl.pallas_call(
        functools.partial(_modnorm_kernel, stream=stream, tm=tm),
        out_shape=jax.ShapeDtypeStruct((n, d), BF16),
        grid=(n // tm,),
        in_specs=[pl.BlockSpec((tm, d), lambda i: (i, 0)),
                  pl.BlockSpec((1, d), lambda i: (0, 0)),
                  _mod_spec(d, layer, k_shift),
                  _mod_spec(d, layer, k_shift + 1)],
        out_specs=pl.BlockSpec((tm, d), lambda i: (i, 0)),
        compiler_params=_cparams("parallel"),
        name="modnorm",
    )(x, g.reshape(1, d), mods, mods)


def _rmsnorm_kernel(x_ref, g_ref, o_ref):
    o_ref[...] = _rms(x_ref[...], g_ref[...])


def rmsnorm(x, g, *, tm=ROW_TILE):
    n, d = x.shape
    return pl.pallas_call(
        _rmsnorm_kernel,
        out_shape=jax.ShapeDtypeStruct((n, d), F32),
        grid=(n // tm,),
        in_specs=[pl.BlockSpec((tm, d), lambda i: (i, 0)), pl.BlockSpec((1, d), lambda i: (0, 0))],
        out_specs=pl.BlockSpec((tm, d), lambda i: (i, 0)),
        compiler_params=_cparams("parallel"),
        name="rmsnorm",
    )(x, g.reshape(1, d))


def _ffn_kernel(x_ref, g_ref, sh_ref, sc_ref, gt_ref, w1_ref, w3_ref, w2_ref, o_ref, h_ref, *, stream, tm):
    f = pl.program_id(1)
    row = stream.mod_row(pl.program_id(0) * tm)

    @pl.when(f == 0)
    def _():
        y = _rms(x_ref[...], g_ref[...])
        h_ref[...] = (y * (1.0 + _mod_row(sc_ref, row)) + _mod_row(sh_ref, row)).astype(BF16)
        o_ref[...] = jnp.zeros_like(o_ref)

    h = h_ref[...]
    a = jnp.dot(h, w1_ref[...], preferred_element_type=F32)
    b = jnp.dot(h, w3_ref[...], preferred_element_type=F32)
    u = (a * jax.nn.sigmoid(a)) * b
    o_ref[...] += jnp.dot(u.astype(BF16), w2_ref[...], preferred_element_type=F32)

    @pl.when(f == pl.num_programs(1) - 1)
    def _():
        o_ref[...] = x_ref[...] + (0.5 * _mod_row(gt_ref, row)) * o_ref[...]


def ffn_half(x, g, mods, layer, k_shift, w1, w3, w2, stream, *, tm=FFN_ROW_TILE, tf=FFN_HIDDEN_TILE):
    n, d = x.shape
    dff = w1.shape[2]
    return pl.pallas_call(
        functools.partial(_ffn_kernel, stream=stream, tm=tm),
        out_shape=jax.ShapeDtypeStruct((n, d), F32),
        grid=(n // tm, dff // tf),
        in_specs=[pl.BlockSpec((tm, d), lambda i, f: (i, 0)),
                  pl.BlockSpec((1, d), lambda i, f: (0, 0)),
                  _mod_spec(d, layer, k_shift),
                  _mod_spec(d, layer, k_shift + 1),
                  _mod_spec(d, layer, k_shift + 2),
                  pl.BlockSpec((None, d, tf), lambda i, f: (layer, 0, f)),
                  pl.BlockSpec((None, d, tf), lambda i, f: (layer, 0, f)),
                  pl.BlockSpec((None, tf, d), lambda i, f: (layer, f, 0))],
        out_specs=pl.BlockSpec((tm, d), lambda i, f: (i, 0)),
        scratch_shapes=[pltpu.VMEM((tm, d), BF16)],
        compiler_params=_cparams("parallel", "arbitrary"),
        name="ffn_half",
    )(x, g.reshape(1, d), mods, mods, mods, w1, w3, w2)


def _mm_kernel(x_ref, w_ref, *rest, epilogue, n_extra, slab_out):
    extra, o_refs = rest[:n_extra], rest[n_extra:]
    w = w_ref[...]
    tm = x_ref.shape[0]
    chunk = min(MM_ROW_CHUNK, tm)
    for r in range(tm // chunk):
        rows = slice(r * chunk, (r + 1) * chunk)
        acc = _bdot(x_ref[rows, :], w)
        outs = epilogue(acc, rows, *extra) if epilogue is not None else acc
        if not isinstance(outs, tuple):
            outs = (outs,)
        for o_ref, o in zip(o_refs, outs):
            if slab_out:
                o_ref[:, rows, :] = o.astype(o_ref.dtype)
            else:
                o_ref[rows, :] = o.astype(o_ref.dtype)


MM_TM = 1024
MM_TN = 1024
MM_K_WIDE = 2048
MM_ROW_CHUNK = 512
MM_TM_DIAG = 2048


def matmul(x, w, *, tm=MM_TM, tn=MM_TN, out_dtype=F32, epilogue=None, extras=(), slab_out=False, name="matmul"):
    n, kb = x.shape
    m = w.shape[1]
    tm = min(tm, n)
    dts = out_dtype if isinstance(out_dtype, tuple) else (out_dtype,)
    if slab_out:
        out_shape = tuple(jax.ShapeDtypeStruct((m // 128, n, 128), dt) for dt in dts)
        out_specs = tuple(pl.BlockSpec((tn // 128, tm, 128), lambda i, j: (j, i, 0)) for _ in dts)
    else:
        out_shape = tuple(jax.ShapeDtypeStruct((n, m), dt) for dt in dts)
        out_specs = tuple(pl.BlockSpec((tm, tn), lambda i, j: (i, j)) for _ in dts)
    res = pl.pallas_call(
        functools.partial(_mm_kernel, epilogue=epilogue, n_extra=len(extras), slab_out=slab_out),
        out_shape=out_shape,
        grid=(n // tm, m // tn),
        in_specs=[pl.BlockSpec((tm, kb), lambda i, j: (i, 0)), pl.BlockSpec((kb, tn), lambda i, j: (0, j))]
        + [pl.BlockSpec(bs, im) for _, bs, im in extras],
        out_specs=out_specs,
        compiler_params=_cparams("parallel", "parallel"),
        name=name,
    )(x, w, *[a for a, _, _ in extras])
    return res if isinstance(out_dtype, tuple) else res[0]


def _mm_t_kernel(x_ref, w_ref, *rest, epilogue, n_extra):
    extra, o_ref = rest[:n_extra], rest[n_extra]
    acc = lax.dot_general(w_ref[...], x_ref[...].astype(BF16), (((1,), (1,)), ((), ())), preferred_element_type=F32)
    o_ref[...] = (epilogue(acc, *extra) if epilogue is not None else acc).astype(o_ref.dtype)


def matmul_t(x, w_t, *, tm=MM_TM, tn=MM_TN, out_dtype=BF16, epilogue=None, extras=(), name="matmul_t"):
    n, kb = x.shape
    m = w_t.shape[0]
    tm = min(tm, n)
    return pl.pallas_call(
        functools.partial(_mm_t_kernel, epilogue=epilogue, n_extra=len(extras)),
        out_shape=jax.ShapeDtypeStruct((m, n), out_dtype),
        grid=(n // tm, m // tn),
        in_specs=[pl.BlockSpec((tm, kb), lambda i, j: (i, 0)), pl.BlockSpec((tn, kb), lambda i, j: (j, 0))]
        + [pl.BlockSpec(bs, im) for _, bs, im in extras],
        out_specs=pl.BlockSpec((tn, tm), lambda i, j: (j, i)),
        compiler_params=_cparams("parallel", "parallel"),
        name=name,
    )(x, w_t, *[a for a, _, _ in extras])


def _residual_epilogue(acc, rows, x_ref, gt_ref, *, stream, tm):
    row = stream.mod_row(pl.program_id(0) * tm)
    return x_ref[rows, :] + _mod_row(gt_ref, row) * acc


def out_proj_residual(z, w, x, mods, layer, stream, *, name="out_proj"):
    d = x.shape[1]
    tm = min(MM_TM, x.shape[0])
    tn = MM_TN if z.shape[1] <= MM_K_WIDE else MM_TN // 2
    return matmul(z, w, tm=tm, tn=tn, name=name,
                  epilogue=functools.partial(_residual_epilogue, stream=stream, tm=tm),
                  extras=[(x, (tm, tn), lambda i, j: (i, j)),
                          (mods, (None, MOD_ROWS, tn), lambda i, j: (layer, 0, 5 * (d // tn) + j))])


def _glu_epilogue(acc, rows, z_ref, b_ref):
    return z_ref[rows, :] * jax.nn.sigmoid(acc + b_ref[...])


SUBLANES = 8
LANES = 128
SCAN_BLOCK = 256
SEQ_BLOCK_ELEMS = 1 << 20


def _conv_rows(x_ref, t0, tb, seq, cw, cb):
    x = x_ref[pl.ds(t0, tb), :]
    prev = x_ref[pl.ds(pl.multiple_of(jnp.maximum(t0 - SUBLANES, 0), SUBLANES), SUBLANES), :]
    nxt = x_ref[pl.ds(pl.multiple_of(jnp.minimum(t0 + tb, seq - SUBLANES), SUBLANES), SUBLANES), :]
    has_prev = t0 > 0
    has_next = t0 + tb < seq
    p1 = jnp.where(has_prev, prev[SUBLANES - 1:SUBLANES, :], 0.0)
    n0 = jnp.where(has_next, nxt[0:1, :], 0.0)
    n1 = jnp.where(has_next, nxt[1:2, :], 0.0)
    rows = lax.broadcasted_iota(jnp.int32, x.shape, 0)
    xm1 = jnp.where(rows == 0, p1, pltpu.roll(x, 1, axis=0))
    xp1 = jnp.where(rows == tb - 1, n0, pltpu.roll(x, tb - 1, axis=0))
    xp2 = jnp.where(rows == tb - 1, n1, jnp.where(rows == tb - 2, n0, pltpu.roll(x, tb - 2, axis=0)))
    return cb + xm1 * cw[0:1, :] + x * cw[1:2, :] + xp1 * cw[2:3, :] + xp2 * cw[3:4, :]


def _group_scan(a, b, reverse):
    tb, tc = a.shape
    a = a.reshape(tb // SUBLANES, SUBLANES, tc)
    b = b.reshape(tb // SUBLANES, SUBLANES, tc)
    pos = lax.broadcasted_iota(jnp.int32, a.shape, 1)
    s = 1
    while s < SUBLANES:
        ok = pos < SUBLANES - s if reverse else pos >= s
        shift = SUBLANES - s if reverse else s
        a_s, b_s = pltpu.roll(a, shift, axis=1), pltpu.roll(b, shift, axis=1)
        b = b + jnp.where(ok, a, 0.0) * b_s
        a = jnp.where(ok, a * a_s, a)
        s *= 2
    return a.reshape(tb, tc), b.reshape(tb, tc)


def _softplus(x):
    return jnp.maximum(x, 0.0) + jnp.log1p(jnp.exp(-jnp.abs(x)))


SLAB_CHUNK = 8
S5_RECUR_ROWS = 512


def _s5_slab_operators(a_re, a_im, log_dt, b_re, b_im, c_re, c_im):
    hp = lax.Precision.HIGHEST
    t = SLAB_CHUNK
    a_re, a_im, b_re, b_im, c_re, c_im = (z.astype(F32) for z in (a_re, a_im, b_re, b_im, c_re, c_im))
    ng, p = a_re.shape[1], a_re.shape[2]
    c = b_re.shape[3]
    gs = LANES // c
    ns = ng // gs
    dt = jnp.exp(log_dt.astype(F32))[..., None]
    mag = jnp.exp(a_re * dt)
    lr = mag * jnp.cos(a_im * dt)
    li = mag * jnp.sin(a_im * dt)
    den = a_re * a_re + a_im * a_im
    cr = ((lr - 1.0) * a_re + li * a_im) / den
    ci = (li * a_re - (lr - 1.0) * a_im) / den
    bbr = cr[..., None] * b_re - ci[..., None] * b_im
    bbi = cr[..., None] * b_im + ci[..., None] * b_re

    def powers(k):
        k = k.astype(F32)[None, None, :, None]
        pm = jnp.exp(k * (a_re * dt)[:, :, None, :])
        return pm * jnp.cos(k * (a_im * dt)[:, :, None, :]), pm * jnp.sin(k * (a_im * dt)[:, :, None, :])

    pr, pi = powers(jnp.arange(t + 1))
    pbr = (pr[:, :, :, :, None] * bbr[:, :, None] - pi[:, :, :, :, None] * bbi[:, :, None])
    pbi = (pr[:, :, :, :, None] * bbi[:, :, None] + pi[:, :, :, :, None] * bbr[:, :, None])
    flat = lambda z: z.transpose(0, 1, 3, 2, 4).reshape(2, ng, p, (t + 1) * c)
    m = (jnp.einsum('dgop,dgpx->dgox', c_re, flat(pbr), precision=hp)
         - jnp.einsum('dgop,dgpx->dgox', c_im, flat(pbi), precision=hp)).reshape(2, ng, c, t + 1, c)

    def block_diag(small, row_group, col_group, col_inner):
        rows, q = small.shape[-2], small.shape[-1]
        cols = jnp.arange(gs * q)
        expand = (jnp.arange(q)[:, None] == col_inner(cols)[None, :]).astype(BF16)
        wide = jnp.matmul(small.astype(BF16), expand, preferred_element_type=BF16)
        keep = row_group(jnp.arange(rows))[:, None] == col_group(cols)[None, :]
        return jnp.where(keep, wide, jnp.zeros((), BF16))

    s_idx = jnp.arange(t)[:, None]
    t_idx = jnp.arange(t)[None, :]
    lag = jnp.stack([jnp.clip(t_idx - s_idx, 0, t), jnp.clip(s_idx - t_idx, 0, t)])
    live = jnp.stack([t_idx >= s_idx, s_idx >= t_idx]).astype(F32)
    msel = sum((m[d][:, :, lag[d], :] * live[d][None, None, :, :, None]).transpose(0, 2, 3, 4, 1) for d in range(2))
    msel = msel.reshape(ns, gs, t, t, c, c).transpose(0, 2, 1, 4, 3, 5)
    in_group = lambda r: (r % LANES) // c
    out_col = lambda col: (col // LANES) * c + col % c
    state_group = lambda r: (r % (gs * p)) // p
    state_col = lambda col: (col // (gs * p)) * p + col % p
    toep = block_diag(msel.reshape(ns, t * LANES, t * c), in_group, in_group, out_col)
    pow_s = jnp.stack([t - 1 - jnp.arange(t), jnp.arange(t)])
    pb = jnp.stack([pbr, pbi], axis=2)
    s_sel = jnp.stack([pb[d][:, :, pow_s[d]] for d in range(2)])
    s_sel = s_sel.reshape(2, ns, gs, 2, t, p, c).transpose(0, 1, 4, 2, 6, 3, 5)
    s_op = block_diag(s_sel.reshape(2, ns, t * LANES, 2 * p), in_group, state_group, state_col)
    pow_w = jnp.stack([jnp.arange(t) + 1, t - jnp.arange(t)])
    zr = jnp.stack([c_re[d][:, None] * pr[d][:, pow_w[d]][:, :, None, :]
                    - c_im[d][:, None] * pi[d][:, pow_w[d]][:, :, None, :] for d in range(2)])
    zi = jnp.stack([c_re[d][:, None] * pi[d][:, pow_w[d]][:, :, None, :]
                    + c_im[d][:, None] * pr[d][:, pow_w[d]][:, :, None, :] for d in range(2)])
    w_sel = jnp.stack([zr, -zi], axis=2)
    w_sel = w_sel.reshape(2, ns, gs, 2, t, c, p).transpose(0, 1, 3, 2, 6, 4, 5)
    w_op = block_diag(w_sel.reshape(2, ns, 2 * gs * p, t * c), state_group, in_group, out_col)
    qr, qi = powers(t * (jnp.arange(SUBLANES) + 1))
    lanes = lambda z: z.reshape(2, ns, gs, SUBLANES, p).transpose(0, 1, 3, 2, 4).reshape(2, ns, SUBLANES, gs * p)
    pow_a = jnp.concatenate([lanes(qr), lanes(qr)], axis=-1)
    pow_b = jnp.concatenate([-lanes(qi), lanes(qi)], axis=-1)
    return toep, s_op, w_op, pow_a, pow_b


def _s5_state_kernel(u_ref, s_ref, o_ref):
    o_ref[...] = jnp.dot(u_ref[...], s_ref[...], preferred_element_type=F32)


def _s5_recur_slab_kernel(st_ref, pa_ref, pb_ref, h0_ref, hin_ref, fin_ref, loc_ref, *, nj):
    rows, w = loc_ref.shape
    half = w // 2
    ng = nj // SUBLANES
    swap = lambda z: jnp.concatenate([z[..., half:], z[..., :half]], axis=-1)
    for d in range(2):
        reverse = d == 1
        pa = pa_ref[d]
        pb = pb_ref[d]
        z = st_ref[d].reshape(rows // SUBLANES, SUBLANES, w)
        pos = lax.broadcasted_iota(jnp.int32, z.shape, 1)
        s = 1
        while s < SUBLANES:
            ok = pos < SUBLANES - s if reverse else pos >= s
            zs = pltpu.roll(z, SUBLANES - s if reverse else s, axis=1)
            z = z + jnp.where(ok, zs * pa[s - 1:s, :] + swap(zs) * pb[s - 1:s, :], 0.0)
            s *= 2
        loc_ref[...] = z.reshape(rows, w)
        row = lax.broadcasted_iota(jnp.int32, (SUBLANES, w), 0)
        if reverse:
            ca = jnp.concatenate([pa[SUBLANES - 1 - r:SUBLANES - r, :] for r in range(SUBLANES)], axis=0)
            cb = jnp.concatenate([pb[SUBLANES - 1 - r:SUBLANES - r, :] for r in range(SUBLANES)], axis=0)
        else:
            ca, cb = pa, pb

        for b in range(rows // nj):
            def group(i, carry, d=d, b=b, reverse=reverse, ca=ca, cb=cb):
                c, c_sw = carry
                r0 = pl.multiple_of(b * nj + (ng - 1 - i if reverse else i) * SUBLANES, SUBLANES)
                h = loc_ref[pl.ds(r0, SUBLANES), :] + c * ca + c_sw * cb
                shifted = pltpu.roll(h, SUBLANES - 1 if reverse else 1, axis=0)
                hin_ref[d, pl.ds(r0, SUBLANES), :] = jnp.where(row == (SUBLANES - 1 if reverse else 0), c, shifted)
                last = h[0:1, :] if reverse else h[SUBLANES - 1:SUBLANES, :]
                return last, swap(last)

            h0 = h0_ref[d, b]
            fin_ref[d, b] = lax.fori_loop(0, ng, group, (h0, swap(h0)))[0]


def _s5_out_kernel(u_ref, toep_ref, hin_ref, w_ref, o_ref):
    acc = jnp.dot(u_ref[...], toep_ref[...], preferred_element_type=F32)
    for d in range(2):
        acc = acc + jnp.dot(hin_ref[d].astype(BF16), w_ref[d], preferred_element_type=F32)
    o_ref[...] = acc


def s5_slab_scan(u16, h0, ops, batch, seq):
    toep, s_op, w_op, pow_a, pow_b = ops
    ns, n, _ = u16.shape
    t = SLAB_CHUNK
    kc = t * LANES
    w = s_op.shape[-1]
    nj = seq // t
    rows = n // t
    tr = min(rows, MM_TM)
    uc = u16.reshape(ns, rows, kc)
    st = pl.pallas_call(
        _s5_state_kernel,
        out_shape=jax.ShapeDtypeStruct((2, ns, rows, w), F32),
        grid=(ns, 2, rows // tr),
        in_specs=[pl.BlockSpec((None, tr, kc), lambda a, d, r: (a, r, 0)),
                  pl.BlockSpec((None, None, kc, w), lambda a, d, r: (d, a, 0, 0))],
        out_specs=pl.BlockSpec((None, None, tr, w), lambda a, d, r: (d, a, r, 0)),
        compiler_params=_cparams("parallel", "parallel", "parallel"),
        name="s5_state",
    )(uc, s_op)
    bb = max(1, min(batch, S5_RECUR_ROWS // nj))
    hin, fin = pl.pallas_call(
        functools.partial(_s5_recur_slab_kernel, nj=nj),
        out_shape=(jax.ShapeDtypeStruct((2, ns, rows, w), F32), jax.ShapeDtypeStruct((2, ns, batch, 1, w), F32)),
        grid=(ns, batch // bb),
        in_specs=[pl.BlockSpec((2, None, bb * nj, w), lambda a, b: (0, a, b, 0)),
                  pl.BlockSpec((2, None, SUBLANES, w), lambda a, b: (0, a, 0, 0)),
                  pl.BlockSpec((2, None, SUBLANES, w), lambda a, b: (0, a, 0, 0)),
                  pl.BlockSpec((2, None, bb, 1, w), lambda a, b: (0, a, b, 0, 0))],
        out_specs=(pl.BlockSpec((2, None, bb * nj, w), lambda a, b: (0, a, b, 0)),
                   pl.BlockSpec((2, None, bb, 1, w), lambda a, b: (0, a, b, 0, 0))),
        scratch_shapes=[pltpu.VMEM((bb * nj, w), F32)],
        compiler_params=_cparams("parallel", "parallel"),
        name="s5_recur",
    )(st, pow_a, pow_b, h0)
    y = pl.pallas_call(
        _s5_out_kernel,
        out_shape=jax.ShapeDtypeStruct((ns, rows, kc), F32),
        grid=(ns, rows // tr),
        in_specs=[pl.BlockSpec((None, tr, kc), lambda a, r: (a, r, 0)),
                  pl.BlockSpec((None, kc, kc), lambda a, r: (a, 0, 0)),
                  pl.BlockSpec((2, None, tr, w), lambda a, r: (0, a, r, 0)),
                  pl.BlockSpec((2, None, w, kc), lambda a, r: (0, a, 0, 0))],
        out_specs=pl.BlockSpec((None, tr, kc), lambda a, r: (a, r, 0)),
        compiler_params=_cparams("parallel", "parallel"),
        name="s5_out_chunks",
    )(uc, toep, hin, w_op)
    return y.reshape(ns, n, LANES), fin


def _slab_epilogue(acc, rows):
    parts = jnp.stack([acc[:, s * LANES:(s + 1) * LANES] for s in range(acc.shape[1] // LANES)])
    return parts, parts


def _s5_post_slab_kernel(u_ref, y_ref, d_ref, o_ref):
    for s in range(u_ref.shape[0]):
        cols = slice(s * LANES, (s + 1) * LANES)
        o_ref[:, cols] = jax.nn.gelu(d_ref[:, cols] * u_ref[s] + y_ref[s])


def s5_post_slab(u, y, d_skip, *, tm=ROW_TILE):
    ns, n, _ = u.shape
    w = ns * LANES
    slab = pl.BlockSpec((ns, tm, LANES), lambda i: (0, i, 0))
    return pl.pallas_call(
        _s5_post_slab_kernel,
        out_shape=jax.ShapeDtypeStruct((n, w), F32),
        grid=(n // tm,),
        in_specs=[slab, slab, pl.BlockSpec((1, w), lambda i: (0, 0))],
        out_specs=pl.BlockSpec((tm, w), lambda i: (i, 0)),
        compiler_params=_cparams("parallel"),
        name="s5_post",
    )(u, y, d_skip.reshape(1, w).astype(F32))


def s5_slab_mixer(h, x, mods, layer, stream, state, w_in, ops, d_skip, w_glu, b_glu, w_out):
    u, u16 = matmul(h, w_in, out_dtype=(F32, BF16), epilogue=_slab_epilogue, slab_out=True, name="s5_in")
    y, fin = s5_slab_scan(u16, state, ops, stream.batch, stream.seq)
    z = s5_post_slab(u, y, d_skip)
    w = z.shape[1]
    zz = matmul(z, w_glu, out_dtype=BF16, name="s5_glu", epilogue=_glu_epilogue,
                extras=[(z, (min(MM_TM, z.shape[0]), MM_TN), lambda i, j: (i, j)),
                        (b_glu.reshape(1, w).astype(F32), (1, MM_TN), lambda i, j: (0, j))])
    return out_proj_residual(zz, w_out, x, mods, layer, stream, name="s5_out"), fin


def s5_pack_state(state, slab_groups):
    b, _, _, g, p = state.shape
    s = state.astype(F32).reshape(b, 2, 2, g // slab_groups, slab_groups * p)
    return s.transpose(1, 3, 0, 2, 4).reshape(2, g // slab_groups, b, 1, 2 * slab_groups * p)


def s5_unpack_state(fin, slab_groups, p):
    _, ns, b, _, w = fin.shape
    s = fin.reshape(2, ns, b, 2, slab_groups * p).transpose(2, 0, 3, 1, 4)
    return s.reshape(b, 2, 2, ns * slab_groups, p)


RG_C = 8.0


def _rg_core_kernel(x_ref, gate_ref, cw_ref, cb_ref, w_ref, bias_ref, lam_ref, h0_ref, o_ref, fin_ref,
                    y_ref, xc_ref, a_ref, b_ref, *, seq):
    tb = a_ref.shape[0]
    tc = a_ref.shape[1]
    n_blocks = seq // tb
    n_groups = tb // SUBLANES
    cw = cw_ref[...]
    cb = cb_ref[...]
    for d in range(2):
        reverse = d == 1
        sp = _softplus(-lam_ref[d:d + 1, :])
        bias_a = bias_ref[d, 0:1, :]
        bias_x = bias_ref[d, 1:2, :]

        def block(k, carry, d=d, reverse=reverse, sp=sp, bias_a=bias_a, bias_x=bias_x):
            t0 = pl.multiple_of((n_blocks - 1 - k if reverse else k) * tb, tb)
            if reverse:
                xc = xc_ref[pl.ds(t0, tb), :]
            else:
                xc = _conv_rows(x_ref, t0, tb, seq, cw, cb)
                xc_ref[pl.ds(t0, tb), :] = xc
            pre = [_bdot(xc[:, n * LANES:(n + 1) * LANES], w_ref[d, n]) for n in range(tc // LANES)]
            r = jax.nn.sigmoid(jnp.concatenate([p[:, :LANES] for p in pre], axis=1) + bias_a)
            ig = jax.nn.sigmoid(jnp.concatenate([p[:, LANES:] for p in pre], axis=1) + bias_x)
            log_a = -RG_C * r * sp
            a = jnp.exp(log_a)
            bv = jnp.sqrt(-jnp.tanh(log_a) * (a * a + 1.0)) * (ig * xc)
            a_ref[...], b_ref[...] = _group_scan(a, bv, reverse)

            def group(i, c):
                r0 = pl.multiple_of((n_groups - 1 - i if reverse else i) * SUBLANES, SUBLANES)
                h = b_ref[pl.ds(r0, SUBLANES), :] + a_ref[pl.ds(r0, SUBLANES), :] * c
                rows = pl.ds(pl.multiple_of(t0 + r0, SUBLANES), SUBLANES)
                if reverse:
                    y_ref[rows, :] += h
                    return h[0:1, :]
                y_ref[rows, :] = h
                return h[SUBLANES - 1:SUBLANES, :]

            carry = lax.fori_loop(0, n_groups, group, carry)
            if reverse:
                o_ref[pl.ds(t0, tb), :] = (y_ref[pl.ds(t0, tb), :] * gate_ref[pl.ds(t0, tb), :]).astype(o_ref.dtype)
            return carry

        fin_ref[d:d + 1, :] = lax.fori_loop(0, n_blocks, block, h0_ref[d:d + 1, :])


def rg_core(xin, gate, conv_w, conv_b, w_gates, bias, lam, h0, stream, *, tc=2 * LANES):
    n, c = xin.shape
    seq, batch = stream.seq, stream.batch
    tb = min(SCAN_BLOCK, seq)
    nb = tc // LANES
    return pl.pallas_call(
        functools.partial(_rg_core_kernel, seq=seq),
        out_shape=(jax.ShapeDtypeStruct((n, c), BF16), jax.ShapeDtypeStruct((batch, 2, c), F32)),
        grid=(batch, c // tc),
        in_specs=[pl.BlockSpec((seq, tc), lambda b, j: (b, j)),
                  pl.BlockSpec((seq, tc), lambda b, j: (b, j)),
                  pl.BlockSpec((4, tc), lambda b, j: (0, j)),
                  pl.BlockSpec((1, tc), lambda b, j: (0, j)),
                  pl.BlockSpec((2, nb, LANES, 2 * LANES), lambda b, j: (0, j, 0, 0)),
                  pl.BlockSpec((2, 2, tc), lambda b, j: (0, 0, j)),
                  pl.BlockSpec((2, tc), lambda b, j: (0, j)),
                  pl.BlockSpec((None, 2, tc), lambda b, j: (b, 0, j))],
        out_specs=(pl.BlockSpec((seq, tc), lambda b, j: (b, j)),
                   pl.BlockSpec((None, 2, tc), lambda b, j: (b, 0, j))),
        scratch_shapes=[pltpu.VMEM((seq, tc), F32), pltpu.VMEM((seq, tc), F32),
                        pltpu.VMEM((tb, tc), F32), pltpu.VMEM((tb, tc), F32)],
        compiler_params=_cparams("parallel", "parallel"),
        name="rg_core",
    )(xin, gate, conv_w, conv_b.reshape(1, c), w_gates, bias, lam, h0)


def _gelu_epilogue(acc, rows):
    return jax.nn.gelu(acc)


def rg_mixer(h, x, mods, layer, stream, state, w_in, w_gate, conv_w, conv_b, w_gates, bias, lam, w_out):
    xin = matmul(h, w_in, name="rg_in")
    gate = matmul(h, w_gate, epilogue=_gelu_epilogue, name="rg_gate")
    yg, fin = rg_core(xin, gate, conv_w, conv_b, w_gates, bias, lam, state, stream)
    return out_proj_residual(yg, w_out, x, mods, layer, stream, name="rg_out"), fin


GRID_W = 64
ROPE_BASE = 10000.0
ATTN_TQ = 512
ATTN_TK = 512


def rope_tables(seq, dk):
    half = dk // 2
    quarter = half // 2
    pos = jnp.arange(seq)
    inv = ROPE_BASE ** (-jnp.arange(0, half, 2, dtype=F32) / half)
    ang_r = (pos // GRID_W).astype(F32)[:, None] * inv
    ang_c = (pos % GRID_W).astype(F32)[:, None] * inv
    cos = jnp.concatenate([jnp.cos(ang_r)] * 2 + [jnp.cos(ang_c)] * 2, axis=1)
    sin = jnp.concatenate([-jnp.sin(ang_r), jnp.sin(ang_r), -jnp.sin(ang_c), jnp.sin(ang_c)], axis=1)
    assert cos.shape == (seq, 4 * quarter)
    return cos, sin


def _rope_epilogue(acc, rows, cos_ref, sin_ref):
    cos = cos_ref[rows, :]
    sin = sin_ref[rows, :]
    dk = cos.shape[1]
    lane = lax.broadcasted_iota(jnp.int32, cos.shape, 1)
    first = (lane % (dk // 2)) < dk // 4
    outs = []
    for g in range(acc.shape[1] // dk):
        x = acc[:, g * dk:(g + 1) * dk]
        rot = jnp.where(first, pltpu.roll(x, dk - dk // 4, axis=1), pltpu.roll(x, dk // 4, axis=1))
        outs.append(x * cos + rot * sin)
    return jnp.concatenate(outs, axis=1)


def _attn_t_kernel(*refs, n_tiles, has_cache, lam_init, scale):
    if has_cache:
        qt_ref, k_ref, vt_ref, ck_ref, cvt_ref, lam_ref, g_ref, o_ref, s_ref, sc_ref = refs
    else:
        qt_ref, k_ref, vt_ref, lam_ref, g_ref, o_ref, s_ref = refs
    tq = qt_ref.shape[1]
    dk = qt_ref.shape[0] // 2
    dv = vt_ref.shape[0]
    tk = s_ref.shape[2]
    maps = range(2)
    qt = [qt_ref[c * dk:(c + 1) * dk, :] for c in maps]

    def scores(kt):
        return [jnp.dot(kt[:, c * dk:(c + 1) * dk], qt[c], preferred_element_type=F32) for c in maps]

    m = [jnp.full((1, tq), -jnp.inf, F32) for _ in maps]
    if has_cache:
        s = scores(ck_ref[...])
        for c in maps:
            sc_ref[c] = s[c]
            m[c] = jnp.maximum(m[c], jnp.max(s[c], axis=0, keepdims=True))

    for i in range(n_tiles):
        s = scores(k_ref[i * tk:(i + 1) * tk, :])
        for c in maps:
            s_ref[c, i] = s[c]
            m[c] = jnp.maximum(m[c], jnp.max(s[c], axis=0, keepdims=True))

    c2 = scale * math.log2(math.e)

    def accumulate(s, vt, st):
        new = []
        for c in maps:
            l, acc = st[c]
            p = jnp.exp2((s[c] - m[c]) * c2)
            new.append((l + jnp.sum(p, axis=0, keepdims=True),
                        acc + jnp.dot(vt, p.astype(BF16), preferred_element_type=F32)))
        return tuple(new)

    st = tuple((jnp.zeros((1, tq), F32), jnp.zeros((dv, tq), F32)) for _ in maps)
    if has_cache:
        st = accumulate([sc_ref[c] for c in maps], cvt_ref[...], st)

    for i in range(n_tiles):
        st = accumulate([s_ref[c, i] for c in maps], vt_ref[:, i * tk:(i + 1) * tk], st)
    lp = lam_ref[...]
    lam = (jnp.exp(jnp.sum(lp[0:1, :] * lp[1:2, :], axis=-1, keepdims=True))
           - jnp.exp(jnp.sum(lp[2:3, :] * lp[3:4, :], axis=-1, keepdims=True)) + lam_init)
    diff = st[0][1] / st[0][0] - lam * (st[1][1] / st[1][0])
    diff = diff * lax.rsqrt(jnp.mean(diff * diff, axis=0, keepdims=True) + NORM_EPS)
    o_ref[...] = (diff.T * g_ref[...] * (1.0 - lam_init)).astype(o_ref.dtype)


def diff_attention_t(q_t, k, v_t, *rest, stream, lam_init):
    has_cache = len(rest) == 4
    da_lam, subln_g = rest[-2], rest[-1]
    n = k.shape[0]
    dv = subln_g.shape[0]
    heads = v_t.shape[0] // dv
    batch, seq = stream.batch, stream.seq
    tq = min(ATTN_TQ, seq)
    tk = min(ATTN_TK, seq)
    n_tiles = seq // tk
    nq = seq // tq
    in_specs = [pl.BlockSpec((dv, tq), lambda b, h, i: (h, b * nq + i)),
                pl.BlockSpec((seq, dv), lambda b, h, i: (b, h)),
                pl.BlockSpec((dv, seq), lambda b, h, i: (h, b))]
    args = [q_t, k, v_t]
    scratch = [pltpu.VMEM((2, n_tiles, tk, tq), F32)]
    if has_cache:
        ck, cv_t = rest[0], rest[1]
        past = cv_t.shape[3]
        in_specs += [pl.BlockSpec((past, dv), lambda b, h, i: (b, h)),
                     pl.BlockSpec((None, None, dv, past), lambda b, h, i: (b, h, 0, 0))]
        args += [ck, cv_t]
        scratch += [pltpu.VMEM((2, past, tq), F32)]
    in_specs += [pl.BlockSpec(da_lam.shape, lambda b, h, i: (0, 0)), pl.BlockSpec((1, dv), lambda b, h, i: (0, 0))]
    args += [da_lam.astype(F32), subln_g.reshape(1, dv).astype(F32)]
    return pl.pallas_call(
        functools.partial(_attn_t_kernel, n_tiles=n_tiles, has_cache=has_cache, lam_init=lam_init,
                          scale=1.0 / math.sqrt(dv // 2)),
        out_shape=jax.ShapeDtypeStruct((n, heads * dv), BF16),
        grid=(batch, heads, nq),
        in_specs=in_specs,
        out_specs=pl.BlockSpec((tq, dv), lambda b, h, i: (b * nq + i, h)),
        scratch_shapes=scratch,
        compiler_params=_cparams("parallel", "parallel", "parallel"),
        name="diff_attention",
    )(*args)


def _rope_t_epilogue(acc, cos_ref, sin_ref):
    cos = cos_ref[...]
    sin = sin_ref[...]
    dk = cos.shape[0]
    row = lax.broadcasted_iota(jnp.int32, cos.shape, 0)
    first = (row % (dk // 2)) < dk // 4
    outs = []
    for g in range(acc.shape[0] // dk):
        x = acc[g * dk:(g + 1) * dk, :]
        rot = jnp.where(first, pltpu.roll(x, dk - dk // 4, axis=0), pltpu.roll(x, dk // 4, axis=0))
        outs.append(x * cos + rot * sin)
    return jnp.concatenate(outs, axis=0)


def _dup_epilogue(acc, rows):
    return acc, acc


def dattn_mixer(h, x, mods, layer, stream, cache, wq_t, wk, wv, wv_t, wo, da_lam, subln_g, lam_init):
    dk = subln_g.shape[0] // 2
    v_t = matmul_t(h, wv_t, name="da_v_t")
    if cache is None:
        q_t = matmul_t(h, wq_t, name="da_q_t")
        k32, k = matmul(h, wk, out_dtype=(F32, BF16), epilogue=_dup_epilogue, name="da_k")
        v32 = matmul(h, wv, name="da_v")
        o = diff_attention_t(q_t, k, v_t, da_lam, subln_g, stream=stream, lam_init=lam_init)
        new = (k32, v32)
    else:
        cos, sin = rope_tables(stream.seq, dk)
        tm = min(MM_TM, stream.seq)
        per = stream.seq // tm
        rope = [(cos, (tm, dk), lambda i, j: (i % per, 0)), (sin, (tm, dk), lambda i, j: (i % per, 0))]
        rope_t = [(cos.T, (dk, tm), lambda i, j: (0, i % per)), (sin.T, (dk, tm), lambda i, j: (0, i % per))]
        q_t = matmul_t(h, wq_t, tm=tm, epilogue=_rope_t_epilogue, extras=rope_t, name="da_q_t_rope")
        k = matmul(h, wk, tm=tm, out_dtype=BF16, epilogue=_rope_epilogue, extras=rope, name="da_k_rope")
        o = diff_attention_t(q_t, k, v_t, cache[0], cache[1], da_lam, subln_g, stream=stream, lam_init=lam_init)
        new = None
    return out_proj_residual(o, wo, x, mods, layer, stream, name="da_out"), new


ML_CHUNK = 256


def _ml_conv_kernel(x_ref, cw_ref, cb_ref, o32_ref, o16_ref, *, seq):
    tb = min(SCAN_BLOCK, seq)
    cw = cw_ref[...]
    cb = cb_ref[...]

    def body(i, carry):
        t0 = pl.multiple_of(i * tb, tb)
        y = _conv_rows(x_ref, t0, tb, seq, cw, cb)
        y = y * jax.nn.sigmoid(y)
        o32_ref[pl.ds(t0, tb), :] = y
        o16_ref[pl.ds(t0, tb), :] = y.astype(BF16)
        return carry

    lax.fori_loop(0, seq // tb, body, 0)


def ml_conv(xi, conv_w, conv_b, stream):
    n, w = xi.shape
    seq = stream.seq
    tc = min(w, MM_TM, SEQ_BLOCK_ELEMS // seq)
    return pl.pallas_call(
        functools.partial(_ml_conv_kernel, seq=seq),
        out_shape=(jax.ShapeDtypeStruct((n, w), F32), jax.ShapeDtypeStruct((n, w), BF16)),
        grid=(stream.batch, w // tc),
        in_specs=[pl.BlockSpec((seq, tc), lambda b, j: (b, j)),
                  pl.BlockSpec((conv_w.shape[0], tc), lambda b, j: (0, j)),
                  pl.BlockSpec((1, tc), lambda b, j: (0, j))],
        out_specs=(pl.BlockSpec((seq, tc), lambda b, j: (b, j)), pl.BlockSpec((seq, tc), lambda b, j: (b, j))),
        compiler_params=_cparams("parallel", "parallel"),
        name="ml_conv",
    )(xi, conv_w.astype(F32), conv_b.reshape(1, w).astype(F32))


def _ml_qkv_kernel(xc_ref, xi_ref, wq_ref, wk_ref, wkt_ref, wv_ref, wg_ref, bg_ref,
                   q_ref, k_ref, kt_ref, v_ref, g_ref, *, scale, n_heads):
    hd = pl.program_id(1)
    xc = xc_ref[...]
    q = jnp.dot(xc, wq_ref[...], preferred_element_type=F32).astype(BF16)
    k = (jnp.dot(xc, wk_ref[...], preferred_element_type=F32) * scale).astype(BF16)
    kt = lax.dot_general(wkt_ref[...], xc, (((1,), (1,)), ((), ())), preferred_element_type=F32)
    v = _bdot(xi_ref[...], wv_ref[...]).astype(BF16)
    q_ref[...] = q
    k_ref[...] = k
    kt_ref[...] = (kt * scale).astype(BF16)
    v_ref[...] = v
    part = (jnp.dot(q, wg_ref[0], preferred_element_type=F32) + jnp.dot(k, wg_ref[1], preferred_element_type=F32)
            + jnp.dot(v, wg_ref[2], preferred_element_type=F32))

    @pl.when(hd == 0)
    def _():
        g_ref[...] = part + bg_ref[...]

    @pl.when(hd > 0)
    def _():
        g_ref[...] += part

    @pl.when(hd == n_heads - 1)
    def _():
        g = g_ref[...]
        col = lax.broadcasted_iota(jnp.int32, g.shape, 1)
        is_forget = (col // n_heads) % 2 == 1
        g_ref[...] = jnp.where(is_forget, -_softplus(-g), g)


def ml_qkv(xc16, xi, wq, wk, wk_t, wv, w_gate, b_gate):
    n, w = xc16.shape
    dh = wq.shape[1]
    ng = w_gate.shape[2]
    tm = min(MM_TM_DIAG, n)
    tok = pl.BlockSpec((tm, dh), lambda i, h: (i, h))
    wsp = pl.BlockSpec((dh, dh), lambda i, h: (h, 0))
    tokens = jax.ShapeDtypeStruct((n, w), BF16)
    return pl.pallas_call(
        functools.partial(_ml_qkv_kernel, scale=1.0 / math.sqrt(dh), n_heads=w // dh),
        out_shape=(tokens, tokens, jax.ShapeDtypeStruct((w, n), BF16), tokens, jax.ShapeDtypeStruct((n, ng), F32)),
        grid=(n // tm, w // dh),
        in_specs=[tok, tok, wsp, wsp, wsp, wsp,
                  pl.BlockSpec((3, dh, ng), lambda i, h: (0, h, 0)), pl.BlockSpec((1, ng), lambda i, h: (0, 0))],
        out_specs=(tok, tok, pl.BlockSpec((dh, tm), lambda i, h: (h, i)), tok,
                   pl.BlockSpec((tm, ng), lambda i, h: (i, 0))),
        compiler_params=_cparams("parallel", "arbitrary"),
        name="ml_qkv",
    )(xc16, xi, wq, wk, wk_t, wv, w_gate, b_gate.reshape(1, ng).astype(F32))


ML_HEADS_PER_STEP = 2


def _mlstm_cell_kernel(*refs, n_heads, has_init, want_final):
    q_ref, k_ref, kt_ref, v_ref, g_ref, gt_ref = refs[:6]
    init_refs = refs[6:9] if has_init else None
    h_ref = refs[9 if has_init else 6]
    ct_s, n_s, m_s = refs[-3:]
    final_refs = refs[-6:-3] if want_final else None
    hb = ct_s.shape[0]
    dh = ct_s.shape[1]
    h0 = pl.program_id(1) * hb
    d = pl.program_id(2)
    ci = pl.program_id(3)
    reverse = d == 1
    t = q_ref.shape[0]

    @pl.when(ci == 0)
    def _():
        for j in range(hb):
            if has_init:
                ct_s[j] = init_refs[0][j].T
                n_s[j] = init_refs[1][j]
                m_s[j] = init_refs[2][j]
            else:
                ct_s[j] = jnp.zeros((dh, dh), F32)
                n_s[j] = jnp.zeros((1, dh), F32)
                m_s[j] = jnp.zeros((1, 1), F32)

    g = g_ref[...]
    gt = gt_ref[...]
    lane = lax.broadcasted_iota(jnp.int32, g.shape, 1)
    sub = lax.broadcasted_iota(jnp.int32, gt.shape, 0)
    r_idx = lax.broadcasted_iota(jnp.int32, (t, t), 0)
    s_idx = lax.broadcasted_iota(jnp.int32, (t, t), 1)
    ahead = (s_idx - r_idx) * jnp.where(reverse, -1, 1)
    mask = ahead <= 0
    mask_t = ahead >= 0
    row = lax.broadcasted_iota(jnp.int32, (t, 1), 0)
    last = row == jnp.where(reverse, 0, t - 1)

    for j in range(hb):
        cols = slice(j * dh, (j + 1) * dh)
        q = q_ref[:, cols]
        k = k_ref[:, cols]
        v = v_ref[:, cols]
        kt = kt_ref[cols, :]
        col_i = d * (2 * n_heads) + h0 + j
        col_f = col_i + n_heads
        ig_col = jnp.sum(jnp.where(lane == col_i, g, 0.0), axis=1, keepdims=True)
        lf_col = jnp.sum(jnp.where(lane == col_f, g, 0.0), axis=1, keepdims=True)
        ig_row = jnp.sum(jnp.where(sub == col_i, gt, 0.0), axis=0, keepdims=True)
        lf_row = jnp.sum(jnp.where(sub == col_f, gt, 0.0), axis=0, keepdims=True)
        b_col = jnp.sum(jnp.where(mask, lf_row, 0.0), axis=1, keepdims=True)
        b_row = jnp.sum(jnp.where(mask_t, lf_col, 0.0), axis=0, keepdims=True)
        m_prev = m_s[j]
        dlog = jnp.where(mask, b_col - b_row + ig_row, -jnp.inf)
        inter = b_col + m_prev
        m_t = jnp.maximum(inter, jnp.max(dlog, axis=1, keepdims=True))
        w_intra = jnp.exp(dlog - m_t)
        w_inter = jnp.exp(inter - m_t)
        s_mat = lax.dot_general(q, k, (((1,), (1,)), ((), ())), preferred_element_type=F32) * w_intra
        ct_old = ct_s[j]
        n_old = n_s[j]
        qc = jnp.dot(q, ct_old.astype(BF16), preferred_element_type=F32)
        num = jnp.dot(s_mat.astype(BF16), v, preferred_element_type=F32) + w_inter * qc
        n_rows = jnp.broadcast_to(n_old.astype(BF16), (SUBLANES, dh))
        qn = lax.dot_general(q, n_rows, (((1,), (1,)), ((), ())), preferred_element_type=F32)[:, 0:1]
        den = jnp.sum(s_mat, axis=1, keepdims=True) + w_inter * qn
        h_ref[:, cols] = num / jnp.maximum(jnp.abs(den), jnp.exp(-m_t))
        m_new = jnp.sum(jnp.where(last, m_t, 0.0), axis=0, keepdims=True)
        b_last = jnp.sum(jnp.where(last, b_col, 0.0), axis=0, keepdims=True)
        w_old = jnp.exp(b_last + m_prev - m_new)
        w_in = jnp.exp(b_last - b_col + ig_col - m_new)
        vw = (v.astype(F32) * w_in).astype(BF16)
        ct_s[j] = w_old * ct_old + jnp.dot(kt, vw, preferred_element_type=F32)
        n_s[j] = w_old * n_old + jnp.sum(w_in * k.astype(F32), axis=0, keepdims=True)
        m_s[j] = m_new

    if want_final:
        @pl.when(ci == pl.num_programs(3) - 1)
        def _():
            for j in range(hb):
                final_refs[0][j] = ct_s[j].T
                final_refs[1][j] = n_s[j]
                final_refs[2][j] = m_s[j]


def mlstm_cell(q, k, k_t, v, g, g_t, init, stream, heads, want_final):
    n, w = q.shape
    batch, seq = stream.batch, stream.seq
    dh = w // heads
    hb = ML_HEADS_PER_STEP
    t = min(ML_CHUNK, seq)
    nc = seq // t

    def blk(b, d, ci):
        return b * nc + ci + d * (nc - 1 - 2 * ci)

    tok = pl.BlockSpec((t, hb * dh), lambda b, h, d, ci: (blk(b, d, ci), h))
    st5 = lambda r, c: pl.BlockSpec((None, None, hb, r, c), lambda b, h, d, ci: (b, d, h, 0, 0))
    state_dims = ((dh, dh), (1, dh), (1, 1))
    out_shape = [jax.ShapeDtypeStruct((2, n, w), F32)]
    out_specs = [pl.BlockSpec((None, t, hb * dh), lambda b, h, d, ci: (d, blk(b, d, ci), h))]
    if want_final:
        out_shape += [jax.ShapeDtypeStruct((batch, 2, heads) + rc, F32) for rc in state_dims]
        out_specs += [st5(*rc) for rc in state_dims]
    res = pl.pallas_call(
        functools.partial(_mlstm_cell_kernel, n_heads=heads, has_init=init is not None, want_final=want_final),
        out_shape=tuple(out_shape),
        grid=(batch, heads // hb, 2, nc),
        in_specs=[tok, tok,
                  pl.BlockSpec((hb * dh, t), lambda b, h, d, ci: (h, blk(b, d, ci))),
                  tok,
                  pl.BlockSpec((t, g.shape[1]), lambda b, h, d, ci: (blk(b, d, ci), 0)),
                  pl.BlockSpec((g.shape[1], t), lambda b, h, d, ci: (0, blk(b, d, ci)))]
        + ([st5(*rc) for rc in state_dims] if init is not None else []),
        out_specs=tuple(out_specs),
        scratch_shapes=[pltpu.VMEM((hb,) + rc, F32) for rc in state_dims],
        compiler_params=_cparams("parallel", "parallel", "arbitrary", "arbitrary"),
        name="mlstm_cell",
    )(q, k, k_t, v, g, g_t, *(init or ()))
    return res[0], (tuple(res[1:]) if want_final else None)


def _ml_out_gate_epilogue(acc, rows, b_ref, h_ref, xc_ref, gn_ref, skip_ref):
    hc = jax.nn.sigmoid(acc + b_ref[...]) * (h_ref[0, rows, :] + h_ref[1, rows, :])
    hn = hc * lax.rsqrt(jnp.mean(hc * hc, axis=-1, keepdims=True) + NORM_EPS)
    return hn * gn_ref[...] + skip_ref[...] * xc_ref[rows, :]


def mlstm_mixer(h, x, mods, layer, stream, init, want_final, w_up, conv_w, conv_b, wq, wk, wk_t, wv, w_gate, b_gate,
                w_o, b_o, gn, skip, w_down):
    dh = wq.shape[1]
    xi = matmul(h, w_up, name="ml_up")
    xc, xc16 = ml_conv(xi, conv_w, conv_b, stream)
    q, k, k_t, v, g = ml_qkv(xc16, xi, wq, wk, wk_t, wv, w_gate, b_gate)
    w = xi.shape[1]
    hdir, final = mlstm_cell(q, k, k_t, v, g, g.T, init, stream, w // dh, want_final)
    tm = min(MM_TM, h.shape[0])
    vec = lambda a: (a.reshape(1, w).astype(F32), (1, dh), lambda i, j: (0, j))
    y = matmul(h, w_o, tm=tm, tn=dh, out_dtype=BF16, name="ml_o", epilogue=_ml_out_gate_epilogue,
               extras=[vec(b_o), (hdir, (2, tm, dh), lambda i, j: (0, i, j)), (xc, (tm, dh), lambda i, j: (i, j)),
                       vec(gn), vec(skip)])
    return out_proj_residual(y, w_down, x, mods, layer, stream, name="ml_down"), final


def kernel(x_prompt, x_sample, state_s5, state_rglru, cache_dattn_k, cache_dattn_v, state_mlstm_C, state_mlstm_n, state_mlstm_m, c, c_ctx, ada_w, ada_b, norm_g, ffn1_w1, ffn1_w3, ffn1_w2, ffn2_w1, ffn2_w3, ffn2_w2, final_norm_g, s5_w_in, s5_a_re, s5_a_im, s5_log_dt, s5_b_re, s5_b_im, s5_c_re, s5_c_im, s5_d, s5_w_glu, s5_b_glu, s5_w_out, rg_w_in, rg_w_gate, rg_conv_w, rg_conv_b, rg_wa, rg_ba, rg_wx, rg_bx, rg_lam, rg_w_out, da_wq, da_wk, da_wv, da_wo, da_lam, da_subln_g, ml_w_up, ml_conv_w, ml_conv_b, ml_wq, ml_wk, ml_wv, ml_w_gate, ml_b_gate, ml_w_o, ml_b_o, ml_gn, ml_skip, ml_w_down):
    bc, lc, d = x_prompt.shape
    bl, ll, _ = x_sample.shape
    depth = ada_w.shape[0]
    cs = Stream(bc, lc, 0, False)
    ls = Stream(bl, ll, 1, True)
    ctx = x_prompt.reshape(cs.n, d)
    lat = x_sample.reshape(ls.n, d)
    c_all = jnp.concatenate([c_ctx[None, :], c, jnp.zeros((MOD_ROWS - 1 - bl, d), F32)], axis=0)
    mods = adaln(c_all, ada_w, ada_b)
    bf = lambda a: a.astype(BF16)
    ffn1 = (bf(ffn1_w1), bf(ffn1_w3), bf(ffn1_w2))
    ffn2 = (bf(ffn2_w1), bf(ffn2_w3), bf(ffn2_w2))

    new_s5 = new_rg = None
    for i in range(depth):
        ctx = ffn_half(ctx, norm_g[i, 0], mods, i, 0, *ffn1, cs)
        lat = ffn_half(lat, norm_g[i, 0], mods, i, 0, *ffn1, ls)
        hc = modnorm(ctx, norm_g[i, 1], mods, i, 3, cs)
        hl = modnorm(lat, norm_g[i, 1], mods, i, 3, ls)
        kind = i % 4
        if kind == 0:
            ops = _s5_slab_operators(s5_a_re, s5_a_im, s5_log_dt, s5_b_re, s5_b_im, s5_c_re, s5_c_im)
            p = s5_a_re.shape[2]
            slab_groups = LANES // s5_b_re.shape[3]
            args = (bf(s5_w_in), ops, s5_d, bf(s5_w_glu), s5_b_glu, bf(s5_w_out))
            zero_state = jnp.zeros((bc,) + state_s5.shape[1:], F32)
            ctx, fin = s5_slab_mixer(hc, ctx, mods, i, cs, s5_pack_state(zero_state, slab_groups), *args)
            lat, _ = s5_slab_mixer(hl, lat, mods, i, ls, s5_pack_state(state_s5, slab_groups), *args)
            new_s5 = s5_unpack_state(fin, slab_groups, p)
        elif kind == 1:
            w_gates = bf(jnp.concatenate([rg_wa, rg_wx], axis=-1))
            bias = jnp.stack([rg_ba, rg_bx], axis=1).astype(F32)
            args = (bf(rg_w_in), bf(rg_w_gate), rg_conv_w, rg_conv_b, w_gates, bias, rg_lam, bf(rg_w_out))
            ctx, new_rg = rg_mixer(hc, ctx, mods, i, cs, jnp.zeros((bc, 2, rg_lam.shape[1]), F32), *args)
            lat, _ = rg_mixer(hl, lat, mods, i, ls, state_rglru, *args)
        elif kind == 2:
            lam_init = 0.8 - 0.6 * math.exp(-0.3 * i)
            heads, dv = cache_dattn_v.shape[2], cache_dattn_v.shape[3]
            args = (bf(da_wq).T, bf(da_wk), bf(da_wv), bf(da_wv).T, bf(da_wo), da_lam, da_subln_g, lam_init)
            ctx, (k32, v32) = dattn_mixer(hc, ctx, mods, i, cs, None, *args)
            new_k = k32.reshape(bc, lc, heads, 2, dv // 2)
            new_v = v32.reshape(bc, lc, heads, dv)
            cache = (bf(cache_dattn_k).reshape(-1, heads * dv), bf(cache_dattn_v).transpose(0, 2, 3, 1))
            lat, _ = dattn_mixer(hl, lat, mods, i, ls, cache, *args)
        else:
            heads, dh = ml_wq.shape[0], ml_wq.shape[1]
            args = (bf(ml_w_up), ml_conv_w, ml_conv_b, bf(ml_wq).reshape(heads * dh, dh),
                    bf(ml_wk).reshape(heads * dh, dh), bf(ml_wk).transpose(0, 2, 1).reshape(heads * dh, dh),
                    bf(ml_wv).reshape(heads * dh, dh), bf(ml_w_gate), ml_b_gate,
                    bf(ml_w_o), ml_b_o, ml_gn, ml_skip, bf(ml_w_down))
            ctx, (cf, nf, mf) = mlstm_mixer(hc, ctx, mods, i, cs, None, True, *args)
            init = (state_mlstm_C.astype(F32), state_mlstm_n.astype(F32).reshape(bl, 2, heads, 1, dh),
                    state_mlstm_m.astype(F32).reshape(bl, 2, heads, 1, 1))
            lat, _ = mlstm_mixer(hl, lat, mods, i, ls, init, False, *args)
            new_c, new_n, new_m = cf, nf.reshape(bc, 2, heads, dh), mf.reshape(bc, 2, heads)
        ctx = ffn_half(ctx, norm_g[i, 2], mods, i, 6, *ffn2, cs)
        lat = ffn_half(lat, norm_g[i, 2], mods, i, 6, *ffn2, ls)
    y_prompt = rmsnorm(ctx, final_norm_g).reshape(bc, lc, d)
    y_sample = rmsnorm(lat, final_norm_g).reshape(bl, ll, d)
    return (y_prompt, y_sample, new_s5, new_rg, new_k, new_v, new_c, new_n, new_m)
```
